```python
import math
import jax
import jax.numpy as jnp
from jax import lax
import numpy as np

D_MODEL = 1024
BATCH = 2
SEQ = 8192
DEPTH = 1
DEC_BATCH = 16
DEC_SEQ = 16
PAST_LEN = 1024

CHUNK = 64
MIX_WIDTH = D_MODEL
GDN_HEAD_DIM = 128
GDN_WIDTH = MIX_WIDTH // 2
GDN_HEADS = GDN_WIDTH // GDN_HEAD_DIM
GDN_CONV = 4
GDN_QKV = 3 * GDN_WIDTH
GDN_PROJ = GDN_QKV + GDN_WIDTH + 2 * GDN_HEADS
RWKV_HEAD_DIM = 64
RWKV_WIDTH = MIX_WIDTH - GDN_WIDTH
RWKV_HEADS = RWKV_WIDTH // RWKV_HEAD_DIM
DECAY_LORA = 64
AAA_LORA = 64
GATE_LORA = 160
RWKV_PROJ = 3 * RWKV_WIDTH + DECAY_LORA + AAA_LORA + GATE_LORA
P_IN = GDN_PROJ + RWKV_PROJ
FFN_DIM = 2816
FFN_CONV = 3
NORM_EPS = 1e-6
L2_EPS = 1e-6
RWKV_LN_EPS = 64e-5

kernel_name = 'hymba_gdn_rwkv7_convffn_stream_step'


def rms_norm(x, w, eps=NORM_EPS):
    x32 = x.astype(jnp.float32)
    y = x32 * lax.rsqrt(jnp.mean(x32 * x32, axis=-1, keepdims=True) + eps)
    return (y * w.astype(jnp.float32)).astype(x.dtype)


def l2_normalize(x):
    return x * lax.rsqrt(jnp.sum(x * x, axis=-1, keepdims=True) + L2_EPS)


def causal_dwconv(x, buf, w):
    width, chans = w.shape
    xp = jnp.concatenate([buf.astype(x.dtype), x], axis=1)
    y = lax.conv_general_dilated(xp, w.astype(x.dtype)[:, None, :], window_strides=(1,), padding='VALID',
                                 dimension_numbers=('NWC', 'WIO', 'NWC'), feature_group_count=chans)
    return y, xp[:, xp.shape[1] - (width - 1):]


def gdn_chunked(q, k, v, log_g, beta, s0):
    bsz, t_len, n_h, dk = q.shape
    dv = v.shape[-1]
    c = min(CHUNK, t_len)
    n = t_len // c

    def blocks(a):
        a = a.reshape((bsz, n, c) + a.shape[2:])
        return jnp.moveaxis(jnp.moveaxis(a, 1, 0), 3, 2)

    qb, kb, vb, bb = blocks(q), blocks(k), blocks(v), blocks(beta)
    gb = jnp.cumsum(blocks(log_g), axis=-1)
    causal = jnp.tril(jnp.ones((c, c), dtype=bool))
    strict = jnp.tril(jnp.ones((c, c), dtype=bool), -1)
    decay = jnp.exp(jnp.where(causal, gb[..., :, None] - gb[..., None, :], -jnp.inf))
    kbeta = kb * bb[..., None]
    lmat = jnp.where(strict, jnp.einsum('nbhid,nbhjd->nbhij', kbeta, kb) * decay, 0.0)
    a_mat = lmat + jnp.eye(c, dtype=lmat.dtype)
    rhs = jnp.concatenate([vb * bb[..., None], kbeta * jnp.exp(gb)[..., None]], axis=-1)
    sol = lax.linalg.triangular_solve(a_mat, rhs, left_side=True, lower=True, unit_diagonal=True)
    value, kcd = sol[..., :dv], sol[..., dv:]
    attn = jnp.einsum('nbhid,nbhjd->nbhij', qb, kb) * decay
    q_dec = qb * jnp.exp(gb)[..., None]
    k_dec = kb * jnp.exp(gb[..., -1:] - gb)[..., None]
    g_last = jnp.exp(gb[..., -1])

    def step(s, xs):
        val_c, kcd_c, q_c, k_c, attn_c, gl_c = xs
        v_new = val_c - jnp.einsum('bhcd,bhde->bhce', kcd_c, s)
        o_c = jnp.einsum('bhcd,bhde->bhce', q_c, s) + jnp.einsum('bhij,bhje->bhie', attn_c, v_new)
        s = s * gl_c[..., None, None] + jnp.einsum('bhcd,bhce->bhde', k_c, v_new)
        return s, o_c

    s_fin, o = lax.scan(step, s0, (value, kcd, q_dec, k_dec, attn, g_last))
    o = jnp.transpose(o, (1, 0, 3, 2, 4)).reshape(bsz, t_len, n_h, dv)
    return o, s_fin


def rwkv7_recurrence(r, log_w, k, v, a_vec, b_vec, s0):
    def step(s, xs):
        r_t, lw_t, k_t, v_t, a_t, b_t = xs
        sa = jnp.einsum('bhvk,bhk->bhv', s, a_t)
        s = (s * jnp.exp(lw_t)[:, :, None, :] + sa[..., :, None] * b_t[..., None, :]
             + v_t[..., :, None] * k_t[..., None, :])
        return s, jnp.einsum('bhvk,bhk->bhv', s, r_t)

    xs = (jnp.moveaxis(r, 1, 0), jnp.moveaxis(log_w, 1, 0), jnp.moveaxis(k, 1, 0),
          jnp.moveaxis(v, 1, 0), jnp.moveaxis(a_vec, 1, 0), jnp.moveaxis(b_vec, 1, 0))
    s_fin, y = lax.scan(step, s0, xs)
    return jnp.moveaxis(y, 0, 1), s_fin


def hybrid_layer(x, gdn_conv_buf, s_gdn, shift_buf, s_rwkv, ffn_buf,
                 norm_mix, w_in, gdn_conv_w, gdn_a_log, gdn_dt_bias, gdn_norm,
                 rwkv_mu, rwkv_w0, rwkv_w2, rwkv_a0, rwkv_a2, rwkv_g2, rwkv_k_k, rwkv_k_a, rwkv_r_k,
                 rwkv_ln_w, rwkv_ln_b, w_out, norm_ffn, w_up, ffn_conv_w, w_down):
    f32 = jnp.float32
    bsz, t_len, _ = x.shape
    h = rms_norm(x, norm_mix)
    p = h @ w_in
    p_gdn, p_rwkv = p[..., :GDN_PROJ], p[..., GDN_PROJ:]

    qkv, gdn_conv_new = causal_dwconv(p_gdn[..., :GDN_QKV], gdn_conv_buf, gdn_conv_w)
    qkv = jax.nn.silu(qkv.astype(f32))
    q, k, v = jnp.split(qkv, 3, axis=-1)
    gh = lambda a: a.reshape(bsz, t_len, GDN_HEADS, GDN_HEAD_DIM)
    q = l2_normalize(gh(q)) * (GDN_HEAD_DIM ** -0.5)
    k = l2_normalize(gh(k))
    v = gh(v)
    z = p_gdn[..., GDN_QKV:GDN_QKV + GDN_WIDTH].astype(f32)
    b_logit = p_gdn[..., GDN_QKV + GDN_WIDTH:GDN_QKV + GDN_WIDTH + GDN_HEADS].astype(f32)
    a_logit = p_gdn[..., GDN_QKV + GDN_WIDTH + GDN_HEADS:].astype(f32)
    beta = jax.nn.sigmoid(b_logit)
    log_g = -jnp.exp(gdn_a_log.astype(f32)) * jax.nn.softplus(a_logit + gdn_dt_bias.astype(f32))
    o_gdn, s_gdn_new = gdn_chunked(q, k, v, log_g, beta, s_gdn.astype(f32))
    o_gdn = (rms_norm(o_gdn, gdn_norm) * jax.nn.silu(gh(z))).reshape(bsz, t_len, GDN_WIDTH)

    ps, shift_new = causal_dwconv(p_rwkv, shift_buf, jnp.stack([rwkv_mu, 1.0 - rwkv_mu]))
    ps = ps.astype(f32)
    o1, o2, o3 = RWKV_WIDTH, 2 * RWKV_WIDTH, 3 * RWKV_WIDTH
    o4, o5 = o3 + DECAY_LORA, o3 + DECAY_LORA + AAA_LORA
    r, kr, vr = ps[..., :o1], ps[..., o1:o2], ps[..., o2:o3]
    wd, ad, gd = ps[..., o3:o4], ps[..., o4:o5], ps[..., o5:]
    w = -jax.nn.softplus(-(rwkv_w0.astype(f32) + jnp.tanh(wd) @ rwkv_w2.astype(f32))) - 0.5
    log_w = -jnp.exp(w)
    a = jax.nn.sigmoid(rwkv_a0.astype(f32) + ad @ rwkv_a2.astype(f32))
    g = jax.nn.sigmoid(gd) @ rwkv_g2.astype(f32)
    rh = lambda t: t.reshape(bsz, t_len, RWKV_HEADS, RWKV_HEAD_DIM)
    kk = l2_normalize(rh(kr * rwkv_k_k.astype(f32)))
    kr = kr * (1.0 + (a - 1.0) * rwkv_k_a.astype(f32))
    y, s_rwkv_new = rwkv7_recurrence(rh(r), rh(log_w), rh(kr), rh(vr), -kk, kk * rh(a), s_rwkv.astype(f32))
    y_mean = jnp.mean(y, axis=-1, keepdims=True)
    y_c = y - y_mean
    y_n = y_c * lax.rsqrt(jnp.mean(y_c * y_c, axis=-1, keepdims=True) + RWKV_LN_EPS)
    y_n = y_n.reshape(bsz, t_len, RWKV_WIDTH) * rwkv_ln_w.astype(f32) + rwkv_ln_b.astype(f32)
    bonus = jnp.sum(rh(r) * rh(kr) * rwkv_r_k.astype(f32), axis=-1, keepdims=True) * rh(vr)
    o_rwkv = (y_n + bonus.reshape(bsz, t_len, RWKV_WIDTH)) * g

    o = jnp.concatenate([o_gdn, o_rwkv], axis=-1).astype(x.dtype) @ w_out
    x = x + o

    h2 = rms_norm(x, norm_ffn)
    up = h2 @ w_up
    gate, ffn_new = causal_dwconv(up[..., :FFN_DIM], ffn_buf, ffn_conv_w)
    x = x + (jax.nn.silu(gate) * up[..., FFN_DIM:]) @ w_down
    new_states = (gdn_conv_new.astype(gdn_conv_buf.dtype), s_gdn_new.astype(s_gdn.dtype),
                  shift_new.astype(shift_buf.dtype), s_rwkv_new.astype(s_rwkv.dtype),
                  ffn_new.astype(ffn_buf.dtype))
    return x, new_states


def setup_inputs(seed: int = 0) -> dict:
    key = jax.random.key(seed)
    ks = iter(jax.random.split(key, 40))
    nrm = lambda shape, scale=1.0: scale * jax.random.normal(next(ks), shape, jnp.float32)
    uni = lambda shape, lo, hi: jax.random.uniform(next(ks), shape, jnp.float32, lo, hi)
    L = DEPTH
    dt = jnp.exp(uni((L, GDN_HEADS), math.log(1e-3), math.log(1e-1)))
    return {
        'x_prompt': nrm((BATCH, SEQ, D_MODEL)),
        'x_sample': nrm((DEC_BATCH, DEC_SEQ, D_MODEL)),
        'state_gdn_conv': nrm((L, DEC_BATCH, GDN_CONV - 1, GDN_QKV)),
        'state_gdn': nrm((L, DEC_BATCH, GDN_HEADS, GDN_HEAD_DIM, GDN_HEAD_DIM), 0.3),
        'state_rwkv_shift': nrm((L, DEC_BATCH, 1, RWKV_PROJ)),
        'state_rwkv': nrm((L, DEC_BATCH, RWKV_HEADS, RWKV_HEAD_DIM, RWKV_HEAD_DIM), 0.3),
        'state_ffn_conv': nrm((L, DEC_BATCH, FFN_CONV - 1, FFN_DIM)),
        'norm_mix': 1.0 + nrm((L, D_MODEL), 0.02),
        'w_in': nrm((L, D_MODEL, P_IN), D_MODEL ** -0.5),
        'gdn_conv_w': nrm((L, GDN_CONV, GDN_QKV), GDN_CONV ** -0.5),
        'gdn_a_log': jnp.log(uni((L, GDN_HEADS), 1.0, 16.0)),
        'gdn_dt_bias': dt + jnp.log(-jnp.expm1(-dt)),
        'gdn_norm': 1.0 + nrm((L, GDN_HEAD_DIM), 0.02),
        'rwkv_mu': uni((L, RWKV_PROJ), 0.0, 1.0),
        'rwkv_w0': uni((L, RWKV_WIDTH), -6.0, -1.0),
        'rwkv_w2': nrm((L, DECAY_LORA, RWKV_WIDTH), 0.1),
        'rwkv_a0': nrm((L, RWKV_WIDTH), 0.1),
        'rwkv_a2': nrm((L, AAA_LORA, RWKV_WIDTH), 0.1),
        'rwkv_g2': nrm((L, GATE_LORA, RWKV_WIDTH), 0.1),
        'rwkv_k_k': 0.85 + nrm((L, RWKV_WIDTH), 0.02),
        'rwkv_k_a': 1.0 + nrm((L, RWKV_WIDTH), 0.02),
        'rwkv_r_k': nrm((L, RWKV_HEADS, RWKV_HEAD_DIM), 0.1),
        'rwkv_ln_w': 1.0 + nrm((L, RWKV_WIDTH), 0.02),
        'rwkv_ln_b': nrm((L, RWKV_WIDTH), 0.02),
        'w_out': nrm((L, MIX_WIDTH, D_MODEL), MIX_WIDTH ** -0.5),
        'norm_ffn': 1.0 + nrm((L, D_MODEL), 0.02),
        'w_up': nrm((L, D_MODEL, 2 * FFN_DIM), D_MODEL ** -0.5),
        'ffn_conv_w': nrm((L, FFN_CONV, FFN_DIM), FFN_CONV ** -0.5),
        'w_down': nrm((L, FFN_DIM, D_MODEL), FFN_DIM ** -0.5),
        'norm_final': 1.0 + nrm((D_MODEL,), 0.02),
    }


def reference(x_prompt, x_sample, state_gdn_conv, state_gdn, state_rwkv_shift, state_rwkv, state_ffn_conv,
              norm_mix, w_in, gdn_conv_w, gdn_a_log, gdn_dt_bias, gdn_norm,
              rwkv_mu, rwkv_w0, rwkv_w2, rwkv_a0, rwkv_a2, rwkv_g2, rwkv_k_k, rwkv_k_a, rwkv_r_k,
              rwkv_ln_w, rwkv_ln_b, w_out, norm_ffn, w_up, ffn_conv_w, w_down, norm_final):
    caches = (state_gdn_conv, state_gdn, state_rwkv_shift, state_rwkv, state_ffn_conv)
    n_prompt = x_prompt.shape[0]
    new_p = ([], [], [], [], [])
    new_s = ([], [], [], [], [])
    y_p, y_s = x_prompt, x_sample
    for layer in range(DEPTH):
        params = (norm_mix[layer], w_in[layer], gdn_conv_w[layer], gdn_a_log[layer], gdn_dt_bias[layer],
                  gdn_norm[layer], rwkv_mu[layer], rwkv_w0[layer], rwkv_w2[layer], rwkv_a0[layer],
                  rwkv_a2[layer], rwkv_g2[layer], rwkv_k_k[layer], rwkv_k_a[layer], rwkv_r_k[layer],
                  rwkv_ln_w[layer], rwkv_ln_b[layer], w_out[layer], norm_ffn[layer], w_up[layer],
                  ffn_conv_w[layer], w_down[layer])
        zero_states = [jnp.zeros((n_prompt,) + c.shape[2:], c.dtype) for c in caches]
        y_p, st_p = hybrid_layer(y_p, *zero_states, *params)
        y_s, st_s = hybrid_layer(y_s, *[c[layer] for c in caches], *params)
        for i in range(5):
            new_p[i].append(st_p[i])
            new_s[i].append(st_s[i])
    y_prompt = rms_norm(y_p, norm_final)
    y_sample = rms_norm(y_s, norm_final)
    return (y_prompt, y_sample,
            jnp.stack(new_p[0]), jnp.stack(new_s[0]),
            jnp.stack(new_p[1]), jnp.stack(new_s[1]),
            jnp.stack(new_p[2]), jnp.stack(new_s[2]),
            jnp.stack(new_p[3]), jnp.stack(new_s[3]),
            jnp.stack(new_p[4]), jnp.stack(new_s[4]))
```

```python
import functools

import jax
import jax.numpy as jnp
from jax import lax
from jax.experimental import pallas as pl
from jax.experimental.pallas import tpu as pltpu

F32 = jnp.float32
BF16 = jnp.bfloat16

D_MODEL = 1024
GDN_HEAD_DIM = 128
GDN_HEADS = 4
GDN_WIDTH = GDN_HEADS * GDN_HEAD_DIM
GDN_QKV = 3 * GDN_WIDTH
GDN_CONV = 4
RWKV_HEAD_DIM = 64
RWKV_HEADS = 8
RWKV_WIDTH = RWKV_HEADS * RWKV_HEAD_DIM
RWKV_PAIRS = RWKV_HEADS // 2
DECAY_LORA = 64
AAA_LORA = 64
GATE_LORA = 160
RWKV_PROJ = 3 * RWKV_WIDTH + DECAY_LORA + AAA_LORA + GATE_LORA
FFN_DIM = 2816
FFN_CONV = 3
NORM_EPS = 1e-6
L2_EPS = 1e-6
RWKV_LN_EPS = 64e-5

LANES = 128
SUBLANES = 8
VMEM_LIMIT_BYTES = 56 * 1024 * 1024

LORA_PAD = 384
BA_PAD = LANES
P_PAD = GDN_QKV + GDN_WIDTH + 3 * RWKV_WIDTH + LORA_PAD + BA_PAD
OFF_QKV = 0
OFF_Z = OFF_QKV + GDN_QKV
OFF_RKV = OFF_Z + GDN_WIDTH
OFF_LORA = OFF_RKV + 3 * RWKV_WIDTH
OFF_BA = OFF_LORA + LORA_PAD

ROW_BLOCK = 256
GDN_CHUNK = 128
RWKV_CHUNK = 64
SAMPLE_SEQ = 16

NN = (((1,), (0,)), ((), ()))
NT = (((1,), (1,)), ((), ()))
TN = (((0,), (0,)), ((), ()))


def _mm(a, b, dims=NN):
    return lax.dot_general(a.astype(BF16), b.astype(BF16), dims, preferred_element_type=F32)


def _split3(a):
    hi = a.astype(BF16)
    r1 = a - hi.astype(F32)
    mid = r1.astype(BF16)
    lo = (r1 - mid.astype(F32)).astype(BF16)
    return hi, mid, lo


def _mm3(a, b, dims=NN):
    a_hi = a.astype(BF16)
    a_lo = (a - a_hi.astype(F32)).astype(BF16)
    b_hi = b.astype(BF16)
    b_lo = (b - b_hi.astype(F32)).astype(BF16)
    dot = functools.partial(lax.dot_general, dimension_numbers=dims, preferred_element_type=F32)
    return dot(a_hi, b_hi) + (dot(a_hi, b_lo) + dot(a_lo, b_hi))


def _mm_exact_lhs(mask_bf16, x, dims=NN):
    hi, mid, lo = _split3(x)
    dot = functools.partial(lax.dot_general, dimension_numbers=dims, preferred_element_type=F32)
    return dot(mask_bf16, hi) + (dot(mask_bf16, mid) + dot(mask_bf16, lo))


def _mm_exact_rhs(x, mask_bf16, dims=NN):
    hi, mid, lo = _split3(x)
    dot = functools.partial(lax.dot_general, dimension_numbers=dims, preferred_element_type=F32)
    return dot(hi, mask_bf16) + (dot(mid, mask_bf16) + dot(lo, mask_bf16))


def _iota(shape, dim):
    return lax.broadcasted_iota(jnp.int32, shape, dim)


def _sigmoid(x):
    return 1.0 / (1.0 + jnp.exp(-x))


def _silu(x):
    return x * _sigmoid(x)


def _softplus(x):
    return jnp.maximum(x, 0.0) + jnp.log1p(jnp.exp(-jnp.abs(x)))


def _rms_norm(x, w):
    return x * lax.rsqrt(jnp.mean(x * x, axis=-1, keepdims=True) + NORM_EPS) * w


def _shift_rows_carry(x, s, carry):
    xr = pltpu.roll(x, s, 0)
    cr = pltpu.roll(carry, s, 0)
    row = _iota((SUBLANES, x.shape[1]), 0)
    head = jnp.where(row < s, cr, xr[:SUBLANES])
    return jnp.concatenate([head, xr[SUBLANES:]], axis=0)


def _shift_rows_fix(x, s, fix, seq_len):
    xr = pltpu.roll(x, s, 0)
    row = _iota(x.shape, 0)
    return jnp.where((row & (seq_len - 1)) < s, fix, xr)


def _unit_lower_inverse(p, same_mask_fn, seq_len, mul):
    eye, off1 = same_mask_fn(0), same_mask_fn(1)
    t = jnp.where(eye, 1.0, 0.0) + jnp.where(off1, p, 0.0)
    s = 2
    while s < seq_len:
        b = jnp.where(same_mask_fn(s), p, 0.0)
        t = t + mul(mul(t, b), t)
        s *= 2
    return t


def _inproj_kernel(x_ref, nw_ref, w_ref, qkv_ref, z_ref, rkv_ref, lora_ref, ba_ref):
    h = _rms_norm(x_ref[...], nw_ref[...]).astype(BF16)
    for out_ref, off in ((qkv_ref, OFF_QKV), (z_ref, OFF_Z), (rkv_ref, OFF_RKV), (lora_ref, OFF_LORA),
                         (ba_ref, OFF_BA)):
        width = out_ref.shape[1]
        out_ref[...] = jnp.dot(h, w_ref[:, off:off + width], preferred_element_type=F32)


def _inproj(x2d, norm_w, w_perm):
    n_rows = x2d.shape[0]
    widths = (GDN_QKV, GDN_WIDTH, 3 * RWKV_WIDTH, LORA_PAD, BA_PAD)
    return pl.pallas_call(
        _inproj_kernel,
        grid=(n_rows // ROW_BLOCK,),
        in_specs=[
            pl.BlockSpec((ROW_BLOCK, D_MODEL), lambda i: (i, 0)),
            pl.BlockSpec((1, D_MODEL), lambda i: (0, 0)),
            pl.BlockSpec((D_MODEL, P_PAD), lambda i: (0, 0), pipeline_mode=pl.Buffered(1)),
        ],
        out_specs=[pl.BlockSpec((ROW_BLOCK, w), lambda i: (i, 0)) for w in widths],
        out_shape=[jax.ShapeDtypeStruct((n_rows, w), F32) for w in widths],
        compiler_params=pltpu.CompilerParams(dimension_semantics=("arbitrary",),
                                             vmem_limit_bytes=VMEM_LIMIT_BYTES),
        name="inproj",
    )(x2d, norm_w, w_perm)


def _gdn_chunk(q, k, v, beta, lg, seq_len, get_state, set_state):
    c = q.shape[0]
    i = _iota((c, c), 0)
    j = _iota((c, c), 1)
    same = ((i ^ j) < seq_len)
    incl = same & (j <= i)
    strict = same & (j < i)
    lg_row = jnp.sum(jnp.where(i == j, lg, 0.0), axis=0, keepdims=True)
    gb_row = jnp.sum(jnp.where(same & (i <= j), lg, 0.0), axis=0, keepdims=True)
    gb = jnp.sum(jnp.where(incl, lg_row, 0.0), axis=1, keepdims=True)
    gtot = jnp.sum(jnp.where(same, lg_row, 0.0), axis=1, keepdims=True)
    decay = jnp.exp(jnp.where(incl, gb - gb_row, -jnp.inf))
    kbeta = k * beta
    kq = _mm(jnp.concatenate([kbeta, q], axis=0), k, NT)
    lmat = jnp.where(strict, kq[:c] * decay, 0.0)
    attn = kq[c:] * decay
    e_g = jnp.exp(gb)

    def off_mask(s):
        if s == 0:
            return i == j
        return ((i ^ j) < 2 * s) & ((i & s) != 0) & ((j & s) == 0)

    t_inv = _unit_lower_inverse(-lmat, off_mask, seq_len, _mm3)
    sol = _mm3(t_inv, jnp.concatenate([v * beta, kbeta * e_g], axis=1))
    value, kcd = sol[:, :GDN_HEAD_DIM], sol[:, GDN_HEAD_DIM:]
    q_dec = q * e_g
    k_dec = k * jnp.exp(gtot - gb)
    g_last = jnp.exp(gtot)
    v_new_parts, qs_parts = [], []
    for n in range(c // seq_len):
        rows = slice(n * seq_len, (n + 1) * seq_len)
        s = get_state(n)
        vn = value[rows] - _mm(kcd[rows], s)
        qs_parts.append(_mm(q_dec[rows], s))
        v_new_parts.append(vn)
        set_state(n, s * g_last[n * seq_len:n * seq_len + 1] + _mm(k_dec[rows], vn, TN))
    v_new = v_new_parts[0] if len(v_new_parts) == 1 else jnp.concatenate(v_new_parts, axis=0)
    qs = qs_parts[0] if len(qs_parts) == 1 else jnp.concatenate(qs_parts, axis=0)
    return qs + _mm(attn, v_new)


def _gdn_kernel(*refs, rows, chunk, seq_len, carry_mode):
    q_ref, k_ref, v_ref, z_ref, ba_ref, wq_ref, wk_ref, wv_ref, gp_ref, nw_ref = refs[:10]
    n_halo = 3 if carry_mode else 9
    halo_refs = refs[10:10 + n_halo]
    s_in_ref = refs[10 + n_halo]
    o_ref, s_out_ref = refs[11 + n_halo], refs[12 + n_halo]
    scratch = refs[13 + n_halo:]
    head = pl.program_id(1)
    t_idx = pl.program_id(2)

    if carry_mode:
        s_scr, cq_scr, ck_scr, cv_scr = scratch
        carries = (cq_scr, ck_scr, cv_scr)

        @pl.when(t_idx == 0)
        def _():
            s_scr[...] = s_in_ref[...]
            for c_scr, h_ref in zip(carries, halo_refs):
                c_scr[...] = h_ref[...]

    def conv_silu(x_ref, w_ref, idx):
        x = x_ref[...]
        w = w_ref[...]
        acc = x * w[GDN_CONV - 1:GDN_CONV]
        for s in range(1, GDN_CONV):
            if carry_mode:
                xs = _shift_rows_carry(x, s, carries[idx][...])
            else:
                xs = _shift_rows_fix(x, s, halo_refs[3 * idx + s - 1][...], seq_len)
            acc = acc + xs * w[GDN_CONV - 1 - s:GDN_CONV - s]
        if carry_mode:
            carries[idx][...] = x[rows - SUBLANES:]
        return _silu(acc)

    q = conv_silu(q_ref, wq_ref, 0)
    k = conv_silu(k_ref, wk_ref, 1)
    v = conv_silu(v_ref, wv_ref, 2)
    q = q * lax.rsqrt(jnp.sum(q * q, axis=-1, keepdims=True) + L2_EPS) * (GDN_HEAD_DIM ** -0.5)
    k = k * lax.rsqrt(jnp.sum(k * k, axis=-1, keepdims=True) + L2_EPS)

    ba = ba_ref[...]
    gp = gp_ref[...]
    lane = _iota(ba.shape, 1)
    beta_all = _sigmoid(ba)
    lg_all = -jnp.exp(gp[0:1]) * _softplus(ba + gp[1:2])
    beta = jnp.sum(jnp.where(lane == head, beta_all, 0.0), axis=-1, keepdims=True)
    lg = jnp.sum(jnp.where(lane == head + GDN_HEADS, lg_all, 0.0), axis=-1, keepdims=True)

    outs = []
    for ci in range(rows // chunk):
        r = slice(ci * chunk, (ci + 1) * chunk)
        if carry_mode:
            get_state = lambda n: s_scr[...]
            set_state = lambda n, val: s_scr.__setitem__(Ellipsis, val)
        else:
            base = ci * (chunk // seq_len)
            get_state = lambda n, base=base: s_in_ref[base + n]
            set_state = lambda n, val, base=base: s_out_ref.__setitem__(base + n, val)
        outs.append(_gdn_chunk(q[r], k[r], v[r], beta[r], lg[r], seq_len, get_state, set_state))
    o = outs[0] if len(outs) == 1 else jnp.concatenate(outs, axis=0)
    o = _rms_norm(o, nw_ref[...])
    o_ref[...] = o * _silu(z_ref[...])

    if carry_mode:
        @pl.when(t_idx == pl.num_programs(2) - 1)
        def _():
            s_out_ref[...] = s_scr[...]


def _gdn_mixer(qkv, z, ba, conv_w, gparams, norm_w, halo, s_in, *, rows, chunk, seq_len, carry_mode):
    bsz, t_len, _ = qkv.shape
    nh = GDN_HEADS
    grid = (bsz, nh, t_len // rows)
    tok = lambda off: pl.BlockSpec((None, rows, LANES), lambda b, h, t, off=off: (b, t, off + h))
    par = lambda nrow, off: pl.BlockSpec((nrow, LANES), lambda b, h, t, off=off: (0, off + h))
    in_specs = [tok(0), tok(nh), tok(2 * nh), tok(0),
                pl.BlockSpec((None, rows, LANES), lambda b, h, t: (b, t, 0)),
                par(GDN_CONV, 0), par(GDN_CONV, nh), par(GDN_CONV, 2 * nh),
                pl.BlockSpec((2, LANES), lambda b, h, t: (0, 0)),
                pl.BlockSpec((1, LANES), lambda b, h, t: (0, 0))]
    args = [qkv, qkv, qkv, z, ba, conv_w, conv_w, conv_w, gparams, norm_w]
    if carry_mode:
        for idx in range(3):
            in_specs.append(pl.BlockSpec((None, SUBLANES, LANES), lambda b, h, t, off=idx * nh: (b, 0, off + h)))
            args.append(halo)
        state_spec = pl.BlockSpec((None, None, GDN_HEAD_DIM, GDN_HEAD_DIM), lambda b, h, t: (b, h, 0, 0))
        scratch = [pltpu.VMEM((GDN_HEAD_DIM, GDN_HEAD_DIM), F32)] + [pltpu.VMEM((SUBLANES, LANES), F32)] * 3
    else:
        for idx in range(3):
            for s in range(GDN_CONV - 1):
                in_specs.append(tok(idx * nh))
                args.append(halo[s])
        n_seq = rows // seq_len
        state_spec = pl.BlockSpec((n_seq, None, GDN_HEAD_DIM, GDN_HEAD_DIM), lambda b, h, t: (t, h, 0, 0))
        scratch = []
    in_specs.append(state_spec)
    args.append(s_in)
    return pl.pallas_call(
        functools.partial(_gdn_kernel, rows=rows, chunk=chunk, seq_len=seq_len, carry_mode=carry_mode),
        grid=grid,
        in_specs=in_specs,
        out_specs=[tok(0), state_spec],
        out_shape=[jax.ShapeDtypeStruct((bsz, t_len, GDN_WIDTH), F32),
                   jax.ShapeDtypeStruct(s_in.shape, F32)],
        scratch_shapes=scratch,
        compiler_params=pltpu.CompilerParams(dimension_semantics=("arbitrary",) * 3,
                                             vmem_limit_bytes=VMEM_LIMIT_BYTES),
        name="gdn_mixer",
    )(*args)


def _rwkv_chunk(r, lw, k, v, a, b, seq_len, get_state, set_state):
    c = r.shape[0]
    hd = RWKV_HEAD_DIM
    i1 = _iota((c, c), 0)
    j1 = _iota((c, c), 1)
    same1 = (i1 ^ j1) < seq_len
    incl1 = (same1 & (j1 <= i1)).astype(BF16)
    ip = _iota((c, 2 * c), 0)
    jp_full = _iota((c, 2 * c), 1)
    jp = jp_full & (c - 1)
    head1_cols = jp_full >= c
    same_p = (ip ^ jp) < seq_len
    incl_p = same_p & (jp <= ip)
    strict_p = same_p & (jp < ip)
    lane = _iota((c, 2 * hd), 1)
    head1_lanes = lane >= hd

    def stack_heads(x):
        return jnp.concatenate([jnp.where(head1_lanes, 0.0, x), jnp.where(head1_lanes, x, 0.0)], axis=0)

    def block_diag(m):
        return jnp.concatenate([jnp.where(head1_cols, 0.0, m), jnp.where(head1_cols, m, 0.0)], axis=0)

    g_incl = _mm_exact_lhs(incl1, lw)
    g_tot = _mm_exact_lhs(same1.astype(BF16), lw)
    g_excl = g_incl - lw
    g_mid = 0.5 * g_tot
    r_n = r * jnp.exp(g_incl - g_mid)
    a_n = a * jnp.exp(g_excl - g_mid)
    inv_n = jnp.exp(g_mid - g_incl)
    b_n = b * inv_n
    k_n = k * inv_n
    r_t = r * jnp.exp(g_incl)
    a_t = a * jnp.exp(g_excl)
    to_end = jnp.exp(g_tot - g_incl)
    b_end = b * to_end
    k_end = k * to_end

    ar = jnp.concatenate([a_n, r_n], axis=0)
    m_b = _mm(ar, stack_heads(b_n), NT)
    m_k = _mm(ar, stack_heads(k_n), NT)
    a_ab = jnp.where(strict_p, m_b[:c], 0.0)
    a_rb = jnp.where(incl_p, m_b[c:], 0.0)
    a_ak = jnp.where(strict_p, m_k[:c], 0.0)
    a_rk = jnp.where(incl_p, m_k[c:], 0.0)

    def off_mask(s):
        if s == 0:
            return ip == jp
        return ((ip ^ jp) < 2 * s) & ((ip & s) != 0) & ((jp & s) == 0)

    pair_mul = lambda x, y: _mm3(x, block_diag(y))
    t_inv = _unit_lower_inverse(a_ab, off_mask, seq_len, pair_mul)
    zv = _mm(a_ak, stack_heads(v))
    uw = _mm3(t_inv, jnp.concatenate([stack_heads(zv), stack_heads(a_t)], axis=1))
    u_free, w_state = uw[:, :2 * hd], uw[:, 2 * hd:]

    vi = _iota((2 * hd, 2 * hd), 0)
    ki = _iota((2 * hd, 2 * hd), 1)
    bd_state = (vi >= hd) == (ki >= hd)
    u_parts, y_parts = [], []
    for n in range(c // seq_len):
        rows = slice(n * seq_len, (n + 1) * seq_len)
        s = get_state(n)
        u = _mm3(w_state[rows], s, NT) + u_free[rows]
        y_parts.append(_mm3(r_t[rows], s, NT))
        u_parts.append(u)
        gam = jnp.exp(g_tot[n * seq_len:n * seq_len + 1])
        upd = _mm3(u, b_end[rows], TN) + _mm3(v[rows], k_end[rows], TN)
        set_state(n, s * gam + jnp.where(bd_state, upd, 0.0))
    u_all = u_parts[0] if len(u_parts) == 1 else jnp.concatenate(u_parts, axis=0)
    y_s = y_parts[0] if len(y_parts) == 1 else jnp.concatenate(y_parts, axis=0)
    return y_s + _mm3(a_rb, stack_heads(u_all)) + _mm3(a_rk, stack_heads(v))


def _rwkv_kernel(*refs, rows, chunk, seq_len, carry_mode):
    (r_ref, k_ref, v_ref, lora_ref, mur_ref, muk_ref, muv_ref, mul_ref, par_ref, w2_ref, a2_ref, g2_ref,
     hr_ref, hk_ref, hv_ref, hl_ref, s_in_ref, o_ref, s_out_ref) = refs[:19]
    scratch = refs[19:]
    t_idx = pl.program_id(2)
    hd = RWKV_HEAD_DIM

    if carry_mode:
        s_scr, cr_scr, ck_scr, cv_scr, cl_scr = scratch

        @pl.when(t_idx == 0)
        def _():
            s_scr[...] = s_in_ref[...]
            cr_scr[...] = hr_ref[...]
            ck_scr[...] = hk_ref[...]
            cv_scr[...] = hv_ref[...]
            cl_scr[...] = hl_ref[...]
        carries = (cr_scr, ck_scr, cv_scr, cl_scr)

    def shifted(x_ref, mu_ref, h_ref, idx):
        x = x_ref[...]
        mu = mu_ref[...]
        if carry_mode:
            prev = _shift_rows_carry(x, 1, carries[idx][...])
            carries[idx][...] = x[rows - SUBLANES:]
        else:
            prev = _shift_rows_fix(x, 1, h_ref[...], seq_len)
        return mu * prev + (1.0 - mu) * x

    r = shifted(r_ref, mur_ref, hr_ref, 0)
    kr = shifted(k_ref, muk_ref, hk_ref, 1)
    vr = shifted(v_ref, muv_ref, hv_ref, 2)
    lora = shifted(lora_ref, mul_ref, hl_ref, 3)
    par = par_ref[...]
    w0, a0, k_k, k_a, r_k, ln_w, ln_b = (par[n:n + 1] for n in range(7))
    wd = lora[:, :DECAY_LORA]
    ad = lora[:, DECAY_LORA:DECAY_LORA + AAA_LORA]
    gd = lora[:, DECAY_LORA + AAA_LORA:]
    w = -_softplus(-(w0 + _mm(jnp.tanh(wd), w2_ref[...]))) - 0.5
    lw = -jnp.exp(w)
    asig = _sigmoid(a0 + _mm(ad, a2_ref[...]))
    g = _mm(_sigmoid(gd), g2_ref[...])

    li = _iota((2 * hd, 2 * hd), 0)
    lj = _iota((2 * hd, 2 * hd), 1)
    head_ones = ((li >= hd) == (lj >= hd)).astype(BF16)
    head_sum = lambda x: _mm_exact_rhs(x, head_ones)

    kk = kr * k_k
    kk = kk * lax.rsqrt(head_sum(kk * kk) + L2_EPS)
    kr = kr * (1.0 + (asig - 1.0) * k_a)

    outs = []
    for ci in range(rows // chunk):
        rs = slice(ci * chunk, (ci + 1) * chunk)
        if carry_mode:
            get_state = lambda n: s_scr[...]
            set_state = lambda n, val: s_scr.__setitem__(Ellipsis, val)
        else:
            base = ci * (chunk // seq_len)
            get_state = lambda n, base=base: s_in_ref[base + n]
            set_state = lambda n, val, base=base: s_out_ref.__setitem__(base + n, val)
        outs.append(_rwkv_chunk(r[rs], lw[rs], kr[rs], vr[rs], -kk[rs], (kk * asig)[rs], seq_len,
                                get_state, set_state))
    y = outs[0] if len(outs) == 1 else jnp.concatenate(outs, axis=0)

    y_mean = head_sum(y) * (1.0 / hd)
    y_c = y - y_mean
    y_n = y_c * lax.rsqrt(head_sum(y_c * y_c) * (1.0 / hd) + RWKV_LN_EPS)
    y_n = y_n * ln_w + ln_b
    bonus = head_sum(r * kr * r_k) * vr
    o_ref[...] = (y_n + bonus) * g

    if carry_mode:
        @pl.when(t_idx == pl.num_programs(2) - 1)
        def _():
            s_out_ref[...] = s_scr[...]


def _rwkv_mixer(rkv, lora, mu_rkv, mu_lora, params, w2, a2, g2, halo_rkv, halo_lora, s_in, *,
                rows, chunk, seq_len, carry_mode):
    bsz, t_len, _ = rkv.shape
    npair = RWKV_PAIRS
    grid = (bsz, npair, t_len // rows)
    tok = lambda off: pl.BlockSpec((None, rows, LANES), lambda b, h, t, off=off: (b, t, off + h))
    par = lambda nrow, off: pl.BlockSpec((nrow, LANES), lambda b, h, t, off=off: (0, off + h))
    const = lambda shape: pl.BlockSpec(shape, lambda b, h, t: (0,) * len(shape))
    in_specs = [tok(0), tok(npair), tok(2 * npair),
                pl.BlockSpec((None, rows, LORA_PAD), lambda b, h, t: (b, t, 0)),
                par(1, 0), par(1, npair), par(1, 2 * npair), const((1, LORA_PAD)),
                par(SUBLANES, 0), par(DECAY_LORA, 0), par(AAA_LORA, 0), par(LORA_PAD - DECAY_LORA - AAA_LORA, 0)]
    if carry_mode:
        hrow = SUBLANES
        htok = lambda off: pl.BlockSpec((None, hrow, LANES), lambda b, h, t, off=off: (b, 0, off + h))
        in_specs += [htok(0), htok(npair), htok(2 * npair),
                     pl.BlockSpec((None, hrow, LORA_PAD), lambda b, h, t: (b, 0, 0))]
        state_spec = pl.BlockSpec((None, None, LANES, LANES), lambda b, h, t: (b, h, 0, 0))
        scratch = [pltpu.VMEM((LANES, LANES), F32)] + [pltpu.VMEM((SUBLANES, LANES), F32)] * 3 + [
            pltpu.VMEM((SUBLANES, LORA_PAD), F32)]
    else:
        in_specs += [tok(0), tok(npair), tok(2 * npair),
                     pl.BlockSpec((None, rows, LORA_PAD), lambda b, h, t: (b, t, 0))]
        state_spec = pl.BlockSpec((rows // seq_len, None, LANES, LANES), lambda b, h, t: (t, h, 0, 0))
        scratch = []
    in_specs.append(state_spec)
    args = [rkv, rkv, rkv, lora, mu_rkv, mu_rkv, mu_rkv, mu_lora, params, w2, a2, g2,
            halo_rkv, halo_rkv, halo_rkv, halo_lora, s_in]
    return pl.pallas_call(
        functools.partial(_rwkv_kernel, rows=rows, chunk=chunk, seq_len=seq_len, carry_mode=carry_mode),
        grid=grid,
        in_specs=in_specs,
        out_specs=[tok(0), state_spec],
        out_shape=[jax.ShapeDtypeStruct((bsz, t_len, RWKV_WIDTH), F32),
                   jax.ShapeDtypeStruct(s_in.shape, F32)],
        scratch_shapes=scratch,
        compiler_params=pltpu.CompilerParams(dimension_semantics=("arbitrary",) * 3,
                                             vmem_limit_bytes=VMEM_LIMIT_BYTES),
        name="rwkv_mixer",
    )(*args)


def _ffn_kernel(*refs, rows, seq_len, blocks_per_seq, carry_mode):
    x_ref, og_ref, orw_ref, wo_ref, nf_ref, wu_ref, cw_ref, wd_ref, nfin_ref = refs[:9]
    if carry_mode:
        h_ref, y_ref, tail_ref, carry_scr = refs[9:]
    else:
        f1_ref, f2_ref, y_ref, tail_ref = refs[9:]
    blk = pl.program_id(0)

    if carry_mode:
        @pl.when(blk % blocks_per_seq == 0)
        def _():
            carry_scr[...] = h_ref[...]

    o = (jnp.dot(og_ref[...].astype(BF16), wo_ref[:GDN_WIDTH], preferred_element_type=F32)
         + jnp.dot(orw_ref[...].astype(BF16), wo_ref[GDN_WIDTH:], preferred_element_type=F32))
    x1 = x_ref[...] + o
    h2 = _rms_norm(x1, nf_ref[...]).astype(BF16)
    gate = jnp.dot(h2, wu_ref[:, :FFN_DIM], preferred_element_type=F32)
    val = jnp.dot(h2, wu_ref[:, FFN_DIM:], preferred_element_type=F32)
    cw = cw_ref[...]
    conv = gate * cw[FFN_CONV - 1:FFN_CONV]
    for s in range(1, FFN_CONV):
        if carry_mode:
            gs = _shift_rows_carry(gate, s, carry_scr[...])
        else:
            gs = _shift_rows_fix(gate, s, (f1_ref, f2_ref)[s - 1][...], seq_len)
        conv = conv + gs * cw[FFN_CONV - 1 - s:FFN_CONV - s]
    if carry_mode:
        carry_scr[...] = gate[rows - SUBLANES:]
        tail_ref[...] = gate[rows - SUBLANES:]
    else:
        tail_ref[...] = gate
    act = (_silu(conv) * val).astype(BF16)
    x2 = x1 + jnp.dot(act, wd_ref[...], preferred_element_type=F32)
    y_ref[...] = _rms_norm(x2, nfin_ref[...])


def _ffn(x2d, o_gdn, o_rwkv, w_out, norm_ffn, w_up, conv_w, w_down, norm_final, halo, *,
         rows, seq_len, carry_mode):
    n_rows = x2d.shape[0]
    n_blocks = n_rows // rows
    row_spec = lambda w: pl.BlockSpec((rows, w), lambda i: (i, 0))
    whole = lambda shape: pl.BlockSpec(shape, lambda i: (0,) * len(shape), pipeline_mode=pl.Buffered(1))
    in_specs = [row_spec(D_MODEL), row_spec(GDN_WIDTH), row_spec(RWKV_WIDTH),
                whole((D_MODEL, D_MODEL)), whole((1, D_MODEL)), whole((D_MODEL, 2 * FFN_DIM)),
                whole((FFN_CONV, FFN_DIM)), whole((FFN_DIM, D_MODEL)), whole((1, D_MODEL))]
    args = [x2d, o_gdn, o_rwkv, w_out, norm_ffn, w_up, conv_w, w_down, norm_final]
    if carry_mode:
        blocks_per_seq = seq_len // rows
        in_specs.append(pl.BlockSpec((None, SUBLANES, FFN_DIM), lambda i: (i // blocks_per_seq, 0, 0)))
        args.append(halo)
        tail_spec = pl.BlockSpec((None, SUBLANES, FFN_DIM), lambda i: (i, 0, 0))
        tail_shape = jax.ShapeDtypeStruct((n_blocks, SUBLANES, FFN_DIM), F32)
        scratch = [pltpu.VMEM((SUBLANES, FFN_DIM), F32)]
    else:
        blocks_per_seq = 1
        in_specs += [row_spec(FFN_DIM), row_spec(FFN_DIM)]
        args += [halo[0], halo[1]]
        tail_spec = row_spec(FFN_DIM)
        tail_shape = jax.ShapeDtypeStruct((n_rows, FFN_DIM), F32)
        scratch = []
    return pl.pallas_call(
        functools.partial(_ffn_kernel, rows=rows, seq_len=seq_len, blocks_per_seq=blocks_per_seq,
                          carry_mode=carry_mode),
        grid=(n_blocks,),
        in_specs=in_specs,
        out_specs=[row_spec(D_MODEL), tail_spec],
        out_shape=[jax.ShapeDtypeStruct((n_rows, D_MODEL), F32), tail_shape],
        scratch_shapes=scratch,
        compiler_params=pltpu.CompilerParams(dimension_semantics=("arbitrary",),
                                             vmem_limit_bytes=VMEM_LIMIT_BYTES),
        name="ffn",
    )(*args)


def _pad_cols(a, width):
    return jnp.pad(a, ((0, 0), (0, width - a.shape[1])))


def _regroup_in_weights(w_in):
    g_end = GDN_QKV + GDN_WIDTH
    ba = w_in[:, g_end:g_end + 2 * GDN_HEADS]
    rw = w_in[:, g_end + 2 * GDN_HEADS:]
    return jnp.concatenate([w_in[:, :g_end], rw[:, :3 * RWKV_WIDTH],
                            _pad_cols(rw[:, 3 * RWKV_WIDTH:], LORA_PAD), _pad_cols(ba, BA_PAD)],
                           axis=1).astype(BF16)


def _tail_tile(state_rows):
    return jnp.pad(state_rows, ((0, 0), (SUBLANES - state_rows.shape[1], 0), (0, 0)))


def _fix_rows(state_rows, s, seq_len):
    n = state_rows.shape[1]
    first = state_rows[:, n - s:, :]
    return jnp.pad(first, ((0, 0), (0, seq_len - s), (0, 0))).reshape(-1, state_rows.shape[2])


def _pair_states(s):
    bsz = s.shape[0]
    sp = s.reshape(bsz, RWKV_PAIRS, 2, RWKV_HEAD_DIM, RWKV_HEAD_DIM)
    zero = jnp.zeros_like(sp[:, :, 0])
    top = jnp.concatenate([sp[:, :, 0], zero], axis=-1)
    bot = jnp.concatenate([zero, sp[:, :, 1]], axis=-1)
    return jnp.concatenate([top, bot], axis=-2)


def _unpair_states(sp):
    hd = RWKV_HEAD_DIM
    bsz = sp.shape[0]
    return jnp.stack([sp[:, :, :hd, :hd], sp[:, :, hd:, hd:]], axis=2).reshape(bsz, RWKV_HEADS, hd, hd)


def _layer(x, st_gdn_conv, st_gdn, st_shift, st_rwkv, st_ffn, wts, *, carry_mode):
    bsz, t_len, _ = x.shape
    n_rows = bsz * t_len
    x2d = x.reshape(n_rows, D_MODEL)
    qkv, z, rkv, lora, ba = _inproj(x2d, wts["norm_mix"], wts["w_in"])

    shift_rkv = st_shift[:, :, :3 * RWKV_WIDTH]
    shift_lora = _pad_cols(st_shift[:, 0, 3 * RWKV_WIDTH:], LORA_PAD)[:, None, :]
    if carry_mode:
        shape3 = lambda a: a.reshape(bsz, t_len, a.shape[-1])
        o_gdn, s_gdn = _gdn_mixer(shape3(qkv), shape3(z), shape3(ba), wts["gdn_conv_w"], wts["gparams"],
                                  wts["gdn_norm"], _tail_tile(st_gdn_conv), st_gdn,
                                  rows=ROW_BLOCK, chunk=GDN_CHUNK, seq_len=GDN_CHUNK, carry_mode=True)
        o_rwkv, s_rwkv = _rwkv_mixer(shape3(rkv), shape3(lora), wts["mu_rkv"], wts["mu_lora"], wts["rparams"],
                                     wts["w2"], wts["a2"], wts["g2"], _tail_tile(shift_rkv),
                                     _tail_tile(shift_lora), _pair_states(st_rwkv),
                                     rows=ROW_BLOCK, chunk=RWKV_CHUNK, seq_len=RWKV_CHUNK, carry_mode=True)
        ffn_halo = _tail_tile(st_ffn)
        ffn_rows, ffn_seq = ROW_BLOCK, t_len
    else:
        shape3 = lambda a: a.reshape(1, n_rows, a.shape[-1])
        fix = lambda st, s: _fix_rows(st, s, t_len)[None]
        o_gdn, s_gdn = _gdn_mixer(shape3(qkv), shape3(z), shape3(ba), wts["gdn_conv_w"], wts["gparams"],
                                  wts["gdn_norm"], [fix(st_gdn_conv, s) for s in range(1, GDN_CONV)], st_gdn,
                                  rows=GDN_CHUNK, chunk=GDN_CHUNK, seq_len=t_len, carry_mode=False)
        o_rwkv, s_rwkv = _rwkv_mixer(shape3(rkv), shape3(lora), wts["mu_rkv"], wts["mu_lora"], wts["rparams"],
                                     wts["w2"], wts["a2"], wts["g2"], fix(shift_rkv, 1), fix(shift_lora, 1),
                                     _pair_states(st_rwkv),
                                     rows=RWKV_CHUNK, chunk=RWKV_CHUNK, seq_len=t_len, carry_mode=False)
        ffn_halo = [_fix_rows(st_ffn, s, t_len) for s in range(1, FFN_CONV)]
        ffn_rows, ffn_seq = n_rows, t_len
    y2d, gate_tail = _ffn(x2d, o_gdn.reshape(n_rows, GDN_WIDTH), o_rwkv.reshape(n_rows, RWKV_WIDTH),
                          wts["w_out"], wts["norm_ffn"], wts["w_up"], wts["ffn_conv_w"], wts["w_down"],
                          wts["norm_final"], ffn_halo, rows=ffn_rows, seq_len=ffn_seq, carry_mode=carry_mode)

    qkv3 = qkv.reshape(bsz, t_len, GDN_QKV)
    gdn_conv_new = qkv3[:, t_len - (GDN_CONV - 1):, :]
    shift_new = jnp.concatenate([rkv.reshape(bsz, t_len, -1)[:, t_len - 1:, :],
                                 lora.reshape(bsz, t_len, -1)[:, t_len - 1:, :RWKV_PROJ - 3 * RWKV_WIDTH]], axis=-1)
    if carry_mode:
        blocks_per_seq = t_len // ROW_BLOCK
        tails = gate_tail.reshape(bsz, blocks_per_seq, SUBLANES, FFN_DIM)[:, -1]
        ffn_new = tails[:, SUBLANES - (FFN_CONV - 1):, :]
    else:
        ffn_new = gate_tail.reshape(bsz, t_len, FFN_DIM)[:, t_len - (FFN_CONV - 1):, :]
    return (y2d.reshape(bsz, t_len, D_MODEL), gdn_conv_new, s_gdn, shift_new, _unpair_states(s_rwkv), ffn_new)


def kernel(x_prompt, x_sample, state_gdn_conv, state_gdn, state_rwkv_shift, state_rwkv, state_ffn_conv,
           norm_mix, w_in, gdn_conv_w, gdn_a_log, gdn_dt_bias, gdn_norm,
           rwkv_mu, rwkv_w0, rwkv_w2, rwkv_a0, rwkv_a2, rwkv_g2, rwkv_k_k, rwkv_k_a, rwkv_r_k,
           rwkv_ln_w, rwkv_ln_b, w_out, norm_ffn, w_up, ffn_conv_w, w_down, norm_final):
    layer = 0
    gparams = jnp.zeros((2, BA_PAD), F32)
    gparams = gparams.at[0, GDN_HEADS:2 * GDN_HEADS].set(gdn_a_log[layer])
    gparams = gparams.at[1, GDN_HEADS:2 * GDN_HEADS].set(gdn_dt_bias[layer])
    mu = rwkv_mu[layer][None, :]
    rparams = jnp.stack([rwkv_w0[layer], rwkv_a0[layer], rwkv_k_k[layer], rwkv_k_a[layer],
                         rwkv_r_k[layer].reshape(-1), rwkv_ln_w[layer], rwkv_ln_b[layer],
                         jnp.zeros((RWKV_WIDTH,), F32)])
    wts = {
        "norm_mix": norm_mix[layer][None, :],
        "w_in": _regroup_in_weights(w_in[layer]),
        "gdn_conv_w": gdn_conv_w[layer],
        "gparams": gparams,
        "gdn_norm": gdn_norm[layer][None, :],
        "mu_rkv": mu[:, :3 * RWKV_WIDTH],
        "mu_lora": _pad_cols(mu[:, 3 * RWKV_WIDTH:], LORA_PAD),
        "rparams": rparams,
        "w2": rwkv_w2[layer].astype(BF16),
        "a2": rwkv_a2[layer].astype(BF16),
        "g2": jnp.pad(rwkv_g2[layer], ((0, LORA_PAD - DECAY_LORA - AAA_LORA - GATE_LORA), (0, 0))).astype(BF16),
        "w_out": w_out[layer].astype(BF16),
        "norm_ffn": norm_ffn[layer][None, :],
        "w_up": w_up[layer].astype(BF16),
        "ffn_conv_w": ffn_conv_w[layer],
        "w_down": w_down[layer].astype(BF16),
        "norm_final": norm_final[None, :],
    }
    n_prompt = x_prompt.shape[0]
    zeros = lambda c: jnp.zeros((n_prompt,) + c.shape[2:], c.dtype)
    outs_p = _layer(x_prompt, zeros(state_gdn_conv), zeros(state_gdn), zeros(state_rwkv_shift),
                    zeros(state_rwkv), zeros(state_ffn_conv), wts, carry_mode=True)
    outs_s = _layer(x_sample, state_gdn_conv[layer], state_gdn[layer], state_rwkv_shift[layer],
                    state_rwkv[layer], state_ffn_conv[layer], wts, carry_mode=False)
    y_p, y_s = outs_p[0], outs_s[0]
    states = []
    for n in range(1, 6):
        states += [outs_p[n][None], outs_s[n][None]]
    return (y_p, y_s, *states)
```

```python
import functools

import jax
import jax.numpy as jnp
from jax import lax
from jax.experimental import pallas as pl
from jax.experimental.pallas import tpu as pltpu

F32 = jnp.float32
BF16 = jnp.bfloat16

D_MODEL = 1024
GDN_HEAD_DIM = 128
GDN_HEADS = 4
GDN_WIDTH = GDN_HEADS * GDN_HEAD_DIM
GDN_QKV = 3 * GDN_WIDTH
GDN_CONV = 4
RWKV_HEAD_DIM = 64
RWKV_HEADS = 8
RWKV_WIDTH = RWKV_HEADS * RWKV_HEAD_DIM
RWKV_PAIRS = RWKV_HEADS // 2
DECAY_LORA = 64
AAA_LORA = 64
GATE_LORA = 160
RWKV_PROJ = 3 * RWKV_WIDTH + DECAY_LORA + AAA_LORA + GATE_LORA
FFN_DIM = 2816
FFN_CONV = 3
NORM_EPS = 1e-6
L2_EPS = 1e-6
RWKV_LN_EPS = 64e-5

LANES = 128
SUBLANES = 8
VMEM_LIMIT_BYTES = 56 * 1024 * 1024

LORA_PAD = 384
BA_PAD = LANES
P_PAD = GDN_QKV + GDN_WIDTH + 3 * RWKV_WIDTH + LORA_PAD + BA_PAD
OFF_QKV = 0
OFF_Z = OFF_QKV + GDN_QKV
OFF_RKV = OFF_Z + GDN_WIDTH
OFF_LORA = OFF_RKV + 3 * RWKV_WIDTH
OFF_BA = OFF_LORA + LORA_PAD

ROW_BLOCK = 256
GDN_CHUNK = 128
RWKV_CHUNK = 64

NN = (((1,), (0,)), ((), ()))
NT = (((1,), (1,)), ((), ()))
TN = (((0,), (0,)), ((), ()))


def _dot(a, b, dims):
    return lax.dot_general(a, b, dims, preferred_element_type=F32)


def _mm(a, b, dims=NN):
    return _dot(a.astype(BF16), b.astype(BF16), dims)


def _split2(a):
    hi = a.astype(BF16)
    return hi, (a - hi.astype(F32)).astype(BF16)


def _mm3(a, b, dims=NN):
    a_hi, a_lo = _split2(a)
    b_hi, b_lo = _split2(b)
    return _dot(a_hi, b_hi, dims) + (_dot(a_hi, b_lo, dims) + _dot(a_lo, b_hi, dims))


def _mm_mask_lhs(mask_bf16, x, dims=NN):
    hi, lo = _split2(x)
    return _dot(mask_bf16, hi, dims) + _dot(mask_bf16, lo, dims)


def _mm_mask_rhs(x, mask_bf16, dims=NN):
    hi, lo = _split2(x)
    return _dot(hi, mask_bf16, dims) + _dot(lo, mask_bf16, dims)


_mm_acc = _mm3


def _iota(shape, dim):
    return lax.broadcasted_iota(jnp.int32, shape, dim)


def _sigmoid(x):
    return 1.0 / (1.0 + jnp.exp(-x))


def _silu(x):
    return x * _sigmoid(x)


def _softplus(x):
    return jnp.maximum(x, 0.0) + jnp.log1p(jnp.exp(-jnp.abs(x)))


def _rms_norm(x, w):
    return x * lax.rsqrt(jnp.mean(x * x, axis=-1, keepdims=True) + NORM_EPS) * w


def _cat(parts, axis):
    return parts[0] if len(parts) == 1 else jnp.concatenate(parts, axis=axis)


def _shift_rows_carry(x, s, carry):
    xr = pltpu.roll(x, s, 0)
    cr = pltpu.roll(carry, s, 0)
    row = _iota((SUBLANES, x.shape[1]), 0)
    head = jnp.where(row < s, cr, xr[:SUBLANES])
    return jnp.concatenate([head, xr[SUBLANES:]], axis=0)


def _shift_rows_fix(x, s, fix, seq_len):
    xr = pltpu.roll(x, s, 0)
    row = _iota(x.shape, 0)
    return jnp.where((row & (seq_len - 1)) < s, fix, xr)


def _unit_lower_inverses(ps, ti, tj, seq_len, mul):
    def joins(s):
        return ((ti ^ tj) < 2 * s) & ((ti & s) != 0) & ((tj & s) == 0)

    eye = jnp.where(ti == tj, 1.0, 0.0)
    ts = [eye + jnp.where(joins(1), p, 0.0) for p in ps]
    s = 2
    while s < seq_len:
        mask = joins(s)
        tbs = [mul(t, jnp.where(mask, p, 0.0)) for t, p in zip(ts, ps)]
        ts = [t + mul(tb, t) for tb, t in zip(tbs, ts)]
        s *= 2
    return ts


def _inproj_kernel(x_ref, nw_ref, w_ref, qkv_ref, z_ref, rkv_ref, lora_ref, ba_ref):
    h = _rms_norm(x_ref[...], nw_ref[...]).astype(BF16)
    for out_ref, off in ((qkv_ref, OFF_QKV), (z_ref, OFF_Z), (rkv_ref, OFF_RKV), (lora_ref, OFF_LORA),
                         (ba_ref, OFF_BA)):
        width = out_ref.shape[1]
        out_ref[...] = jnp.dot(h, w_ref[:, off:off + width], preferred_element_type=F32)


def _inproj(x2d, norm_w, w_perm):
    n_rows = x2d.shape[0]
    widths = (GDN_QKV, GDN_WIDTH, 3 * RWKV_WIDTH, LORA_PAD, BA_PAD)
    return pl.pallas_call(
        _inproj_kernel,
        grid=(n_rows // ROW_BLOCK,),
        in_specs=[
            pl.BlockSpec((ROW_BLOCK, D_MODEL), lambda i: (i, 0)),
            pl.BlockSpec((1, D_MODEL), lambda i: (0, 0)),
            pl.BlockSpec((D_MODEL, P_PAD), lambda i: (0, 0), pipeline_mode=pl.Buffered(1)),
        ],
        out_specs=[pl.BlockSpec((ROW_BLOCK, w), lambda i: (i, 0)) for w in widths],
        out_shape=[jax.ShapeDtypeStruct((n_rows, w), F32) for w in widths],
        compiler_params=pltpu.CompilerParams(dimension_semantics=("arbitrary",),
                                             vmem_limit_bytes=VMEM_LIMIT_BYTES),
        name="inproj",
    )(x2d, norm_w, w_perm)


def _gdn_kernel(*refs, rows, chunk, seq_len, carry_mode):
    qkv_ref, z_ref, ba_ref, cw_ref, gp_ref, nw_ref = refs[:6]
    n_halo = 1 if carry_mode else GDN_CONV - 1
    halo_refs = refs[6:6 + n_halo]
    s_in_ref, o_ref, s_out_ref = refs[6 + n_halo:9 + n_halo]
    scratch = refs[9 + n_halo:]
    nh, hd, c = GDN_HEADS, GDN_HEAD_DIM, chunk
    n_chunks = rows // chunk
    n_seg = chunk // seq_len
    t_idx = pl.program_id(1)

    if carry_mode:
        s_scr, c_scr = scratch

        @pl.when(t_idx == 0)
        def _():
            s_scr[...] = s_in_ref[...]
            c_scr[...] = halo_refs[0][...]

    x = qkv_ref[...]
    w = cw_ref[...]
    acc = x * w[GDN_CONV - 1:GDN_CONV]
    for s in range(1, GDN_CONV):
        if carry_mode:
            xs = _shift_rows_carry(x, s, c_scr[...])
        else:
            xs = _shift_rows_fix(x, s, halo_refs[s - 1][...], seq_len)
        acc = acc + xs * w[GDN_CONV - 1 - s:GDN_CONV - s]
    if carry_mode:
        c_scr[...] = x[rows - SUBLANES:]
    act = _silu(acc)

    ones = jnp.ones((hd, hd), BF16)
    lane_sum = lambda a: _mm_mask_rhs(a, ones)
    l2n = lambda a: a * lax.rsqrt(lane_sum(a * a) + L2_EPS)
    q_h = [l2n(act[:, h * hd:(h + 1) * hd]) * (hd ** -0.5) for h in range(nh)]
    k_h = [l2n(act[:, (nh + h) * hd:(nh + h + 1) * hd]) for h in range(nh)]
    v_h = [act[:, (2 * nh + h) * hd:(2 * nh + h + 1) * hd] for h in range(nh)]

    ba = ba_ref[...]
    gp = gp_ref[...]
    beta_all = _sigmoid(ba)
    lg_all = -jnp.exp(gp[0:1]) * _softplus(ba + gp[1:2])
    lane = _iota((c, LANES), 1)
    pick = lambda arr, col: jnp.sum(jnp.where(lane == col, arr, 0.0), axis=-1, keepdims=True)

    ti = _iota((c, c), 0)
    tj = _iota((c, c), 1)
    same = (ti ^ tj) < seq_len
    incl = same & (tj <= ti)
    strict = same & (tj < ti)
    incl_bf = incl.astype(BF16)
    same_bf = same.astype(BF16)
    incl_t_bf = (same & (ti <= tj)).astype(BF16)

    probs = []
    for ci in range(n_chunks):
        r = slice(ci * c, (ci + 1) * c)
        lgc = lg_all[r]
        gb_all = _mm_mask_lhs(incl_bf, lgc)
        gtot_all = _mm_mask_lhs(same_bf, lgc)
        gb_rows = _mm_mask_rhs(lgc, incl_t_bf, TN)
        for h in range(nh):
            probs.append(dict(ci=ci, h=h, q=q_h[h][r], k=k_h[h][r], v=v_h[h][r], beta=pick(beta_all[r], h),
                              gb=pick(gb_all, nh + h), gtot=pick(gtot_all, nh + h),
                              gb_row=gb_rows[nh + h:nh + h + 1, :]))

    decays = [jnp.exp(jnp.where(incl, p["gb"] - p["gb_row"], -jnp.inf)) for p in probs]
    kbetas = [p["k"] * p["beta"] for p in probs]
    kqs = [_mm(jnp.concatenate([kb, p["q"]], axis=0), p["k"], NT) for kb, p in zip(kbetas, probs)]
    neg_ls = [jnp.where(strict, -(kq[:c] * d), 0.0) for kq, d in zip(kqs, decays)]
    attns = [kq[c:] * d for kq, d in zip(kqs, decays)]
    e_gs = [jnp.exp(p["gb"]) for p in probs]
    t_invs = _unit_lower_inverses(neg_ls, ti, tj, seq_len, _mm_acc)
    sols = [_mm_acc(t, jnp.concatenate([p["v"] * p["beta"], kb * eg], axis=1))
            for t, p, kb, eg in zip(t_invs, probs, kbetas, e_gs)]
    atts = [_mm(a, sol) for a, sol in zip(attns, sols)]
    q_effs = [p["q"] * eg - att[:, hd:] for p, eg, att in zip(probs, e_gs, atts)]
    k_decs = [p["k"] * jnp.exp(p["gtot"] - p["gb"]) for p in probs]
    g_lasts = [jnp.exp(p["gtot"]) for p in probs]
    nms = [[_mm(kd[n * seq_len:(n + 1) * seq_len], sol[n * seq_len:(n + 1) * seq_len], TN) for n in range(n_seg)]
           for kd, sol in zip(k_decs, sols)]

    states = [s_scr[h] for h in range(nh)] if carry_mode else None
    o_rows = [[] for _ in range(nh)]
    for idx, p in enumerate(probs):
        h, ci = p["h"], p["ci"]
        for n in range(n_seg):
            rs = slice(n * seq_len, (n + 1) * seq_len)
            seq = ci * n_seg + n
            s = states[h] if carry_mode else s_in_ref[seq, h]
            o_rows[h].append(_mm(q_effs[idx][rs], s) + atts[idx][rs, :hd])
            nm = nms[idx][n]
            s_new = s * g_lasts[idx][n * seq_len:n * seq_len + 1] - _mm(nm[:, hd:], s) + nm[:, :hd]
            if carry_mode:
                states[h] = s_new
            else:
                s_out_ref[seq, h] = s_new

    nw = nw_ref[...]
    for h in range(nh):
        o = _cat(o_rows[h], 0)
        o = o * lax.rsqrt(lane_sum(o * o) * (1.0 / hd) + NORM_EPS) * nw
        o_ref[:, h * hd:(h + 1) * hd] = o * _silu(z_ref[:, h * hd:(h + 1) * hd])

    if carry_mode:
        for h in range(nh):
            s_scr[h] = states[h]

        @pl.when(t_idx == pl.num_programs(1) - 1)
        def _():
            for h in range(nh):
                s_out_ref[h] = states[h]


def _gdn_mixer(qkv, z, ba, conv_w, gparams, norm_w, halo, s_in, *, rows, chunk, seq_len, carry_mode):
    bsz, t_len, _ = qkv.shape
    grid = (bsz, t_len // rows)
    tok = lambda width: pl.BlockSpec((None, rows, width), lambda b, t: (b, t, 0))
    const = lambda shape: pl.BlockSpec(shape, lambda b, t: (0,) * len(shape))
    in_specs = [tok(GDN_QKV), tok(GDN_WIDTH), tok(BA_PAD), const((GDN_CONV, GDN_QKV)), const((2, BA_PAD)),
                const((1, GDN_HEAD_DIM))]
    args = [qkv, z, ba, conv_w, gparams, norm_w]
    state_tail = (GDN_HEADS, GDN_HEAD_DIM, GDN_HEAD_DIM)
    if carry_mode:
        in_specs.append(pl.BlockSpec((None, SUBLANES, GDN_QKV), lambda b, t: (b, 0, 0)))
        args.append(halo)
        state_spec = pl.BlockSpec((None,) + state_tail, lambda b, t: (b, 0, 0, 0))
        scratch = [pltpu.VMEM(state_tail, F32), pltpu.VMEM((SUBLANES, GDN_QKV), F32)]
    else:
        in_specs += [tok(GDN_QKV)] * (GDN_CONV - 1)
        args += list(halo)
        state_spec = pl.BlockSpec((rows // seq_len,) + state_tail, lambda b, t: (t, 0, 0, 0))
        scratch = []
    in_specs.append(state_spec)
    args.append(s_in)
    return pl.pallas_call(
        functools.partial(_gdn_kernel, rows=rows, chunk=chunk, seq_len=seq_len, carry_mode=carry_mode),
        grid=grid,
        in_specs=in_specs,
        out_specs=[tok(GDN_WIDTH), state_spec],
        out_shape=[jax.ShapeDtypeStruct((bsz, t_len, GDN_WIDTH), F32),
                   jax.ShapeDtypeStruct(s_in.shape, F32)],
        scratch_shapes=scratch,
        compiler_params=pltpu.CompilerParams(dimension_semantics=("arbitrary",) * 2,
                                             vmem_limit_bytes=VMEM_LIMIT_BYTES),
        name="gdn_mixer",
    )(*args)


def _rwkv_kernel(*refs, rows, chunk, seq_len, carry_mode):
    (rkv_ref, lora_ref, mur_ref, mul_ref, par_ref, w2_ref, a2_ref, g2_ref, hr_ref, hl_ref, s_in_ref,
     o_ref, s_out_ref) = refs[:13]
    scratch = refs[13:]
    hd, npair, c, wd_all = RWKV_HEAD_DIM, RWKV_PAIRS, chunk, RWKV_WIDTH
    pw = 2 * hd
    n_chunks = rows // chunk
    n_seg = chunk // seq_len
    t_idx = pl.program_id(1)

    if carry_mode:
        s_scr, cr_scr, cl_scr = scratch

        @pl.when(t_idx == 0)
        def _():
            s_scr[...] = s_in_ref[...]
            cr_scr[...] = hr_ref[...]
            cl_scr[...] = hl_ref[...]

    def token_shift(x_ref, mu_ref, h_ref, c_scr):
        x = x_ref[...]
        mu = mu_ref[...]
        if carry_mode:
            prev = _shift_rows_carry(x, 1, c_scr[...])
            c_scr[...] = x[rows - SUBLANES:]
        else:
            prev = _shift_rows_fix(x, 1, h_ref[...], seq_len)
        return mu * prev + (1.0 - mu) * x

    xs = token_shift(rkv_ref, mur_ref, hr_ref, cr_scr if carry_mode else None)
    lora = token_shift(lora_ref, mul_ref, hl_ref, cl_scr if carry_mode else None)
    r, kr, vr = xs[:, :wd_all], xs[:, wd_all:2 * wd_all], xs[:, 2 * wd_all:]
    par = par_ref[...]
    w0, a0, k_k, k_a, r_k, ln_w, ln_b = (par[n:n + 1] for n in range(7))
    wd = lora[:, :DECAY_LORA]
    ad = lora[:, DECAY_LORA:DECAY_LORA + AAA_LORA]
    gd = lora[:, DECAY_LORA + AAA_LORA:]
    w = -_softplus(-(w0 + _mm(jnp.tanh(wd), w2_ref[...]))) - 0.5
    lw = -jnp.exp(w)
    asig = _sigmoid(a0 + _mm(ad, a2_ref[...]))
    g = _mm(_sigmoid(gd), g2_ref[...])

    li = _iota((pw, pw), 0)
    lj = _iota((pw, pw), 1)
    pair_bd = (li >= hd) == (lj >= hd)
    head_ones = pair_bd.astype(BF16)

    def head_sum(a):
        return _cat([_mm_mask_rhs(a[:, p * pw:(p + 1) * pw], head_ones) for p in range(a.shape[1] // pw)], 1)

    kk = kr * k_k
    kk = kk * lax.rsqrt(head_sum(kk * kk) + L2_EPS)
    kr = kr * (1.0 + (asig - 1.0) * k_a)
    a_vec = -kk
    b_vec = kk * asig

    i1 = _iota((c, c), 0)
    j1 = _iota((c, c), 1)
    same1 = (i1 ^ j1) < seq_len
    incl1_bf = (same1 & (j1 <= i1)).astype(BF16)
    same1_bf = same1.astype(BF16)
    ip = _iota((c, 2 * c), 0)
    jp_full = _iota((c, 2 * c), 1)
    jp = jp_full & (c - 1)
    head1_cols = jp_full >= c
    same_p = (ip ^ jp) < seq_len
    incl_p = same_p & (jp <= ip)
    strict_p = same_p & (jp < ip)

    def stack_heads(a):
        head1 = (_iota(a.shape, 1) & hd) != 0
        return jnp.concatenate([jnp.where(head1, 0.0, a), jnp.where(head1, a, 0.0)], axis=0)

    def block_diag(m):
        return jnp.concatenate([jnp.where(head1_cols, 0.0, m), jnp.where(head1_cols, m, 0.0)], axis=0)

    pair_mul = lambda a, b: _mm_acc(a, block_diag(b))

    probs = []
    for ci in range(n_chunks):
        rs = slice(ci * c, (ci + 1) * c)
        lwc = lw[rs]
        g_incl = _mm_mask_lhs(incl1_bf, lwc)
        g_tot = _mm_mask_lhs(same1_bf, lwc)
        g_excl = g_incl - lwc
        g_mid = 0.5 * g_tot
        inv_n = jnp.exp(g_mid - g_incl)
        to_end = jnp.exp(g_tot - g_incl)
        full = dict(r_n=r[rs] * jnp.exp(g_incl - g_mid), a_n=a_vec[rs] * jnp.exp(g_excl - g_mid),
                    b_n=b_vec[rs] * inv_n, k_n=kr[rs] * inv_n, r_t=r[rs] * jnp.exp(g_incl),
                    a_t=a_vec[rs] * jnp.exp(g_excl), b_end=b_vec[rs] * to_end, k_end=kr[rs] * to_end,
                    gam=jnp.exp(g_tot), v=vr[rs])
        for p in range(npair):
            prob = {name: val[:, p * pw:(p + 1) * pw] for name, val in full.items()}
            prob.update(ci=ci, p=p)
            probs.append(prob)

    ars = [jnp.concatenate([p["a_n"], p["r_n"]], axis=0) for p in probs]
    m_bs = [_mm(ar, stack_heads(p["b_n"]), NT) for ar, p in zip(ars, probs)]
    m_ks = [_mm(ar, stack_heads(p["k_n"]), NT) for ar, p in zip(ars, probs)]
    a_abs = [jnp.where(strict_p, m[:c], 0.0) for m in m_bs]
    a_rbs = [jnp.where(incl_p, m[c:], 0.0) for m in m_bs]
    a_aks = [jnp.where(strict_p, m[:c], 0.0) for m in m_ks]
    a_rks = [jnp.where(incl_p, m[c:], 0.0) for m in m_ks]
    t_invs = _unit_lower_inverses(a_abs, ip, jp, seq_len, pair_mul)
    zvs = [_mm(a_ak, stack_heads(p["v"])) for a_ak, p in zip(a_aks, probs)]
    uws = [_mm_acc(t, stack_heads(jnp.concatenate([zv, p["a_t"]], axis=1)))
           for t, zv, p in zip(t_invs, zvs, probs)]
    rbs = [_mm(a_rb, stack_heads(uw)) for a_rb, uw in zip(a_rbs, uws)]
    y_frees = [rb[:, :pw] + _mm(a_rk, stack_heads(p["v"])) for rb, a_rk, p in zip(rbs, a_rks, probs)]
    r_effs = [p["r_t"] + rb[:, pw:] for p, rb in zip(probs, rbs)]
    m_corrs, n_adds = [], []
    for p, uw in zip(probs, uws):
        ms, ns = [], []
        for n in range(n_seg):
            sr = slice(n * seq_len, (n + 1) * seq_len)
            ms.append(jnp.where(pair_bd, _mm_acc(uw[sr, pw:], p["b_end"][sr], TN), 0.0))
            lhs = jnp.concatenate([uw[sr, :pw], p["v"][sr]], axis=0)
            rhs = jnp.concatenate([p["b_end"][sr], p["k_end"][sr]], axis=0)
            ns.append(jnp.where(pair_bd, _mm_acc(lhs, rhs, TN), 0.0))
        m_corrs.append(ms)
        n_adds.append(ns)

    states = [s_scr[p] for p in range(npair)] if carry_mode else None
    y_rows = [[] for _ in range(npair)]
    for idx, prob in enumerate(probs):
        p, ci = prob["p"], prob["ci"]
        for n in range(n_seg):
            sr = slice(n * seq_len, (n + 1) * seq_len)
            seq = ci * n_seg + n
            s = states[p] if carry_mode else s_in_ref[seq, p]
            y_rows[p].append(_mm_acc(r_effs[idx][sr], s, NT) + y_frees[idx][sr])
            gam = prob["gam"][n * seq_len:n * seq_len + 1]
            s_new = s * gam + _mm_acc(s, m_corrs[idx][n]) + n_adds[idx][n]
            if carry_mode:
                states[p] = s_new
            else:
                s_out_ref[seq, p] = s_new

    y = _cat([_cat(y_rows[p], 0) for p in range(npair)], 1)
    y_c = y - head_sum(y) * (1.0 / hd)
    y_n = y_c * lax.rsqrt(head_sum(y_c * y_c) * (1.0 / hd) + RWKV_LN_EPS)
    y_n = y_n * ln_w + ln_b
    bonus = head_sum(r * kr * r_k) * vr
    o_ref[...] = (y_n + bonus) * g

    if carry_mode:
        for p in range(npair):
            s_scr[p] = states[p]

        @pl.when(t_idx == pl.num_programs(1) - 1)
        def _():
            for p in range(npair):
                s_out_ref[p] = states[p]


def _rwkv_mixer(rkv, lora, mu_rkv, mu_lora, params, w2, a2, g2, halo_rkv, halo_lora, s_in, *,
                rows, chunk, seq_len, carry_mode):
    bsz, t_len, _ = rkv.shape
    grid = (bsz, t_len // rows)
    tok = lambda width: pl.BlockSpec((None, rows, width), lambda b, t: (b, t, 0))
    const = lambda shape: pl.BlockSpec(shape, lambda b, t: (0,) * len(shape))
    in_specs = [tok(3 * RWKV_WIDTH), tok(LORA_PAD), const((1, 3 * RWKV_WIDTH)), const((1, LORA_PAD)),
                const((SUBLANES, RWKV_WIDTH)), const((DECAY_LORA, RWKV_WIDTH)), const((AAA_LORA, RWKV_WIDTH)),
                const((LORA_PAD - DECAY_LORA - AAA_LORA, RWKV_WIDTH))]
    state_tail = (RWKV_PAIRS, LANES, LANES)
    if carry_mode:
        in_specs += [pl.BlockSpec((None, SUBLANES, 3 * RWKV_WIDTH), lambda b, t: (b, 0, 0)),
                     pl.BlockSpec((None, SUBLANES, LORA_PAD), lambda b, t: (b, 0, 0))]
        state_spec = pl.BlockSpec((None,) + state_tail, lambda b, t: (b, 0, 0, 0))
        scratch = [pltpu.VMEM(state_tail, F32), pltpu.VMEM((SUBLANES, 3 * RWKV_WIDTH), F32),
                   pltpu.VMEM((SUBLANES, LORA_PAD), F32)]
    else:
        in_specs += [tok(3 * RWKV_WIDTH), tok(LORA_PAD)]
        state_spec = pl.BlockSpec((rows // seq_len,) + state_tail, lambda b, t: (t, 0, 0, 0))
        scratch = []
    in_specs.append(state_spec)
    args = [rkv, lora, mu_rkv, mu_lora, params, w2, a2, g2, halo_rkv, halo_lora, s_in]
    return pl.pallas_call(
        functools.partial(_rwkv_kernel, rows=rows, chunk=chunk, seq_len=seq_len, carry_mode=carry_mode),
        grid=grid,
        in_specs=in_specs,
        out_specs=[tok(RWKV_WIDTH), state_spec],
        out_shape=[jax.ShapeDtypeStruct((bsz, t_len, RWKV_WIDTH), F32),
                   jax.ShapeDtypeStruct(s_in.shape, F32)],
        scratch_shapes=scratch,
        compiler_params=pltpu.CompilerParams(dimension_semantics=("arbitrary",) * 2,
                                             vmem_limit_bytes=VMEM_LIMIT_BYTES),
        name="rwkv_mixer",
    )(*args)


def _ffn_kernel(*refs, rows, seq_len, blocks_per_seq, carry_mode):
    x_ref, og_ref, orw_ref, wo_ref, nf_ref, wu_ref, cw_ref, wd_ref, nfin_ref = refs[:9]
    if carry_mode:
        h_ref, y_ref, tail_ref, carry_scr = refs[9:]
    else:
        f1_ref, f2_ref, y_ref, tail_ref = refs[9:]
    blk = pl.program_id(0)

    if carry_mode:
        @pl.when(blk % blocks_per_seq == 0)
        def _():
            carry_scr[...] = h_ref[...]

    o = (jnp.dot(og_ref[...].astype(BF16), wo_ref[:GDN_WIDTH], preferred_element_type=F32)
         + jnp.dot(orw_ref[...].astype(BF16), wo_ref[GDN_WIDTH:], preferred_element_type=F32))
    x1 = x_ref[...] + o
    h2 = _rms_norm(x1, nf_ref[...]).astype(BF16)
    gate = jnp.dot(h2, wu_ref[:, :FFN_DIM], preferred_element_type=F32)
    val = jnp.dot(h2, wu_ref[:, FFN_DIM:], preferred_element_type=F32)
    cw = cw_ref[...]
    conv = gate * cw[FFN_CONV - 1:FFN_CONV]
    for s in range(1, FFN_CONV):
        if carry_mode:
            gs = _shift_rows_carry(gate, s, carry_scr[...])
        else:
            gs = _shift_rows_fix(gate, s, (f1_ref, f2_ref)[s - 1][...], seq_len)
        conv = conv + gs * cw[FFN_CONV - 1 - s:FFN_CONV - s]
    if carry_mode:
        carry_scr[...] = gate[rows - SUBLANES:]
        tail_ref[...] = gate[rows - SUBLANES:]
    else:
        tail_ref[...] = gate
    act = (_silu(conv) * val).astype(BF16)
    x2 = x1 + jnp.dot(act, wd_ref[...], preferred_element_type=F32)
    y_ref[...] = _rms_norm(x2, nfin_ref[...])


def _ffn(x2d, o_gdn, o_rwkv, w_out, norm_ffn, w_up, conv_w, w_down, norm_final, halo, *,
         rows, seq_len, carry_mode):
    n_rows = x2d.shape[0]
    n_blocks = n_rows // rows
    row_spec = lambda w: pl.BlockSpec((rows, w), lambda i: (i, 0))
    whole = lambda shape: pl.BlockSpec(shape, lambda i: (0,) * len(shape), pipeline_mode=pl.Buffered(1))
    in_specs = [row_spec(D_MODEL), row_spec(GDN_WIDTH), row_spec(RWKV_WIDTH),
                whole((D_MODEL, D_MODEL)), whole((1, D_MODEL)), whole((D_MODEL, 2 * FFN_DIM)),
                whole((FFN_CONV, FFN_DIM)), whole((FFN_DIM, D_MODEL)), whole((1, D_MODEL))]
    args = [x2d, o_gdn, o_rwkv, w_out, norm_ffn, w_up, conv_w, w_down, norm_final]
    if carry_mode:
        blocks_per_seq = seq_len // rows
        in_specs.append(pl.BlockSpec((None, SUBLANES, FFN_DIM), lambda i: (i // blocks_per_seq, 0, 0)))
        args.append(halo)
        tail_spec = pl.BlockSpec((None, SUBLANES, FFN_DIM), lambda i: (i, 0, 0))
        tail_shape = jax.ShapeDtypeStruct((n_blocks, SUBLANES, FFN_DIM), F32)
        scratch = [pltpu.VMEM((SUBLANES, FFN_DIM), F32)]
    else:
        blocks_per_seq = 1
        in_specs += [row_spec(FFN_DIM), row_spec(FFN_DIM)]
        args += [halo[0], halo[1]]
        tail_spec = row_spec(FFN_DIM)
        tail_shape = jax.ShapeDtypeStruct((n_rows, FFN_DIM), F32)
        scratch = []
    return pl.pallas_call(
        functools.partial(_ffn_kernel, rows=rows, seq_len=seq_len, blocks_per_seq=blocks_per_seq,
                          carry_mode=carry_mode),
        grid=(n_blocks,),
        in_specs=in_specs,
        out_specs=[row_spec(D_MODEL), tail_spec],
        out_shape=[jax.ShapeDtypeStruct((n_rows, D_MODEL), F32), tail_shape],
        scratch_shapes=scratch,
        compiler_params=pltpu.CompilerParams(dimension_semantics=("arbitrary",),
                                             vmem_limit_bytes=VMEM_LIMIT_BYTES),
        name="ffn",
    )(*args)


def _pad_cols(a, width):
    return jnp.pad(a, ((0, 0), (0, width - a.shape[1])))


def _regroup_in_weights(w_in):
    g_end = GDN_QKV + GDN_WIDTH
    ba = w_in[:, g_end:g_end + 2 * GDN_HEADS]
    rw = w_in[:, g_end + 2 * GDN_HEADS:]
    return jnp.concatenate([w_in[:, :g_end], rw[:, :3 * RWKV_WIDTH],
                            _pad_cols(rw[:, 3 * RWKV_WIDTH:], LORA_PAD), _pad_cols(ba, BA_PAD)],
                           axis=1).astype(BF16)


def _tail_tile(state_rows):
    return jnp.pad(state_rows, ((0, 0), (SUBLANES - state_rows.shape[1], 0), (0, 0)))


def _fix_rows(state_rows, s, seq_len):
    n = state_rows.shape[1]
    first = state_rows[:, n - s:, :]
    return jnp.pad(first, ((0, 0), (0, seq_len - s), (0, 0))).reshape(-1, state_rows.shape[2])


def _pair_states(s):
    bsz = s.shape[0]
    sp = s.reshape(bsz, RWKV_PAIRS, 2, RWKV_HEAD_DIM, RWKV_HEAD_DIM)
    zero = jnp.zeros_like(sp[:, :, 0])
    top = jnp.concatenate([sp[:, :, 0], zero], axis=-1)
    bot = jnp.concatenate([zero, sp[:, :, 1]], axis=-1)
    return jnp.concatenate([top, bot], axis=-2)


def _unpair_states(sp):
    hd = RWKV_HEAD_DIM
    bsz = sp.shape[0]
    return jnp.stack([sp[:, :, :hd, :hd], sp[:, :, hd:, hd:]], axis=2).reshape(bsz, RWKV_HEADS, hd, hd)


def _layer(x, st_gdn_conv, st_gdn, st_shift, st_rwkv, st_ffn, wts, *, carry_mode):
    bsz, t_len, _ = x.shape
    n_rows = bsz * t_len
    x2d = x.reshape(n_rows, D_MODEL)
    qkv, z, rkv, lora, ba = _inproj(x2d, wts["norm_mix"], wts["w_in"])

    shift_rkv = st_shift[:, :, :3 * RWKV_WIDTH]
    shift_lora = _pad_cols(st_shift[:, 0, 3 * RWKV_WIDTH:], LORA_PAD)[:, None, :]
    if carry_mode:
        shape3 = lambda a: a.reshape(bsz, t_len, a.shape[-1])
        o_gdn, s_gdn = _gdn_mixer(shape3(qkv), shape3(z), shape3(ba), wts["gdn_conv_w"], wts["gparams"],
                                  wts["gdn_norm"], _tail_tile(st_gdn_conv), st_gdn,
                                  rows=ROW_BLOCK, chunk=GDN_CHUNK, seq_len=GDN_CHUNK, carry_mode=True)
        o_rwkv, s_rwkv = _rwkv_mixer(shape3(rkv), shape3(lora), wts["mu_rkv"], wts["mu_lora"], wts["rparams"],
                                     wts["w2"], wts["a2"], wts["g2"], _tail_tile(shift_rkv),
                                     _tail_tile(shift_lora), _pair_states(st_rwkv),
                                     rows=ROW_BLOCK, chunk=RWKV_CHUNK, seq_len=RWKV_CHUNK, carry_mode=True)
        ffn_halo = _tail_tile(st_ffn)
        ffn_rows, ffn_seq = ROW_BLOCK, t_len
    else:
        shape3 = lambda a: a.reshape(1, n_rows, a.shape[-1])
        fix = lambda st, s: _fix_rows(st, s, t_len)[None]
        o_gdn, s_gdn = _gdn_mixer(shape3(qkv), shape3(z), shape3(ba), wts["gdn_conv_w"], wts["gparams"],
                                  wts["gdn_norm"], [fix(st_gdn_conv, s) for s in range(1, GDN_CONV)], st_gdn,
                                  rows=GDN_CHUNK, chunk=GDN_CHUNK, seq_len=t_len, carry_mode=False)
        o_rwkv, s_rwkv = _rwkv_mixer(shape3(rkv), shape3(lora), wts["mu_rkv"], wts["mu_lora"], wts["rparams"],
                                     wts["w2"], wts["a2"], wts["g2"], fix(shift_rkv, 1), fix(shift_lora, 1),
                                     _pair_states(st_rwkv),
                                     rows=RWKV_CHUNK, chunk=RWKV_CHUNK, seq_len=t_len, carry_mode=False)
        ffn_halo = [_fix_rows(st_ffn, s, t_len) for s in range(1, FFN_CONV)]
        ffn_rows, ffn_seq = n_rows, t_len
    y2d, gate_tail = _ffn(x2d, o_gdn.reshape(n_rows, GDN_WIDTH), o_rwkv.reshape(n_rows, RWKV_WIDTH),
                          wts["w_out"], wts["norm_ffn"], wts["w_up"], wts["ffn_conv_w"], wts["w_down"],
                          wts["norm_final"], ffn_halo, rows=ffn_rows, seq_len=ffn_seq, carry_mode=carry_mode)

    qkv3 = qkv.reshape(bsz, t_len, GDN_QKV)
    gdn_conv_new = qkv3[:, t_len - (GDN_CONV - 1):, :]
    shift_new = jnp.concatenate([rkv.reshape(bsz, t_len, -1)[:, t_len - 1:, :],
                                 lora.reshape(bsz, t_len, -1)[:, t_len - 1:, :RWKV_PROJ - 3 * RWKV_WIDTH]], axis=-1)
    if carry_mode:
        blocks_per_seq = t_len // ROW_BLOCK
        tails = gate_tail.reshape(bsz, blocks_per_seq, SUBLANES, FFN_DIM)[:, -1]
        ffn_new = tails[:, SUBLANES - (FFN_CONV - 1):, :]
    else:
        ffn_new = gate_tail.reshape(bsz, t_len, FFN_DIM)[:, t_len - (FFN_CONV - 1):, :]
    return (y2d.reshape(bsz, t_len, D_MODEL), gdn_conv_new, s_gdn, shift_new, _unpair_states(s_rwkv), ffn_new)


def kernel(x_prompt, x_sample, state_gdn_conv, state_gdn, state_rwkv_shift, state_rwkv, state_ffn_conv,
           norm_mix, w_in, gdn_conv_w, gdn_a_log, gdn_dt_bias, gdn_norm,
           rwkv_mu, rwkv_w0, rwkv_w2, rwkv_a0, rwkv_a2, rwkv_g2, rwkv_k_k, rwkv_k_a, rwkv_r_k,
           rwkv_ln_w, rwkv_ln_b, w_out, norm_ffn, w_up, ffn_conv_w, w_down, norm_final):
    layer = 0
    gparams = jnp.zeros((2, BA_PAD), F32)
    gparams = gparams.at[0, GDN_HEADS:2 * GDN_HEADS].set(gdn_a_log[layer])
    gparams = gparams.at[1, GDN_HEADS:2 * GDN_HEADS].set(gdn_dt_bias[layer])
    mu = rwkv_mu[layer][None, :]
    rparams = jnp.stack([rwkv_w0[layer], rwkv_a0[layer], rwkv_k_k[layer], rwkv_k_a[layer],
                         rwkv_r_k[layer].reshape(-1), rwkv_ln_w[layer], rwkv_ln_b[layer],
                         jnp.zeros((RWKV_WIDTH,), F32)])
    gate_rows = LORA_PAD - DECAY_LORA - AAA_LORA
    wts = {
        "norm_mix": norm_mix[layer][None, :],
        "w_in": _regroup_in_weights(w_in[layer]),
        "gdn_conv_w": gdn_conv_w[layer],
        "gparams": gparams,
        "gdn_norm": gdn_norm[layer][None, :],
        "mu_rkv": mu[:, :3 * RWKV_WIDTH],
        "mu_lora": _pad_cols(mu[:, 3 * RWKV_WIDTH:], LORA_PAD),
        "rparams": rparams,
        "w2": rwkv_w2[layer].astype(BF16),
        "a2": rwkv_a2[layer].astype(BF16),
        "g2": jnp.pad(rwkv_g2[layer], ((0, gate_rows - GATE_LORA), (0, 0))).astype(BF16),
        "w_out": w_out[layer].astype(BF16),
        "norm_ffn": norm_ffn[layer][None, :],
        "w_up": w_up[layer].astype(BF16),
        "ffn_conv_w": ffn_conv_w[layer],
        "w_down": w_down[layer].astype(BF16),
        "norm_final": norm_final[None, :],
    }
    n_prompt = x_prompt.shape[0]
    zeros = lambda c: jnp.zeros((n_prompt,) + c.shape[2:], c.dtype)
    outs_p = _layer(x_prompt, zeros(state_gdn_conv), zeros(state_gdn), zeros(state_rwkv_shift),
                    zeros(state_rwkv), zeros(state_ffn_conv), wts, carry_mode=True)
    outs_s = _layer(x_sample, state_gdn_conv[layer], state_gdn[layer], state_rwkv_shift[layer],
                    state_rwkv[layer], state_ffn_conv[layer], wts, carry_mode=False)
    y_p, y_s = outs_p[0], outs_s[0]
    states = []
    for n in range(1, 6):
        states += [outs_p[n][None], outs_s[n][None]]
    return (y_p, y_s, *states)
```

```python
import functools
import math

import jax
import jax.numpy as jnp
from jax import lax
from jax.experimental import pallas as pl
from jax.experimental.pallas import tpu as pltpu

F32 = jnp.float32
BF16 = jnp.bfloat16

D_MODEL = 1024
GDN_HEAD_DIM = 128
GDN_HEADS = 4
GDN_WIDTH = GDN_HEADS * GDN_HEAD_DIM
GDN_QKV = 3 * GDN_WIDTH
GDN_CONV = 4
RWKV_HEAD_DIM = 64
RWKV_HEADS = 8
RWKV_WIDTH = RWKV_HEADS * RWKV_HEAD_DIM
RWKV_PAIRS = RWKV_HEADS // 2
DECAY_LORA = 64
AAA_LORA = 64
GATE_LORA = 160
RWKV_PROJ = 3 * RWKV_WIDTH + DECAY_LORA + AAA_LORA + GATE_LORA
FFN_DIM = 2816
FFN_CONV = 3
NORM_EPS = 1e-6
L2_EPS = 1e-6
RWKV_LN_EPS = 64e-5
DECAY_SCALE = math.exp(-0.5)

LANES = 128
SUBLANES = 8
VMEM_LIMIT_BYTES = 56 * 1024 * 1024

LORA_PAD = 384
BA_PAD = LANES
P_PAD = GDN_QKV + GDN_WIDTH + 3 * RWKV_WIDTH + LORA_PAD + BA_PAD
OFF_QKV = 0
OFF_Z = OFF_QKV + GDN_QKV
OFF_RKV = OFF_Z + GDN_WIDTH
OFF_LORA = OFF_RKV + 3 * RWKV_WIDTH
OFF_BA = OFF_LORA + LORA_PAD

ROW_BLOCK = 256
MIXER_ROWS = 512
GDN_CHUNK = 128
RWKV_CHUNK = 64

NN = (((1,), (0,)), ((), ()))
NT = (((1,), (1,)), ((), ()))
TN = (((0,), (0,)), ((), ()))


def _dot(a, b, dims):
    return lax.dot_general(a, b, dims, preferred_element_type=F32)


def _mm(a, b, dims=NN):
    return _dot(a.astype(BF16), b.astype(BF16), dims)


def _split2(a):
    hi = a.astype(BF16)
    return hi, (a - hi.astype(F32)).astype(BF16)


def _mm_mask_lhs(mask_bf16, x, dims=NN):
    hi, lo = _split2(x)
    return _dot(mask_bf16, hi, dims) + _dot(mask_bf16, lo, dims)


def _mm_mask_rhs(x, mask_bf16, dims=NN):
    hi, lo = _split2(x)
    return _dot(hi, mask_bf16, dims) + _dot(lo, mask_bf16, dims)


_mm_acc = _mm


def _iota(shape, dim):
    return lax.broadcasted_iota(jnp.int32, shape, dim)


def _sigmoid(x):
    return 0.5 + 0.5 * jnp.tanh(0.5 * x)


def _silu(x):
    h = 0.5 * x
    return h + h * jnp.tanh(h)


def _softplus(x):
    return jnp.maximum(x, 0.0) + jnp.log1p(jnp.exp(-jnp.abs(x)))


def _rms_norm(x, w):
    return x * lax.rsqrt(jnp.mean(x * x, axis=-1, keepdims=True) + NORM_EPS) * w


def _cat(parts, axis):
    return parts[0] if len(parts) == 1 else jnp.concatenate(parts, axis=axis)


def _shift_rows_carry(x, s, carry):
    xr = pltpu.roll(x, s, 0)
    cr = pltpu.roll(carry, s, 0)
    row = _iota((SUBLANES, x.shape[1]), 0)
    head = jnp.where(row < s, cr, xr[:SUBLANES])
    return jnp.concatenate([head, xr[SUBLANES:]], axis=0)


def _shift_rows_fix(x, s, fix, seq_len):
    xr = pltpu.roll(x, s, 0)
    row = _iota(x.shape, 0)
    return jnp.where((row & (seq_len - 1)) < s, fix, xr)


def _unit_lower_inverses(ps, ti, tj, seq_len, mul):
    def joins(s):
        return ((ti ^ tj) < 2 * s) & ((ti & s) != 0) & ((tj & s) == 0)

    eye = jnp.where(ti == tj, 1.0, 0.0)
    ts = [eye + jnp.where(joins(1), p, 0.0) for p in ps]
    s = 2
    while s < seq_len:
        mask = joins(s)
        tbs = [mul(t, jnp.where(mask, p, 0.0)) for t, p in zip(ts, ps)]
        ts = [t + mul(tb, t) for tb, t in zip(tbs, ts)]
        s *= 2
    return ts


def _unit_lower_inverses_halved(ps):
    c = ps[0].shape[0]
    half = c // 2
    ti = _iota((half, c), 0)
    lane = _iota((half, c), 1)
    right = lane >= half
    zeros = jnp.zeros((half, c), F32)

    def block_diag(m):
        return jnp.concatenate([jnp.where(right, 0.0, m), jnp.where(right, m, 0.0)], axis=0)

    diag = [jnp.where(right, p[half:], p[:half]) for p in ps]
    t_ab = _unit_lower_inverses(diag, ti, lane & (half - 1), half, lambda a, b: _mm_acc(a, block_diag(b)))
    ys = [_mm_acc(jnp.where(right, 0.0, p[half:]), jnp.concatenate([jnp.where(right, 0.0, t), zeros], axis=0))
          for p, t in zip(ps, t_ab)]
    x21s = [_mm_acc(t, jnp.concatenate([zeros, y], axis=0)) for t, y in zip(t_ab, ys)]
    return [jnp.concatenate([jnp.where(right, 0.0, t), x21 + jnp.where(right, t, 0.0)], axis=0)
            for t, x21 in zip(t_ab, x21s)]


def _inproj_kernel(x_ref, nw_ref, w_ref, qkv_ref, z_ref, rkv_ref, lora_ref, ba_ref):
    h = _rms_norm(x_ref[...], nw_ref[...]).astype(BF16)
    for out_ref, off in ((qkv_ref, OFF_QKV), (z_ref, OFF_Z), (rkv_ref, OFF_RKV), (lora_ref, OFF_LORA),
                         (ba_ref, OFF_BA)):
        width = out_ref.shape[1]
        out_ref[...] = jnp.dot(h, w_ref[:, off:off + width], preferred_element_type=F32)


def _inproj(x2d, norm_w, w_perm):
    n_rows = x2d.shape[0]
    widths = (GDN_QKV, GDN_WIDTH, 3 * RWKV_WIDTH, LORA_PAD, BA_PAD)
    return pl.pallas_call(
        _inproj_kernel,
        grid=(n_rows // ROW_BLOCK,),
        in_specs=[
            pl.BlockSpec((ROW_BLOCK, D_MODEL), lambda i: (i, 0)),
            pl.BlockSpec((1, D_MODEL), lambda i: (0, 0)),
            pl.BlockSpec((D_MODEL, P_PAD), lambda i: (0, 0), pipeline_mode=pl.Buffered(1)),
        ],
        out_specs=[pl.BlockSpec((ROW_BLOCK, w), lambda i: (i, 0)) for w in widths],
        out_shape=[jax.ShapeDtypeStruct((n_rows, w), F32) for w in widths],
        compiler_params=pltpu.CompilerParams(dimension_semantics=("arbitrary",),
                                             vmem_limit_bytes=VMEM_LIMIT_BYTES),
        name="inproj",
    )(x2d, norm_w, w_perm)


def _gdn_kernel(*refs, rows, chunk, seq_len, carry_mode):
    qkv_ref, z_ref, ba_ref, cw_ref, gp_ref, nw_ref = refs[:6]
    n_halo = 1 if carry_mode else GDN_CONV - 1
    halo_refs = refs[6:6 + n_halo]
    s_in_ref, o_ref, s_out_ref = refs[6 + n_halo:9 + n_halo]
    scratch = refs[9 + n_halo:]
    nh, hd, c = GDN_HEADS, GDN_HEAD_DIM, chunk
    n_chunks = rows // chunk
    n_seg = chunk // seq_len
    t_idx = pl.program_id(1)

    if carry_mode:
        s_scr, c_scr = scratch

        @pl.when(t_idx == 0)
        def _():
            s_scr[...] = s_in_ref[...]
            c_scr[...] = halo_refs[0][...]

    x = qkv_ref[...]
    w = cw_ref[...]
    acc = x * w[GDN_CONV - 1:GDN_CONV]
    for s in range(1, GDN_CONV):
        if carry_mode:
            xs = _shift_rows_carry(x, s, c_scr[...])
        else:
            xs = _shift_rows_fix(x, s, halo_refs[s - 1][...], seq_len)
        acc = acc + xs * w[GDN_CONV - 1 - s:GDN_CONV - s]
    if carry_mode:
        c_scr[...] = x[rows - SUBLANES:]
    act = _silu(acc)

    ones = jnp.ones((hd, hd), BF16)
    lane_sum = lambda a: _mm(a, ones)
    l2n = lambda a: a * lax.rsqrt(lane_sum(a * a) + L2_EPS)
    q_h = [l2n(act[:, h * hd:(h + 1) * hd]) * (hd ** -0.5) for h in range(nh)]
    k_h = [l2n(act[:, (nh + h) * hd:(nh + h + 1) * hd]) for h in range(nh)]
    v_h = [act[:, (2 * nh + h) * hd:(2 * nh + h + 1) * hd] for h in range(nh)]

    ba = ba_ref[...]
    gp = gp_ref[...]
    beta_all = _sigmoid(ba)
    lg_all = -jnp.exp(gp[0:1]) * _softplus(ba + gp[1:2])
    lane = _iota((c, LANES), 1)
    pick = lambda arr, col: jnp.sum(jnp.where(lane == col, arr, 0.0), axis=-1, keepdims=True)

    ti = _iota((c, c), 0)
    tj = _iota((c, c), 1)
    same = (ti ^ tj) < seq_len
    incl = same & (tj <= ti)
    strict = same & (tj < ti)
    incl_bf = incl.astype(BF16)
    same_bf = same.astype(BF16)
    incl_t_bf = (same & (ti <= tj)).astype(BF16)

    probs = []
    for ci in range(n_chunks):
        r = slice(ci * c, (ci + 1) * c)
        lgc = lg_all[r]
        gb_all = _mm_mask_lhs(incl_bf, lgc)
        gtot_all = _mm_mask_lhs(same_bf, lgc)
        gb_rows = _mm_mask_rhs(lgc, incl_t_bf, TN)
        for h in range(nh):
            probs.append(dict(ci=ci, h=h, q=q_h[h][r], k=k_h[h][r], v=v_h[h][r], beta=pick(beta_all[r], h),
                              gb=pick(gb_all, nh + h), gtot=pick(gtot_all, nh + h),
                              gb_row=gb_rows[nh + h:nh + h + 1, :]))

    decays = [jnp.exp(jnp.where(incl, p["gb"] - p["gb_row"], -jnp.inf)) for p in probs]
    kbetas = [p["k"] * p["beta"] for p in probs]
    kqs = [_mm(jnp.concatenate([kb, p["q"]], axis=0), p["k"], NT) for kb, p in zip(kbetas, probs)]
    neg_ls = [jnp.where(strict, -(kq[:c] * d), 0.0) for kq, d in zip(kqs, decays)]
    attns = [kq[c:] * d for kq, d in zip(kqs, decays)]
    e_gs = [jnp.exp(p["gb"]) for p in probs]
    if seq_len == c:
        t_invs = _unit_lower_inverses_halved(neg_ls)
    else:
        t_invs = _unit_lower_inverses(neg_ls, ti, tj, seq_len, _mm_acc)
    sols =[_mm_acc(t, jnp.concatenate([p["v"] * p["beta"], kb * eg], axis=1))
            for t, p, kb, eg in zip(t_invs, probs, kbetas, e_gs)]
    atts = [_mm(a, sol) for a, sol in zip(attns, sols)]
    q_effs = [p["q"] * eg - att[:, hd:] for p, eg, att in zip(probs, e_gs, atts)]
    k_decs = [p["k"] * jnp.exp(p["gtot"] - p["gb"]) for p in probs]
    g_lasts = [jnp.exp(p["gtot"]) for p in probs]
    nms = [[_mm(kd[n * seq_len:(n + 1) * seq_len], sol[n * seq_len:(n + 1) * seq_len], TN) for n in range(n_seg)]
           for kd, sol in zip(k_decs, sols)]

    states = [s_scr[h] for h in range(nh)] if carry_mode else None
    o_rows = [[] for _ in range(nh)]
    for idx, p in enumerate(probs):
        h, ci = p["h"], p["ci"]
        for n in range(n_seg):
            rs = slice(n * seq_len, (n + 1) * seq_len)
            seq = ci * n_seg + n
            s = states[h] if carry_mode else s_in_ref[seq, h]
            o_rows[h].append(_mm(q_effs[idx][rs], s) + atts[idx][rs, :hd])
            nm = nms[idx][n]
            s_new = s * g_lasts[idx][n * seq_len:n * seq_len + 1] - _mm(nm[:, hd:], s) + nm[:, :hd]
            if carry_mode:
                states[h] = s_new
            else:
                s_out_ref[seq, h] = s_new

    nw = nw_ref[...]
    for h in range(nh):
        o = _cat(o_rows[h], 0)
        o = o * lax.rsqrt(lane_sum(o * o) * (1.0 / hd) + NORM_EPS) * nw
        o_ref[:, h * hd:(h + 1) * hd] = o * _silu(z_ref[:, h * hd:(h + 1) * hd])

    if carry_mode:
        for h in range(nh):
            s_scr[h] = states[h]

        @pl.when(t_idx == pl.num_programs(1) - 1)
        def _():
            for h in range(nh):
                s_out_ref[h] = states[h]


def _gdn_mixer(qkv, z, ba, conv_w, gparams, norm_w, halo, s_in, *, rows, chunk, seq_len, carry_mode):
    bsz, t_len, _ = qkv.shape
    grid = (bsz, t_len // rows)
    tok = lambda width: pl.BlockSpec((None, rows, width), lambda b, t: (b, t, 0))
    const = lambda shape: pl.BlockSpec(shape, lambda b, t: (0,) * len(shape))
    in_specs = [tok(GDN_QKV), tok(GDN_WIDTH), tok(BA_PAD), const((GDN_CONV, GDN_QKV)), const((2, BA_PAD)),
                const((1, GDN_HEAD_DIM))]
    args = [qkv, z, ba, conv_w, gparams, norm_w]
    state_tail = (GDN_HEADS, GDN_HEAD_DIM, GDN_HEAD_DIM)
    if carry_mode:
        in_specs.append(pl.BlockSpec((None, SUBLANES, GDN_QKV), lambda b, t: (b, 0, 0)))
        args.append(halo)
        state_spec = pl.BlockSpec((None,) + state_tail, lambda b, t: (b, 0, 0, 0))
        scratch = [pltpu.VMEM(state_tail, F32), pltpu.VMEM((SUBLANES, GDN_QKV), F32)]
    else:
        in_specs += [tok(GDN_QKV)] * (GDN_CONV - 1)
        args += list(halo)
        state_spec = pl.BlockSpec((rows // seq_len,) + state_tail, lambda b, t: (t, 0, 0, 0))
        scratch = []
    in_specs.append(state_spec)
    args.append(s_in)
    return pl.pallas_call(
        functools.partial(_gdn_kernel, rows=rows, chunk=chunk, seq_len=seq_len, carry_mode=carry_mode),
        grid=grid,
        in_specs=in_specs,
        out_specs=[tok(GDN_WIDTH), state_spec],
        out_shape=[jax.ShapeDtypeStruct((bsz, t_len, GDN_WIDTH), F32),
                   jax.ShapeDtypeStruct(s_in.shape, F32)],
        scratch_shapes=scratch,
        compiler_params=pltpu.CompilerParams(dimension_semantics=("arbitrary",) * 2,
                                             vmem_limit_bytes=VMEM_LIMIT_BYTES),
        name="gdn_mixer",
    )(*args)


def _rwkv_kernel(*refs, rows, chunk, seq_len, carry_mode):
    (rkv_ref, lora_ref, mur_ref, mul_ref, par_ref, w2_ref, a2_ref, g2_ref, hr_ref, hl_ref, s_in_ref,
     o_ref, s_out_ref) = refs[:13]
    scratch = refs[13:]
    hd, npair, c, wd_all = RWKV_HEAD_DIM, RWKV_PAIRS, chunk, RWKV_WIDTH
    pw = 2 * hd
    n_chunks = rows // chunk
    n_seg = chunk // seq_len
    t_idx = pl.program_id(1)

    if carry_mode:
        s_scr, cr_scr, cl_scr = scratch

        @pl.when(t_idx == 0)
        def _():
            s_scr[...] = s_in_ref[...]
            cr_scr[...] = hr_ref[...]
            cl_scr[...] = hl_ref[...]

    def token_shift(x_ref, mu_ref, h_ref, c_scr):
        x = x_ref[...]
        if carry_mode:
            prev = _shift_rows_carry(x, 1, c_scr[...])
            c_scr[...] = x[rows - SUBLANES:]
        else:
            prev = _shift_rows_fix(x, 1, h_ref[...], seq_len)
        return x + mu_ref[...] * (prev - x)

    xs = token_shift(rkv_ref, mur_ref, hr_ref, cr_scr if carry_mode else None)
    lora = token_shift(lora_ref, mul_ref, hl_ref, cl_scr if carry_mode else None)
    r, kr, vr = xs[:, :wd_all], xs[:, wd_all:2 * wd_all], xs[:, 2 * wd_all:]
    par = par_ref[...]
    w0, a0, k_k, k_a, r_k, ln_w, ln_b = (par[n:n + 1] for n in range(7))
    wd = lora[:, :DECAY_LORA]
    ad = lora[:, DECAY_LORA:DECAY_LORA + AAA_LORA]
    gd = lora[:, DECAY_LORA + AAA_LORA:]
    lw = -(DECAY_SCALE * _sigmoid(w0 + _mm(jnp.tanh(wd), w2_ref[...])))
    asig = _sigmoid(a0 + _mm(ad, a2_ref[...]))
    g = _mm(_sigmoid(gd), g2_ref[...])

    li = _iota((pw, pw), 0)
    lj = _iota((pw, pw), 1)
    pair_bd = (li >= hd) == (lj >= hd)
    head_ones = pair_bd.astype(BF16)

    def head_sum(a):
        return _cat([_mm(a[:, p * pw:(p + 1) * pw], head_ones) for p in range(a.shape[1] // pw)], 1)

    kk = kr * k_k
    kk = kk * lax.rsqrt(head_sum(kk * kk) + L2_EPS)
    kr = kr * (1.0 + (asig - 1.0) * k_a)
    a_vec = -kk
    b_vec = kk * asig

    i1 = _iota((c, c), 0)
    j1 = _iota((c, c), 1)
    same1 = (i1 ^ j1) < seq_len
    incl1_bf = (same1 & (j1 <= i1)).astype(BF16)
    same1_bf = same1.astype(BF16)
    ip = _iota((c, 2 * c), 0)
    jp_full = _iota((c, 2 * c), 1)
    jp = jp_full & (c - 1)
    head1_cols = jp_full >= c
    same_p = (ip ^ jp) < seq_len
    incl_p = same_p & (jp <= ip)
    strict_p = same_p & (jp < ip)

    def stack_heads(a):
        head1 = (_iota(a.shape, 1) & hd) != 0
        return jnp.concatenate([jnp.where(head1, 0.0, a), jnp.where(head1, a, 0.0)], axis=0)

    def block_diag(m):
        return jnp.concatenate([jnp.where(head1_cols, 0.0, m), jnp.where(head1_cols, m, 0.0)], axis=0)

    pair_mul = lambda a, b: _mm_acc(a, block_diag(b))

    probs = []
    for ci in range(n_chunks):
        rs = slice(ci * c, (ci + 1) * c)
        lwc = lw[rs]
        g_incl = _mm_mask_lhs(incl1_bf, lwc)
        g_tot = _mm_mask_lhs(same1_bf, lwc)
        g_excl = g_incl - lwc
        g_mid = 0.5 * g_tot
        inv_n = jnp.exp(g_mid - g_incl)
        to_end = jnp.exp(g_tot - g_incl)
        full = dict(r_n=r[rs] * jnp.exp(g_incl - g_mid), a_n=a_vec[rs] * jnp.exp(g_excl - g_mid),
                    b_n=b_vec[rs] * inv_n, k_n=kr[rs] * inv_n, r_t=r[rs] * jnp.exp(g_incl),
                    a_t=a_vec[rs] * jnp.exp(g_excl), b_end=b_vec[rs] * to_end, k_end=kr[rs] * to_end,
                    gam=jnp.exp(g_tot), v=vr[rs])
        for p in range(npair):
            prob = {name: val[:, p * pw:(p + 1) * pw] for name, val in full.items()}
            prob.update(ci=ci, p=p)
            probs.append(prob)

    ars = [jnp.concatenate([p["a_n"], p["r_n"]], axis=0) for p in probs]
    m_bs = [_mm(ar, stack_heads(p["b_n"]), NT) for ar, p in zip(ars, probs)]
    m_ks = [_mm(ar, stack_heads(p["k_n"]), NT) for ar, p in zip(ars, probs)]
    a_abs = [jnp.where(strict_p, m[:c], 0.0) for m in m_bs]
    a_rbs = [jnp.where(incl_p, m[c:], 0.0) for m in m_bs]
    a_aks = [jnp.where(strict_p, m[:c], 0.0) for m in m_ks]
    a_rks = [jnp.where(incl_p, m[c:], 0.0) for m in m_ks]
    t_invs = _unit_lower_inverses(a_abs, ip, jp, seq_len, pair_mul)
    zvs = [_mm(a_ak, stack_heads(p["v"])) for a_ak, p in zip(a_aks, probs)]
    uws = [_mm_acc(t, stack_heads(jnp.concatenate([zv, p["a_t"]], axis=1)))
           for t, zv, p in zip(t_invs, zvs, probs)]
    rbs = [_mm(a_rb, stack_heads(uw)) for a_rb, uw in zip(a_rbs, uws)]
    y_frees = [rb[:, :pw] + _mm(a_rk, stack_heads(p["v"])) for rb, a_rk, p in zip(rbs, a_rks, probs)]
    r_effs = [p["r_t"] + rb[:, pw:] for p, rb in zip(probs, rbs)]
    m_corrs, n_adds = [], []
    for p, uw in zip(probs, uws):
        ms, ns = [], []
        for n in range(n_seg):
            sr = slice(n * seq_len, (n + 1) * seq_len)
            ms.append(jnp.where(pair_bd, _mm_acc(uw[sr, pw:], p["b_end"][sr], TN), 0.0))
            lhs = jnp.concatenate([uw[sr, :pw], p["v"][sr]], axis=0)
            rhs = jnp.concatenate([p["b_end"][sr], p["k_end"][sr]], axis=0)
            ns.append(jnp.where(pair_bd, _mm_acc(lhs, rhs, TN), 0.0))
        m_corrs.append(ms)
        n_adds.append(ns)

    states = [s_scr[p] for p in range(npair)] if carry_mode else None
    y_rows = [[] for _ in range(npair)]
    for idx, prob in enumerate(probs):
        p, ci = prob["p"], prob["ci"]
        for n in range(n_seg):
            sr = slice(n * seq_len, (n + 1) * seq_len)
            seq = ci * n_seg + n
            s = states[p] if carry_mode else s_in_ref[seq, p]
            y_rows[p].append(_mm_acc(r_effs[idx][sr], s, NT) + y_frees[idx][sr])
            gam = prob["gam"][n * seq_len:n * seq_len + 1]
            s_new = s * gam + _mm_acc(s, m_corrs[idx][n]) + n_adds[idx][n]
            if carry_mode:
                states[p] = s_new
            else:
                s_out_ref[seq, p] = s_new

    y = _cat([_cat(y_rows[p], 0) for p in range(npair)], 1)
    y_c = y - head_sum(y) * (1.0 / hd)
    y_n = y_c * lax.rsqrt(head_sum(y_c * y_c) * (1.0 / hd) + RWKV_LN_EPS)
    y_n = y_n * ln_w + ln_b
    bonus = head_sum(r * kr * r_k) * vr
    o_ref[...] = (y_n + bonus) * g

    if carry_mode:
        for p in range(npair):
            s_scr[p] = states[p]

        @pl.when(t_idx == pl.num_programs(1) - 1)
        def _():
            for p in range(npair):
                s_out_ref[p] = states[p]


def _rwkv_mixer(rkv, lora, mu_rkv, mu_lora, params, w2, a2, g2, halo_rkv, halo_lora, s_in, *,
                rows, chunk, seq_len, carry_mode):
    bsz, t_len, _ = rkv.shape
    grid = (bsz, t_len // rows)
    tok = lambda width: pl.BlockSpec((None, rows, width), lambda b, t: (b, t, 0))
    const = lambda shape: pl.BlockSpec(shape, lambda b, t: (0,) * len(shape))
    in_specs = [tok(3 * RWKV_WIDTH), tok(LORA_PAD), const((1, 3 * RWKV_WIDTH)), const((1, LORA_PAD)),
                const((SUBLANES, RWKV_WIDTH)), const((DECAY_LORA, RWKV_WIDTH)), const((AAA_LORA, RWKV_WIDTH)),
                const((LORA_PAD - DECAY_LORA - AAA_LORA, RWKV_WIDTH))]
    state_tail = (RWKV_PAIRS, LANES, LANES)
    if carry_mode:
        in_specs += [pl.BlockSpec((None, SUBLANES, 3 * RWKV_WIDTH), lambda b, t: (b, 0, 0)),
                     pl.BlockSpec((None, SUBLANES, LORA_PAD), lambda b, t: (b, 0, 0))]
        state_spec = pl.BlockSpec((None,) + state_tail, lambda b, t: (b, 0, 0, 0))
        scratch = [pltpu.VMEM(state_tail, F32), pltpu.VMEM((SUBLANES, 3 * RWKV_WIDTH), F32),
                   pltpu.VMEM((SUBLANES, LORA_PAD), F32)]
    else:
        in_specs += [tok(3 * RWKV_WIDTH), tok(LORA_PAD)]
        state_spec = pl.BlockSpec((rows // seq_len,) + state_tail, lambda b, t: (t, 0, 0, 0))
        scratch = []
    in_specs.append(state_spec)
    args = [rkv, lora, mu_rkv, mu_lora, params, w2, a2, g2, halo_rkv, halo_lora, s_in]
    return pl.pallas_call(
        functools.partial(_rwkv_kernel, rows=rows, chunk=chunk, seq_len=seq_len, carry_mode=carry_mode),
        grid=grid,
        in_specs=in_specs,
        out_specs=[tok(RWKV_WIDTH), state_spec],
        out_shape=[jax.ShapeDtypeStruct((bsz, t_len, RWKV_WIDTH), F32),
                   jax.ShapeDtypeStruct(s_in.shape, F32)],
        scratch_shapes=scratch,
        compiler_params=pltpu.CompilerParams(dimension_semantics=("arbitrary",) * 2,
                                             vmem_limit_bytes=VMEM_LIMIT_BYTES),
        name="rwkv_mixer",
    )(*args)


def _ffn_kernel(*refs, rows, seq_len, blocks_per_seq, carry_mode):
    x_ref, og_ref, orw_ref, wo_ref, nf_ref, wu_ref, cw_ref, wd_ref, nfin_ref = refs[:9]
    if carry_mode:
        h_ref, y_ref, tail_ref, carry_scr = refs[9:]
    else:
        f1_ref, f2_ref, y_ref, tail_ref = refs[9:]
    blk = pl.program_id(0)

    if carry_mode:
        @pl.when(blk % blocks_per_seq == 0)
        def _():
            carry_scr[...] = h_ref[...]

    o = (jnp.dot(og_ref[...].astype(BF16), wo_ref[:GDN_WIDTH], preferred_element_type=F32)
         + jnp.dot(orw_ref[...].astype(BF16), wo_ref[GDN_WIDTH:], preferred_element_type=F32))
    x1 = x_ref[...] + o
    h2 = _rms_norm(x1, nf_ref[...]).astype(BF16)
    gate = jnp.dot(h2, wu_ref[:, :FFN_DIM], preferred_element_type=F32)
    val = jnp.dot(h2, wu_ref[:, FFN_DIM:], preferred_element_type=F32)
    cw = cw_ref[...]
    conv = gate * cw[FFN_CONV - 1:FFN_CONV]
    for s in range(1, FFN_CONV):
        if carry_mode:
            gs = _shift_rows_carry(gate, s, carry_scr[...])
        else:
            gs = _shift_rows_fix(gate, s, (f1_ref, f2_ref)[s - 1][...], seq_len)
        conv = conv + gs * cw[FFN_CONV - 1 - s:FFN_CONV - s]
    if carry_mode:
        carry_scr[...] = gate[rows - SUBLANES:]
        tail_ref[...] = gate[rows - SUBLANES:]
    else:
        tail_ref[...] = gate
    act = (_silu(conv) * val).astype(BF16)
    x2 = x1 + jnp.dot(act, wd_ref[...], preferred_element_type=F32)
    y_ref[...] = _rms_norm(x2, nfin_ref[...])


def _ffn(x2d, o_gdn, o_rwkv, w_out, norm_ffn, w_up, conv_w, w_down, norm_final, halo, *,
         rows, seq_len, carry_mode):
    n_rows = x2d.shape[0]
    n_blocks = n_rows // rows
    row_spec = lambda w: pl.BlockSpec((rows, w), lambda i: (i, 0))
    whole = lambda shape: pl.BlockSpec(shape, lambda i: (0,) * len(shape), pipeline_mode=pl.Buffered(1))
    in_specs = [row_spec(D_MODEL), row_spec(GDN_WIDTH), row_spec(RWKV_WIDTH),
                whole((D_MODEL, D_MODEL)), whole((1, D_MODEL)), whole((D_MODEL, 2 * FFN_DIM)),
                whole((FFN_CONV, FFN_DIM)), whole((FFN_DIM, D_MODEL)), whole((1, D_MODEL))]
    args = [x2d, o_gdn, o_rwkv, w_out, norm_ffn, w_up, conv_w, w_down, norm_final]
    if carry_mode:
        blocks_per_seq = seq_len // rows
        in_specs.append(pl.BlockSpec((None, SUBLANES, FFN_DIM), lambda i: (i // blocks_per_seq, 0, 0)))
        args.append(halo)
        tail_spec = pl.BlockSpec((None, SUBLANES, FFN_DIM), lambda i: (i, 0, 0))
        tail_shape = jax.ShapeDtypeStruct((n_blocks, SUBLANES, FFN_DIM), F32)
        scratch = [pltpu.VMEM((SUBLANES, FFN_DIM), F32)]
    else:
        blocks_per_seq = 1
        in_specs += [row_spec(FFN_DIM), row_spec(FFN_DIM)]
        args += [halo[0], halo[1]]
        tail_spec = row_spec(FFN_DIM)
        tail_shape = jax.ShapeDtypeStruct((n_rows, FFN_DIM), F32)
        scratch = []
    return pl.pallas_call(
        functools.partial(_ffn_kernel, rows=rows, seq_len=seq_len, blocks_per_seq=blocks_per_seq,
                          carry_mode=carry_mode),
        grid=(n_blocks,),
        in_specs=in_specs,
        out_specs=[row_spec(D_MODEL), tail_spec],
        out_shape=[jax.ShapeDtypeStruct((n_rows, D_MODEL), F32), tail_shape],
        scratch_shapes=scratch,
        compiler_params=pltpu.CompilerParams(dimension_semantics=("arbitrary",),
                                             vmem_limit_bytes=VMEM_LIMIT_BYTES),
        name="ffn",
    )(*args)


def _pad_cols(a, width):
    return jnp.pad(a, ((0, 0), (0, width - a.shape[1])))


def _regroup_in_weights(w_in):
    g_end = GDN_QKV + GDN_WIDTH
    ba = w_in[:, g_end:g_end + 2 * GDN_HEADS]
    rw = w_in[:, g_end + 2 * GDN_HEADS:]
    return jnp.concatenate([w_in[:, :g_end], rw[:, :3 * RWKV_WIDTH],
                            _pad_cols(rw[:, 3 * RWKV_WIDTH:], LORA_PAD), _pad_cols(ba, BA_PAD)],
                           axis=1).astype(BF16)


def _tail_tile(state_rows):
    return jnp.pad(state_rows, ((0, 0), (SUBLANES - state_rows.shape[1], 0), (0, 0)))


def _fix_rows(state_rows, s, seq_len):
    n = state_rows.shape[1]
    first = state_rows[:, n - s:, :]
    return jnp.pad(first, ((0, 0), (0, seq_len - s), (0, 0))).reshape(-1, state_rows.shape[2])


def _pair_states(s):
    bsz = s.shape[0]
    sp = s.reshape(bsz, RWKV_PAIRS, 2, RWKV_HEAD_DIM, RWKV_HEAD_DIM)
    zero = jnp.zeros_like(sp[:, :, 0])
    top = jnp.concatenate([sp[:, :, 0], zero], axis=-1)
    bot = jnp.concatenate([zero, sp[:, :, 1]], axis=-1)
    return jnp.concatenate([top, bot], axis=-2)


def _unpair_states(sp):
    hd = RWKV_HEAD_DIM
    bsz = sp.shape[0]
    return jnp.stack([sp[:, :, :hd, :hd], sp[:, :, hd:, hd:]], axis=2).reshape(bsz, RWKV_HEADS, hd, hd)


def _layer(x, st_gdn_conv, st_gdn, st_shift, st_rwkv, st_ffn, wts, *, carry_mode):
    bsz, t_len, _ = x.shape
    n_rows = bsz * t_len
    x2d = x.reshape(n_rows, D_MODEL)
    qkv, z, rkv, lora, ba = _inproj(x2d, wts["norm_mix"], wts["w_in"])

    shift_rkv = st_shift[:, :, :3 * RWKV_WIDTH]
    shift_lora = _pad_cols(st_shift[:, 0, 3 * RWKV_WIDTH:], LORA_PAD)[:, None, :]
    if carry_mode:
        shape3 = lambda a: a.reshape(bsz, t_len, a.shape[-1])
        o_gdn, s_gdn = _gdn_mixer(shape3(qkv), shape3(z), shape3(ba), wts["gdn_conv_w"], wts["gparams"],
                                  wts["gdn_norm"], _tail_tile(st_gdn_conv), st_gdn,
                                  rows=MIXER_ROWS, chunk=GDN_CHUNK, seq_len=GDN_CHUNK, carry_mode=True)
        o_rwkv, s_rwkv = _rwkv_mixer(shape3(rkv), shape3(lora), wts["mu_rkv"], wts["mu_lora"], wts["rparams"],
                                     wts["w2"], wts["a2"], wts["g2"], _tail_tile(shift_rkv),
                                     _tail_tile(shift_lora), _pair_states(st_rwkv),
                                     rows=MIXER_ROWS, chunk=RWKV_CHUNK, seq_len=RWKV_CHUNK, carry_mode=True)
        ffn_halo = _tail_tile(st_ffn)
        ffn_rows, ffn_seq = ROW_BLOCK, t_len
    else:
        shape3 = lambda a: a.reshape(1, n_rows, a.shape[-1])
        fix = lambda st, s: _fix_rows(st, s, t_len)[None]
        o_gdn, s_gdn = _gdn_mixer(shape3(qkv), shape3(z), shape3(ba), wts["gdn_conv_w"], wts["gparams"],
                                  wts["gdn_norm"], [fix(st_gdn_conv, s) for s in range(1, GDN_CONV)], st_gdn,
                                  rows=GDN_CHUNK, chunk=GDN_CHUNK, seq_len=t_len, carry_mode=False)
        o_rwkv, s_rwkv = _rwkv_mixer(shape3(rkv), shape3(lora), wts["mu_rkv"], wts["mu_lora"], wts["rparams"],
                                     wts["w2"], wts["a2"], wts["g2"], fix(shift_rkv, 1), fix(shift_lora, 1),
                                     _pair_states(st_rwkv),
                                     rows=RWKV_CHUNK, chunk=RWKV_CHUNK, seq_len=t_len, carry_mode=False)
        ffn_halo = [_fix_rows(st_ffn, s, t_len) for s in range(1, FFN_CONV)]
        ffn_rows, ffn_seq = n_rows, t_len
    y2d, gate_tail = _ffn(x2d, o_gdn.reshape(n_rows, GDN_WIDTH), o_rwkv.reshape(n_rows, RWKV_WIDTH),
                          wts["w_out"], wts["norm_ffn"], wts["w_up"], wts["ffn_conv_w"], wts["w_down"],
                          wts["norm_final"], ffn_halo, rows=ffn_rows, seq_len=ffn_seq, carry_mode=carry_mode)

    qkv3 = qkv.reshape(bsz, t_len, GDN_QKV)
    gdn_conv_new = qkv3[:, t_len - (GDN_CONV - 1):, :]
    shift_new = jnp.concatenate([rkv.reshape(bsz, t_len, -1)[:, t_len - 1:, :],
                                 lora.reshape(bsz, t_len, -1)[:, t_len - 1:, :RWKV_PROJ - 3 * RWKV_WIDTH]], axis=-1)
    if carry_mode:
        blocks_per_seq = t_len // ROW_BLOCK
        tails = gate_tail.reshape(bsz, blocks_per_seq, SUBLANES, FFN_DIM)[:, -1]
        ffn_new = tails[:, SUBLANES - (FFN_CONV - 1):, :]
    else:
        ffn_new = gate_tail.reshape(bsz, t_len, FFN_DIM)[:, t_len - (FFN_CONV - 1):, :]
    return (y2d.reshape(bsz, t_len, D_MODEL), gdn_conv_new, s_gdn, shift_new, _unpair_states(s_rwkv), ffn_new)


def kernel(x_prompt, x_sample, state_gdn_conv, state_gdn, state_rwkv_shift, state_rwkv, state_ffn_conv,
           norm_mix, w_in, gdn_conv_w, gdn_a_log, gdn_dt_bias, gdn_norm,
           rwkv_mu, rwkv_w0, rwkv_w2, rwkv_a0, rwkv_a2, rwkv_g2, rwkv_k_k, rwkv_k_a, rwkv_r_k,
           rwkv_ln_w, rwkv_ln_b, w_out, norm_ffn, w_up, ffn_conv_w, w_down, norm_final):
    layer = 0
    gparams = jnp.zeros((2, BA_PAD), F32)
    gparams = gparams.at[0, GDN_HEADS:2 * GDN_HEADS].set(gdn_a_log[layer])
    gparams = gparams.at[1, GDN_HEADS:2 * GDN_HEADS].set(gdn_dt_bias[layer])
    mu = rwkv_mu[layer][None, :]
    rparams = jnp.stack([rwkv_w0[layer], rwkv_a0[layer], rwkv_k_k[layer], rwkv_k_a[layer],
                         rwkv_r_k[layer].reshape(-1), rwkv_ln_w[layer], rwkv_ln_b[layer],
                         jnp.zeros((RWKV_WIDTH,), F32)])
    gate_rows = LORA_PAD - DECAY_LORA - AAA_LORA
    wts = {
        "norm_mix": norm_mix[layer][None, :],
        "w_in": _regroup_in_weights(w_in[layer]),
        "gdn_conv_w": gdn_conv_w[layer],
        "gparams": gparams,
        "gdn_norm": gdn_norm[layer][None, :],
        "mu_rkv": mu[:, :3 * RWKV_WIDTH],
        "mu_lora": _pad_cols(mu[:, 3 * RWKV_WIDTH:], LORA_PAD),
        "rparams": rparams,
        "w2": rwkv_w2[layer].astype(BF16),
        "a2": rwkv_a2[layer].astype(BF16),
        "g2": jnp.pad(rwkv_g2[layer], ((0, gate_rows - GATE_LORA), (0, 0))).astype(BF16),
        "w_out": w_out[layer].astype(BF16),
        "norm_ffn": norm_ffn[layer][None, :],
        "w_up": w_up[layer].astype(BF16),
        "ffn_conv_w": ffn_conv_w[layer],
        "w_down": w_down[layer].astype(BF16),
        "norm_final": norm_final[None, :],
    }
    n_prompt = x_prompt.shape[0]
    zeros = lambda c: jnp.zeros((n_prompt,) + c.shape[2:], c.dtype)
    outs_p = _layer(x_prompt, zeros(state_gdn_conv), zeros(state_gdn), zeros(state_rwkv_shift),
                    zeros(state_rwkv), zeros(state_ffn_conv), wts, carry_mode=True)
    outs_s = _layer(x_sample, state_gdn_conv[layer], state_gdn[layer], state_rwkv_shift[layer],
                    state_rwkv[layer], state_ffn_conv[layer], wts, carry_mode=False)
    y_p, y_s = outs_p[0], outs_s[0]
    states = []
    for n in range(1, 6):
        states += [outs_p[n][None], outs_s[n][None]]
    return (y_p, y_s, *states)
```

```python
import functools
import math

import jax
import jax.numpy as jnp
from jax import lax
from jax.experimental import pallas as pl
from jax.experimental.pallas import tpu as pltpu

F32 = jnp.float32
BF16 = jnp.bfloat16

D_MODEL = 1024
GDN_HEAD_DIM = 128
GDN_HEADS = 4
GDN_WIDTH = GDN_HEADS * GDN_HEAD_DIM
GDN_QKV = 3 * GDN_WIDTH
GDN_CONV = 4
RWKV_HEAD_DIM = 64
RWKV_HEADS = 8
RWKV_WIDTH = RWKV_HEADS * RWKV_HEAD_DIM
RWKV_PAIRS = RWKV_HEADS // 2
DECAY_LORA = 64
AAA_LORA = 64
GATE_LORA = 160
RWKV_PROJ = 3 * RWKV_WIDTH + DECAY_LORA + AAA_LORA + GATE_LORA
FFN_DIM = 2816
FFN_CONV = 3
NORM_EPS = 1e-6
L2_EPS = 1e-6
RWKV_LN_EPS = 64e-5
DECAY_SCALE = math.exp(-0.5)

LANES = 128
SUBLANES = 8
VMEM_LIMIT_BYTES = 56 * 1024 * 1024

LORA_PAD = 384
BA_PAD = LANES
P_PAD = GDN_QKV + GDN_WIDTH + 3 * RWKV_WIDTH + LORA_PAD + BA_PAD
OFF_QKV = 0
OFF_Z = OFF_QKV + GDN_QKV
OFF_RKV = OFF_Z + GDN_WIDTH
OFF_LORA = OFF_RKV + 3 * RWKV_WIDTH
OFF_BA = OFF_LORA + LORA_PAD

INPROJ_ROWS = 512
ROW_BLOCK = 256
MIXER_ROWS = 512
GDN_CHUNK = 128
RWKV_CHUNK = 64

NN = (((1,), (0,)), ((), ()))
NT = (((1,), (1,)), ((), ()))
TN = (((0,), (0,)), ((), ()))


def _dot(a, b, dims):
    return lax.dot_general(a, b, dims, preferred_element_type=F32)


def _mm(a, b, dims=NN):
    return _dot(a.astype(BF16), b.astype(BF16), dims)


def _split2(a):
    hi = a.astype(BF16)
    return hi, (a - hi.astype(F32)).astype(BF16)


def _mm_mask_lhs(mask_bf16, x, dims=NN):
    hi, lo = _split2(x)
    return _dot(mask_bf16, hi, dims) + _dot(mask_bf16, lo, dims)


def _mm_mask_rhs(x, mask_bf16, dims=NN):
    hi, lo = _split2(x)
    return _dot(hi, mask_bf16, dims) + _dot(lo, mask_bf16, dims)


_mm_acc = _mm


def _iota(shape, dim):
    return lax.broadcasted_iota(jnp.int32, shape, dim)


def _sigmoid(x):
    return 0.5 + 0.5 * jnp.tanh(0.5 * x)


def _silu(x):
    h = 0.5 * x
    return h + h * jnp.tanh(h)


def _softplus(x):
    return jnp.maximum(x, 0.0) + jnp.log1p(jnp.exp(-jnp.abs(x)))


def _rms_norm(x, w):
    return x * lax.rsqrt(jnp.mean(x * x, axis=-1, keepdims=True) + NORM_EPS) * w


def _cat(parts, axis):
    return parts[0] if len(parts) == 1 else jnp.concatenate(parts, axis=axis)


def _shift_rows_carry(x, s, carry):
    xr = pltpu.roll(x, s, 0)
    cr = pltpu.roll(carry, s, 0)
    row = _iota((SUBLANES, x.shape[1]), 0)
    head = jnp.where(row < s, cr, xr[:SUBLANES])
    return jnp.concatenate([head, xr[SUBLANES:]], axis=0)


def _shift_rows_fix(x, s, fix, seq_len):
    xr = pltpu.roll(x, s, 0)
    row = _iota(x.shape, 0)
    return jnp.where((row & (seq_len - 1)) < s, fix, xr)


def _unit_lower_inverses(ps, ti, tj, seq_len, mul):
    def joins(s):
        return ((ti ^ tj) < 2 * s) & ((ti & s) != 0) & ((tj & s) == 0)

    eye = jnp.where(ti == tj, 1.0, 0.0)
    ts = [eye + jnp.where(joins(1), p, 0.0) for p in ps]
    s = 2
    while s < seq_len:
        mask = joins(s)
        tbs = [mul(t, jnp.where(mask, p, 0.0)) for t, p in zip(ts, ps)]
        ts = [t + mul(tb, t) for tb, t in zip(tbs, ts)]
        s *= 2
    return ts


def _unit_lower_inverses_halved(ps):
    c = ps[0].shape[0]
    half = c // 2
    ti = _iota((half, c), 0)
    lane = _iota((half, c), 1)
    right = lane >= half
    zeros = jnp.zeros((half, c), F32)

    def block_diag(m):
        return jnp.concatenate([jnp.where(right, 0.0, m), jnp.where(right, m, 0.0)], axis=0)

    diag = [jnp.where(right, p[half:], p[:half]) for p in ps]
    t_ab = _unit_lower_inverses(diag, ti, lane & (half - 1), half, lambda a, b: _mm_acc(a, block_diag(b)))
    ys = [_mm_acc(jnp.where(right, 0.0, p[half:]), jnp.concatenate([jnp.where(right, 0.0, t), zeros], axis=0))
          for p, t in zip(ps, t_ab)]
    x21s = [_mm_acc(t, jnp.concatenate([zeros, y], axis=0)) for t, y in zip(t_ab, ys)]
    return [jnp.concatenate([jnp.where(right, 0.0, t), x21 + jnp.where(right, t, 0.0)], axis=0)
            for t, x21 in zip(t_ab, x21s)]


def _inproj_kernel(x_ref, nw_ref, w_ref, qkv_ref, z_ref, rkv_ref, lora_ref, ba_ref):
    h = _rms_norm(x_ref[...], nw_ref[...]).astype(BF16)
    for out_ref, off in ((qkv_ref, OFF_QKV), (z_ref, OFF_Z), (rkv_ref, OFF_RKV), (lora_ref, OFF_LORA),
                         (ba_ref, OFF_BA)):
        width = out_ref.shape[1]
        out_ref[...] = jnp.dot(h, w_ref[:, off:off + width], preferred_element_type=F32)


def _inproj(x2d, norm_w, w_perm):
    n_rows = x2d.shape[0]
    rows = min(INPROJ_ROWS, n_rows)
    widths = (GDN_QKV, GDN_WIDTH, 3 * RWKV_WIDTH, LORA_PAD, BA_PAD)
    return pl.pallas_call(
        _inproj_kernel,
        grid=(n_rows // rows,),
        in_specs=[
            pl.BlockSpec((rows, D_MODEL), lambda i: (i, 0)),
            pl.BlockSpec((1, D_MODEL), lambda i: (0, 0)),
            pl.BlockSpec((D_MODEL, P_PAD), lambda i: (0, 0), pipeline_mode=pl.Buffered(1)),
        ],
        out_specs=[pl.BlockSpec((rows, w), lambda i: (i, 0)) for w in widths],
        out_shape=[jax.ShapeDtypeStruct((n_rows, w), F32) for w in widths],
        compiler_params=pltpu.CompilerParams(dimension_semantics=("arbitrary",),
                                             vmem_limit_bytes=VMEM_LIMIT_BYTES),
        name="inproj",
    )(x2d, norm_w, w_perm)


def _gdn_kernel(*refs, rows, chunk, seq_len, carry_mode):
    qkv_ref, z_ref, ba_ref, cw_ref, gp_ref, nw_ref = refs[:6]
    n_halo = 1 if carry_mode else GDN_CONV - 1
    halo_refs = refs[6:6 + n_halo]
    s_in_ref, o_ref, s_out_ref = refs[6 + n_halo:9 + n_halo]
    scratch = refs[9 + n_halo:]
    nh, hd, c = GDN_HEADS, GDN_HEAD_DIM, chunk
    n_chunks = rows // chunk
    n_seg = chunk // seq_len
    t_idx = pl.program_id(1)

    if carry_mode:
        s_scr, c_scr = scratch

        @pl.when(t_idx == 0)
        def _():
            s_scr[...] = s_in_ref[...]
            c_scr[...] = halo_refs[0][...]

    x = qkv_ref[...]
    w = cw_ref[...]
    acc = x * w[GDN_CONV - 1:GDN_CONV]
    for s in range(1, GDN_CONV):
        if carry_mode:
            xs = _shift_rows_carry(x, s, c_scr[...])
        else:
            xs = _shift_rows_fix(x, s, halo_refs[s - 1][...], seq_len)
        acc = acc + xs * w[GDN_CONV - 1 - s:GDN_CONV - s]
    if carry_mode:
        c_scr[...] = x[rows - SUBLANES:]
    act = _silu(acc)

    ones = jnp.ones((hd, hd), BF16)
    lane_sum = lambda a: _mm(a, ones)
    l2n = lambda a: a * lax.rsqrt(lane_sum(a * a) + L2_EPS)
    q_h = [l2n(act[:, h * hd:(h + 1) * hd]) * (hd ** -0.5) for h in range(nh)]
    k_h = [l2n(act[:, (nh + h) * hd:(nh + h + 1) * hd]) for h in range(nh)]
    v_h = [act[:, (2 * nh + h) * hd:(2 * nh + h + 1) * hd] for h in range(nh)]

    ba = ba_ref[...]
    gp = gp_ref[...]
    beta_all = _sigmoid(ba)
    lg_all = -jnp.exp(gp[0:1]) * _softplus(ba + gp[1:2])
    lane = _iota((c, LANES), 1)
    pick = lambda arr, col: jnp.sum(jnp.where(lane == col, arr, 0.0), axis=-1, keepdims=True)

    ti = _iota((c, c), 0)
    tj = _iota((c, c), 1)
    same = (ti ^ tj) < seq_len
    incl = same & (tj <= ti)
    strict = same & (tj < ti)
    incl_bf = incl.astype(BF16)
    same_bf = same.astype(BF16)
    incl_t_bf = (same & (ti <= tj)).astype(BF16)

    probs = []
    for ci in range(n_chunks):
        r = slice(ci * c, (ci + 1) * c)
        lgc = lg_all[r]
        gb_all = _mm_mask_lhs(incl_bf, lgc)
        gtot_all = _mm_mask_lhs(same_bf, lgc)
        gb_rows = _mm_mask_rhs(lgc, incl_t_bf, TN)
        for h in range(nh):
            probs.append(dict(ci=ci, h=h, q=q_h[h][r], k=k_h[h][r], v=v_h[h][r], beta=pick(beta_all[r], h),
                              gb=pick(gb_all, nh + h), gtot=pick(gtot_all, nh + h),
                              gb_row=gb_rows[nh + h:nh + h + 1, :]))

    decays = [jnp.exp(jnp.where(incl, p["gb"] - p["gb_row"], -jnp.inf)) for p in probs]
    kbetas = [p["k"] * p["beta"] for p in probs]
    kqs = [_mm(jnp.concatenate([kb, p["q"]], axis=0), p["k"], NT) for kb, p in zip(kbetas, probs)]
    neg_ls = [jnp.where(strict, -(kq[:c] * d), 0.0) for kq, d in zip(kqs, decays)]
    attns = [kq[c:] * d for kq, d in zip(kqs, decays)]
    e_gs = [jnp.exp(p["gb"]) for p in probs]
    if seq_len == c:
        t_invs = _unit_lower_inverses_halved(neg_ls)
    else:
        t_invs = _unit_lower_inverses(neg_ls, ti, tj, seq_len, _mm_acc)
    sols = [_mm_acc(t, jnp.concatenate([p["v"] * p["beta"], kb * eg], axis=1))
            for t, p, kb, eg in zip(t_invs, probs, kbetas, e_gs)]
    atts = [_mm(a, sol) for a, sol in zip(attns, sols)]
    q_effs = [p["q"] * eg - att[:, hd:] for p, eg, att in zip(probs, e_gs, atts)]
    k_decs = [p["k"] * jnp.exp(p["gtot"] - p["gb"]) for p in probs]
    g_lasts = [jnp.exp(p["gtot"]) for p in probs]
    nms = [[_mm(kd[n * seq_len:(n + 1) * seq_len], sol[n * seq_len:(n + 1) * seq_len], TN) for n in range(n_seg)]
           for kd, sol in zip(k_decs, sols)]

    states = [s_scr[h] for h in range(nh)] if carry_mode else None
    o_rows = [[] for _ in range(nh)]
    for idx, p in enumerate(probs):
        h, ci = p["h"], p["ci"]
        for n in range(n_seg):
            rs = slice(n * seq_len, (n + 1) * seq_len)
            seq = ci * n_seg + n
            s = states[h] if carry_mode else s_in_ref[seq, h]
            o_rows[h].append(_mm(q_effs[idx][rs], s) + atts[idx][rs, :hd])
            nm = nms[idx][n]
            s_new = s * g_lasts[idx][n * seq_len:n * seq_len + 1] - _mm(nm[:, hd:], s) + nm[:, :hd]
            if carry_mode:
                states[h] = s_new
            else:
                s_out_ref[seq, h] = s_new

    nw = nw_ref[...]
    for h in range(nh):
        o = _cat(o_rows[h], 0)
        o = o * lax.rsqrt(lane_sum(o * o) * (1.0 / hd) + NORM_EPS) * nw
        o_ref[:, h * hd:(h + 1) * hd] = o * _silu(z_ref[:, h * hd:(h + 1) * hd])

    if carry_mode:
        for h in range(nh):
            s_scr[h] = states[h]

        @pl.when(t_idx == pl.num_programs(1) - 1)
        def _():
            for h in range(nh):
                s_out_ref[h] = states[h]


def _gdn_mixer(qkv, z, ba, conv_w, gparams, norm_w, halo, s_in, *, rows, chunk, seq_len, carry_mode):
    bsz, t_len, _ = qkv.shape
    grid = (bsz, t_len // rows)
    tok = lambda width: pl.BlockSpec((None, rows, width), lambda b, t: (b, t, 0))
    const = lambda shape: pl.BlockSpec(shape, lambda b, t: (0,) * len(shape))
    in_specs = [tok(GDN_QKV), tok(GDN_WIDTH), tok(BA_PAD), const((GDN_CONV, GDN_QKV)), const((2, BA_PAD)),
                const((1, GDN_HEAD_DIM))]
    args = [qkv, z, ba, conv_w, gparams, norm_w]
    state_tail = (GDN_HEADS, GDN_HEAD_DIM, GDN_HEAD_DIM)
    if carry_mode:
        in_specs.append(pl.BlockSpec((None, SUBLANES, GDN_QKV), lambda b, t: (b, 0, 0)))
        args.append(halo)
        state_spec = pl.BlockSpec((None,) + state_tail, lambda b, t: (b, 0, 0, 0))
        scratch = [pltpu.VMEM(state_tail, F32), pltpu.VMEM((SUBLANES, GDN_QKV), F32)]
    else:
        in_specs += [tok(GDN_QKV)] * (GDN_CONV - 1)
        args += list(halo)
        state_spec = pl.BlockSpec((rows // seq_len,) + state_tail, lambda b, t: (t, 0, 0, 0))
        scratch = []
    in_specs.append(state_spec)
    args.append(s_in)
    return pl.pallas_call(
        functools.partial(_gdn_kernel, rows=rows, chunk=chunk, seq_len=seq_len, carry_mode=carry_mode),
        grid=grid,
        in_specs=in_specs,
        out_specs=[tok(GDN_WIDTH), state_spec],
        out_shape=[jax.ShapeDtypeStruct((bsz, t_len, GDN_WIDTH), F32),
                   jax.ShapeDtypeStruct(s_in.shape, F32)],
        scratch_shapes=scratch,
        compiler_params=pltpu.CompilerParams(dimension_semantics=("arbitrary",) * 2,
                                             vmem_limit_bytes=VMEM_LIMIT_BYTES),
        name="gdn_mixer",
    )(*args)


def _rwkv_kernel(*refs, rows, chunk, seq_len, carry_mode):
    (rkv_ref, lora_ref, mur_ref, mul_ref, par_ref, w2_ref, a2_ref, g2_ref, hr_ref, hl_ref, s_in_ref,
     o_ref, s_out_ref) = refs[:13]
    scratch = refs[13:]
    hd, npair, c, wd_all = RWKV_HEAD_DIM, RWKV_PAIRS, chunk, RWKV_WIDTH
    pw = 2 * hd
    n_chunks = rows // chunk
    n_seg = chunk // seq_len
    t_idx = pl.program_id(1)

    if carry_mode:
        s_scr, cr_scr, cl_scr = scratch

        @pl.when(t_idx == 0)
        def _():
            s_scr[...] = s_in_ref[...]
            cr_scr[...] = hr_ref[...]
            cl_scr[...] = hl_ref[...]

    def token_shift(x_ref, mu_ref, h_ref, c_scr):
        x = x_ref[...]
        if carry_mode:
            prev = _shift_rows_carry(x, 1, c_scr[...])
            c_scr[...] = x[rows - SUBLANES:]
        else:
            prev = _shift_rows_fix(x, 1, h_ref[...], seq_len)
        return x + mu_ref[...] * (prev - x)

    xs = token_shift(rkv_ref, mur_ref, hr_ref, cr_scr if carry_mode else None)
    lora = token_shift(lora_ref, mul_ref, hl_ref, cl_scr if carry_mode else None)
    r, kr, vr = xs[:, :wd_all], xs[:, wd_all:2 * wd_all], xs[:, 2 * wd_all:]
    par = par_ref[...]
    w0, a0, k_k, k_a, r_k, ln_w, ln_b = (par[n:n + 1] for n in range(7))
    wd = lora[:, :DECAY_LORA]
    ad = lora[:, DECAY_LORA:DECAY_LORA + AAA_LORA]
    gd = lora[:, DECAY_LORA + AAA_LORA:]
    lw = -(DECAY_SCALE * _sigmoid(w0 + _mm(jnp.tanh(wd), w2_ref[...])))
    asig = _sigmoid(a0 + _mm(ad, a2_ref[...]))
    g = _mm(_sigmoid(gd), g2_ref[...])

    li = _iota((pw, pw), 0)
    lj = _iota((pw, pw), 1)
    pair_bd = (li >= hd) == (lj >= hd)
    head_ones = pair_bd.astype(BF16)

    def head_sum(a):
        return _cat([_mm(a[:, p * pw:(p + 1) * pw], head_ones) for p in range(a.shape[1] // pw)], 1)

    kk = kr * k_k
    kk = kk * lax.rsqrt(head_sum(kk * kk) + L2_EPS)
    kr = kr * (1.0 + (asig - 1.0) * k_a)
    a_vec = -kk
    b_vec = kk * asig

    i1 = _iota((c, c), 0)
    j1 = _iota((c, c), 1)
    same1 = (i1 ^ j1) < seq_len
    incl1_bf = (same1 & (j1 <= i1)).astype(BF16)
    same1_bf = same1.astype(BF16)
    ip = _iota((c, 2 * c), 0)
    jp_full = _iota((c, 2 * c), 1)
    jp = jp_full & (c - 1)
    head1_cols = jp_full >= c
    same_p = (ip ^ jp) < seq_len
    incl_p = same_p & (jp <= ip)
    strict_p = same_p & (jp < ip)

    def stack_heads(a):
        head1 = (_iota(a.shape, 1) & hd) != 0
        return jnp.concatenate([jnp.where(head1, 0.0, a), jnp.where(head1, a, 0.0)], axis=0)

    def block_diag(m):
        return jnp.concatenate([jnp.where(head1_cols, 0.0, m), jnp.where(head1_cols, m, 0.0)], axis=0)

    pair_mul = lambda a, b: _mm_acc(a, block_diag(b))

    probs = []
    for ci in range(n_chunks):
        rs = slice(ci * c, (ci + 1) * c)
        lwc = lw[rs]
        g_incl = _mm_mask_lhs(incl1_bf, lwc)
        g_mid = 0.5 * _mm_mask_lhs(same1_bf, lwc)
        e_mid = jnp.exp(g_mid)
        inv_n = jnp.exp(g_mid - g_incl)
        r_n = r[rs] * jnp.exp(g_incl - g_mid)
        a_n = a_vec[rs] * jnp.exp(g_incl - lwc - g_mid)
        to_end = inv_n * e_mid
        full = dict(r_n=r_n, a_n=a_n, b_n=b_vec[rs] * inv_n, k_n=kr[rs] * inv_n, r_t=r_n * e_mid,
                    a_t=a_n * e_mid, b_end=b_vec[rs] * to_end, k_end=kr[rs] * to_end,
                    gam=e_mid * e_mid, v=vr[rs])
        for p in range(npair):
            prob = {name: val[:, p * pw:(p + 1) * pw] for name, val in full.items()}
            prob.update(ci=ci, p=p)
            probs.append(prob)

    ars = [jnp.concatenate([p["a_n"], p["r_n"]], axis=0) for p in probs]
    m_bs = [_mm(ar, stack_heads(p["b_n"]), NT) for ar, p in zip(ars, probs)]
    m_ks = [_mm(ar, stack_heads(p["k_n"]), NT) for ar, p in zip(ars, probs)]
    a_abs = [jnp.where(strict_p, m[:c], 0.0) for m in m_bs]
    a_rbs = [jnp.where(incl_p, m[c:], 0.0) for m in m_bs]
    a_aks = [jnp.where(strict_p, m[:c], 0.0) for m in m_ks]
    a_rks = [jnp.where(incl_p, m[c:], 0.0) for m in m_ks]
    t_invs = _unit_lower_inverses(a_abs, ip, jp, seq_len, pair_mul)
    zvs = [_mm(a_ak, stack_heads(p["v"])) for a_ak, p in zip(a_aks, probs)]
    uws = [_mm_acc(t, stack_heads(jnp.concatenate([zv, p["a_t"]], axis=1)))
           for t, zv, p in zip(t_invs, zvs, probs)]
    rbs = [_mm(a_rb, stack_heads(uw)) for a_rb, uw in zip(a_rbs, uws)]
    y_frees = [rb[:, :pw] + _mm(a_rk, stack_heads(p["v"])) for rb, a_rk, p in zip(rbs, a_rks, probs)]
    r_effs = [p["r_t"] + rb[:, pw:] for p, rb in zip(probs, rbs)]
    m_corrs, n_adds = [], []
    for p, uw in zip(probs, uws):
        ms, ns = [], []
        for n in range(n_seg):
            sr = slice(n * seq_len, (n + 1) * seq_len)
            ms.append(jnp.where(pair_bd, _mm_acc(uw[sr, pw:], p["b_end"][sr], TN), 0.0))
            lhs = jnp.concatenate([uw[sr, :pw], p["v"][sr]], axis=0)
            rhs = jnp.concatenate([p["b_end"][sr], p["k_end"][sr]], axis=0)
            ns.append(jnp.where(pair_bd, _mm_acc(lhs, rhs, TN), 0.0))
        m_corrs.append(ms)
        n_adds.append(ns)

    states = [s_scr[p] for p in range(npair)] if carry_mode else None
    y_rows = [[] for _ in range(npair)]
    for idx, prob in enumerate(probs):
        p, ci = prob["p"], prob["ci"]
        for n in range(n_seg):
            sr = slice(n * seq_len, (n + 1) * seq_len)
            seq = ci * n_seg + n
            s = states[p] if carry_mode else s_in_ref[seq, p]
            y_rows[p].append(_mm_acc(r_effs[idx][sr], s, NT) + y_frees[idx][sr])
            gam = prob["gam"][n * seq_len:n * seq_len + 1]
            s_new = s * gam + _mm_acc(s, m_corrs[idx][n]) + n_adds[idx][n]
            if carry_mode:
                states[p] = s_new
            else:
                s_out_ref[seq, p] = s_new

    y = _cat([_cat(y_rows[p], 0) for p in range(npair)], 1)
    y_c = y - head_sum(y) * (1.0 / hd)
    y_n = y_c * lax.rsqrt(head_sum(y_c * y_c) * (1.0 / hd) + RWKV_LN_EPS)
    y_n = y_n * ln_w + ln_b
    bonus = head_sum(r * kr * r_k) * vr
    o_ref[...] = (y_n + bonus) * g

    if carry_mode:
        for p in range(npair):
            s_scr[p] = states[p]

        @pl.when(t_idx == pl.num_programs(1) - 1)
        def _():
            for p in range(npair):
                s_out_ref[p] = states[p]


def _rwkv_mixer(rkv, lora, mu_rkv, mu_lora, params, w2, a2, g2, halo_rkv, halo_lora, s_in, *,
                rows, chunk, seq_len, carry_mode):
    bsz, t_len, _ = rkv.shape
    grid = (bsz, t_len // rows)
    tok = lambda width: pl.BlockSpec((None, rows, width), lambda b, t: (b, t, 0))
    const = lambda shape: pl.BlockSpec(shape, lambda b, t: (0,) * len(shape))
    in_specs = [tok(3 * RWKV_WIDTH), tok(LORA_PAD), const((1, 3 * RWKV_WIDTH)), const((1, LORA_PAD)),
                const((SUBLANES, RWKV_WIDTH)), const((DECAY_LORA, RWKV_WIDTH)), const((AAA_LORA, RWKV_WIDTH)),
                const((LORA_PAD - DECAY_LORA - AAA_LORA, RWKV_WIDTH))]
    state_tail = (RWKV_PAIRS, LANES, LANES)
    if carry_mode:
        in_specs += [pl.BlockSpec((None, SUBLANES, 3 * RWKV_WIDTH), lambda b, t: (b, 0, 0)),
                     pl.BlockSpec((None, SUBLANES, LORA_PAD), lambda b, t: (b, 0, 0))]
        state_spec = pl.BlockSpec((None,) + state_tail, lambda b, t: (b, 0, 0, 0))
        scratch = [pltpu.VMEM(state_tail, F32), pltpu.VMEM((SUBLANES, 3 * RWKV_WIDTH), F32),
                   pltpu.VMEM((SUBLANES, LORA_PAD), F32)]
    else:
        in_specs += [tok(3 * RWKV_WIDTH), tok(LORA_PAD)]
        state_spec = pl.BlockSpec((rows // seq_len,) + state_tail, lambda b, t: (t, 0, 0, 0))
        scratch = []
    in_specs.append(state_spec)
    args = [rkv, lora, mu_rkv, mu_lora, params, w2, a2, g2, halo_rkv, halo_lora, s_in]
    return pl.pallas_call(
        functools.partial(_rwkv_kernel, rows=rows, chunk=chunk, seq_len=seq_len, carry_mode=carry_mode),
        grid=grid,
        in_specs=in_specs,
        out_specs=[tok(RWKV_WIDTH), state_spec],
        out_shape=[jax.ShapeDtypeStruct((bsz, t_len, RWKV_WIDTH), F32),
                   jax.ShapeDtypeStruct(s_in.shape, F32)],
        scratch_shapes=scratch,
        compiler_params=pltpu.CompilerParams(dimension_semantics=("arbitrary",) * 2,
                                             vmem_limit_bytes=VMEM_LIMIT_BYTES),
        name="rwkv_mixer",
    )(*args)


def _ffn_kernel(*refs, rows, seq_len, blocks_per_seq, carry_mode):
    x_ref, og_ref, orw_ref, wo_ref, nf_ref, wu_ref, cw_ref, wd_ref, nfin_ref = refs[:9]
    if carry_mode:
        h_ref, y_ref, tail_ref, carry_scr = refs[9:]
    else:
        f1_ref, f2_ref, y_ref, tail_ref = refs[9:]
    blk = pl.program_id(0)

    if carry_mode:
        @pl.when(blk % blocks_per_seq == 0)
        def _():
            carry_scr[...] = h_ref[...]

    o = (jnp.dot(og_ref[...].astype(BF16), wo_ref[:GDN_WIDTH], preferred_element_type=F32)
         + jnp.dot(orw_ref[...].astype(BF16), wo_ref[GDN_WIDTH:], preferred_element_type=F32))
    x1 = x_ref[...] + o
    h2 = _rms_norm(x1, nf_ref[...]).astype(BF16)
    gate = jnp.dot(h2, wu_ref[:, :FFN_DIM], preferred_element_type=F32)
    val = jnp.dot(h2, wu_ref[:, FFN_DIM:], preferred_element_type=F32)
    cw = cw_ref[...]
    conv = gate * cw[FFN_CONV - 1:FFN_CONV]
    for s in range(1, FFN_CONV):
        if carry_mode:
            gs = _shift_rows_carry(gate, s, carry_scr[...])
        else:
            gs = _shift_rows_fix(gate, s, (f1_ref, f2_ref)[s - 1][...], seq_len)
        conv = conv + gs * cw[FFN_CONV - 1 - s:FFN_CONV - s]
    if carry_mode:
        carry_scr[...] = gate[rows - SUBLANES:]
        tail_ref[...] = gate[rows - SUBLANES:]
    else:
        tail_ref[...] = gate
    act = (_silu(conv) * val).astype(BF16)
    x2 = x1 + jnp.dot(act, wd_ref[...], preferred_element_type=F32)
    y_ref[...] = _rms_norm(x2, nfin_ref[...])


def _ffn(x2d, o_gdn, o_rwkv, w_out, norm_ffn, w_up, conv_w, w_down, norm_final, halo, *,
         rows, seq_len, carry_mode):
    n_rows = x2d.shape[0]
    n_blocks = n_rows // rows
    row_spec = lambda w: pl.BlockSpec((rows, w), lambda i: (i, 0))
    whole = lambda shape: pl.BlockSpec(shape, lambda i: (0,) * len(shape), pipeline_mode=pl.Buffered(1))
    in_specs = [row_spec(D_MODEL), row_spec(GDN_WIDTH), row_spec(RWKV_WIDTH),
                whole((D_MODEL, D_MODEL)), whole((1, D_MODEL)), whole((D_MODEL, 2 * FFN_DIM)),
                whole((FFN_CONV, FFN_DIM)), whole((FFN_DIM, D_MODEL)), whole((1, D_MODEL))]
    args = [x2d, o_gdn, o_rwkv, w_out, norm_ffn, w_up, conv_w, w_down, norm_final]
    if carry_mode:
        blocks_per_seq = seq_len // rows
        in_specs.append(pl.BlockSpec((None, SUBLANES, FFN_DIM), lambda i: (i // blocks_per_seq, 0, 0)))
        args.append(halo)
        tail_spec = pl.BlockSpec((None, SUBLANES, FFN_DIM), lambda i: (i, 0, 0))
        tail_shape = jax.ShapeDtypeStruct((n_blocks, SUBLANES, FFN_DIM), F32)
        scratch = [pltpu.VMEM((SUBLANES, FFN_DIM), F32)]
    else:
        blocks_per_seq = 1
        in_specs += [row_spec(FFN_DIM), row_spec(FFN_DIM)]
        args += [halo[0], halo[1]]
        tail_spec = row_spec(FFN_DIM)
        tail_shape = jax.ShapeDtypeStruct((n_rows, FFN_DIM), F32)
        scratch = []
    return pl.pallas_call(
        functools.partial(_ffn_kernel, rows=rows, seq_len=seq_len, blocks_per_seq=blocks_per_seq,
                          carry_mode=carry_mode),
        grid=(n_blocks,),
        in_specs=in_specs,
        out_specs=[row_spec(D_MODEL), tail_spec],
        out_shape=[jax.ShapeDtypeStruct((n_rows, D_MODEL), F32), tail_shape],
        scratch_shapes=scratch,
        compiler_params=pltpu.CompilerParams(dimension_semantics=("arbitrary",),
                                             vmem_limit_bytes=VMEM_LIMIT_BYTES),
        name="ffn",
    )(*args)


def _pad_cols(a, width):
    return jnp.pad(a, ((0, 0), (0, width - a.shape[1])))


def _regroup_in_weights(w_in):
    g_end = GDN_QKV + GDN_WIDTH
    ba = w_in[:, g_end:g_end + 2 * GDN_HEADS]
    rw = w_in[:, g_end + 2 * GDN_HEADS:]
    return jnp.concatenate([w_in[:, :g_end], rw[:, :3 * RWKV_WIDTH],
                            _pad_cols(rw[:, 3 * RWKV_WIDTH:], LORA_PAD), _pad_cols(ba, BA_PAD)],
                           axis=1).astype(BF16)


def _tail_tile(state_rows):
    return jnp.pad(state_rows, ((0, 0), (SUBLANES - state_rows.shape[1], 0), (0, 0)))


def _fix_rows(state_rows, s, seq_len):
    n = state_rows.shape[1]
    first = state_rows[:, n - s:, :]
    return jnp.pad(first, ((0, 0), (0, seq_len - s), (0, 0))).reshape(-1, state_rows.shape[2])


def _pair_states(s):
    bsz = s.shape[0]
    sp = s.reshape(bsz, RWKV_PAIRS, 2, RWKV_HEAD_DIM, RWKV_HEAD_DIM)
    zero = jnp.zeros_like(sp[:, :, 0])
    top = jnp.concatenate([sp[:, :, 0], zero], axis=-1)
    bot = jnp.concatenate([zero, sp[:, :, 1]], axis=-1)
    return jnp.concatenate([top, bot], axis=-2)


def _unpair_states(sp):
    hd = RWKV_HEAD_DIM
    bsz = sp.shape[0]
    return jnp.stack([sp[:, :, :hd, :hd], sp[:, :, hd:, hd:]], axis=2).reshape(bsz, RWKV_HEADS, hd, hd)


def _layer(x, st_gdn_conv, st_gdn, st_shift, st_rwkv, st_ffn, wts, *, carry_mode):
    bsz, t_len, _ = x.shape
    n_rows = bsz * t_len
    x2d = x.reshape(n_rows, D_MODEL)
    qkv, z, rkv, lora, ba = _inproj(x2d, wts["norm_mix"], wts["w_in"])

    shift_rkv = st_shift[:, :, :3 * RWKV_WIDTH]
    shift_lora = _pad_cols(st_shift[:, 0, 3 * RWKV_WIDTH:], LORA_PAD)[:, None, :]
    if carry_mode:
        shape3 = lambda a: a.reshape(bsz, t_len, a.shape[-1])
        o_gdn, s_gdn = _gdn_mixer(shape3(qkv), shape3(z), shape3(ba), wts["gdn_conv_w"], wts["gparams"],
                                  wts["gdn_norm"], _tail_tile(st_gdn_conv), st_gdn,
                                  rows=MIXER_ROWS, chunk=GDN_CHUNK, seq_len=GDN_CHUNK, carry_mode=True)
        o_rwkv, s_rwkv = _rwkv_mixer(shape3(rkv), shape3(lora), wts["mu_rkv"], wts["mu_lora"], wts["rparams"],
                                     wts["w2"], wts["a2"], wts["g2"], _tail_tile(shift_rkv),
                                     _tail_tile(shift_lora), _pair_states(st_rwkv),
                                     rows=MIXER_ROWS, chunk=RWKV_CHUNK, seq_len=RWKV_CHUNK, carry_mode=True)
        ffn_halo = _tail_tile(st_ffn)
        ffn_rows, ffn_seq = ROW_BLOCK, t_len
    else:
        shape3 = lambda a: a.reshape(1, n_rows, a.shape[-1])
        fix = lambda st, s: _fix_rows(st, s, t_len)[None]
        o_gdn, s_gdn = _gdn_mixer(shape3(qkv), shape3(z), shape3(ba), wts["gdn_conv_w"], wts["gparams"],
                                  wts["gdn_norm"], [fix(st_gdn_conv, s) for s in range(1, GDN_CONV)], st_gdn,
                                  rows=n_rows, chunk=GDN_CHUNK, seq_len=t_len, carry_mode=False)
        o_rwkv, s_rwkv = _rwkv_mixer(shape3(rkv), shape3(lora), wts["mu_rkv"], wts["mu_lora"], wts["rparams"],
                                     wts["w2"], wts["a2"], wts["g2"], fix(shift_rkv, 1), fix(shift_lora, 1),
                                     _pair_states(st_rwkv),
                                     rows=n_rows, chunk=RWKV_CHUNK, seq_len=t_len, carry_mode=False)
        ffn_halo = [_fix_rows(st_ffn, s, t_len) for s in range(1, FFN_CONV)]
        ffn_rows, ffn_seq = n_rows, t_len
    y2d, gate_tail = _ffn(x2d, o_gdn.reshape(n_rows, GDN_WIDTH), o_rwkv.reshape(n_rows, RWKV_WIDTH),
                          wts["w_out"], wts["norm_ffn"], wts["w_up"], wts["ffn_conv_w"], wts["w_down"],
                          wts["norm_final"], ffn_halo, rows=ffn_rows, seq_len=ffn_seq, carry_mode=carry_mode)

    qkv3 = qkv.reshape(bsz, t_len, GDN_QKV)
    gdn_conv_new = qkv3[:, t_len - (GDN_CONV - 1):, :]
    shift_new = jnp.concatenate([rkv.reshape(bsz, t_len, -1)[:, t_len - 1:, :],
                                 lora.reshape(bsz, t_len, -1)[:, t_len - 1:, :RWKV_PROJ - 3 * RWKV_WIDTH]], axis=-1)
    if carry_mode:
        blocks_per_seq = t_len // ROW_BLOCK
        tails = gate_tail.reshape(bsz, blocks_per_seq, SUBLANES, FFN_DIM)[:, -1]
        ffn_new = tails[:, SUBLANES - (FFN_CONV - 1):, :]
    else:
        ffn_new = gate_tail.reshape(bsz, t_len, FFN_DIM)[:, t_len - (FFN_CONV - 1):, :]
    return (y2d.reshape(bsz, t_len, D_MODEL), gdn_conv_new, s_gdn, shift_new, _unpair_states(s_rwkv), ffn_new)


def kernel(x_prompt, x_sample, state_gdn_conv, state_gdn, state_rwkv_shift, state_rwkv, state_ffn_conv,
           norm_mix, w_in, gdn_conv_w, gdn_a_log, gdn_dt_bias, gdn_norm,
           rwkv_mu, rwkv_w0, rwkv_w2, rwkv_a0, rwkv_a2, rwkv_g2, rwkv_k_k, rwkv_k_a, rwkv_r_k,
           rwkv_ln_w, rwkv_ln_b, w_out, norm_ffn, w_up, ffn_conv_w, w_down, norm_final):
    layer = 0
    gparams = jnp.zeros((2, BA_PAD), F32)
    gparams = gparams.at[0, GDN_HEADS:2 * GDN_HEADS].set(gdn_a_log[layer])
    gparams = gparams.at[1, GDN_HEADS:2 * GDN_HEADS].set(gdn_dt_bias[layer])
    mu = rwkv_mu[layer][None, :]
    rparams = jnp.stack([rwkv_w0[layer], rwkv_a0[layer], rwkv_k_k[layer], rwkv_k_a[layer],
                         rwkv_r_k[layer].reshape(-1), rwkv_ln_w[layer], rwkv_ln_b[layer],
                         jnp.zeros((RWKV_WIDTH,), F32)])
    gate_rows = LORA_PAD - DECAY_LORA - AAA_LORA
    wts = {
        "norm_mix": norm_mix[layer][None, :],
        "w_in": _regroup_in_weights(w_in[layer]),
        "gdn_conv_w": gdn_conv_w[layer],
        "gparams": gparams,
        "gdn_norm": gdn_norm[layer][None, :],
        "mu_rkv": mu[:, :3 * RWKV_WIDTH],
        "mu_lora": _pad_cols(mu[:, 3 * RWKV_WIDTH:], LORA_PAD),
        "rparams": rparams,
        "w2": rwkv_w2[layer].astype(BF16),
        "a2": rwkv_a2[layer].astype(BF16),
        "g2": jnp.pad(rwkv_g2[layer], ((0, gate_rows - GATE_LORA), (0, 0))).astype(BF16),
        "w_out": w_out[layer].astype(BF16),
        "norm_ffn": norm_ffn[layer][None, :],
        "w_up": w_up[layer].astype(BF16),
        "ffn_conv_w": ffn_conv_w[layer],
        "w_down": w_down[layer].astype(BF16),
        "norm_final": norm_final[None, :],
    }
    n_prompt = x_prompt.shape[0]
    zeros = lambda c: jnp.zeros((n_prompt,) + c.shape[2:], c.dtype)
    outs_p = _layer(x_prompt, zeros(state_gdn_conv), zeros(state_gdn), zeros(state_rwkv_shift),
                    zeros(state_rwkv), zeros(state_ffn_conv), wts, carry_mode=True)
    outs_s = _layer(x_sample, state_gdn_conv[layer], state_gdn[layer], state_rwkv_shift[layer],
                    state_rwkv[layer], state_ffn_conv[layer], wts, carry_mode=False)
    y_p, y_s = outs_p[0], outs_s[0]
    states = []
    for n in range(1, 6):
        states += [outs_p[n][None], outs_s[n][None]]
    return (y_p, y_s, *states)
```

```python
import functools
import math

import jax
import jax.numpy as jnp
from jax import lax
from jax.experimental import pallas as pl
from jax.experimental.pallas import tpu as pltpu

F32 = jnp.float32
BF16 = jnp.bfloat16

D_MODEL = 1024
GDN_HEAD_DIM = 128
GDN_HEADS = 4
GDN_WIDTH = GDN_HEADS * GDN_HEAD_DIM
GDN_QKV = 3 * GDN_WIDTH
GDN_CONV = 4
RWKV_HEAD_DIM = 64
RWKV_HEADS = 8
RWKV_WIDTH = RWKV_HEADS * RWKV_HEAD_DIM
RWKV_PAIRS = RWKV_HEADS // 2
DECAY_LORA = 64
AAA_LORA = 64
GATE_LORA = 160
RWKV_PROJ = 3 * RWKV_WIDTH + DECAY_LORA + AAA_LORA + GATE_LORA
FFN_DIM = 2816
FFN_CONV = 3
NORM_EPS = 1e-6
L2_EPS = 1e-6
RWKV_LN_EPS = 64e-5
DECAY_SCALE = math.exp(-0.5)

LANES = 128
SUBLANES = 8
VMEM_LIMIT_BYTES = 56 * 1024 * 1024

LORA_PAD = 384
BA_PAD = LANES
P_PAD = GDN_QKV + GDN_WIDTH + 3 * RWKV_WIDTH + LORA_PAD + BA_PAD
OFF_QKV = 0
OFF_Z = OFF_QKV + GDN_QKV
OFF_RKV = OFF_Z + GDN_WIDTH
OFF_LORA = OFF_RKV + 3 * RWKV_WIDTH
OFF_BA = OFF_LORA + LORA_PAD

INPROJ_ROWS = 512
ROW_BLOCK = 256
MIXER_ROWS = 512
GDN_CHUNK = 128
RWKV_CHUNK = 64

NN = (((1,), (0,)), ((), ()))
NT = (((1,), (1,)), ((), ()))
TN = (((0,), (0,)), ((), ()))


def _dot(a, b, dims):
    return lax.dot_general(a, b, dims, preferred_element_type=F32)


def _mm(a, b, dims=NN):
    return _dot(a.astype(BF16), b.astype(BF16), dims)


def _split2(a):
    hi = a.astype(BF16)
    return hi, (a - hi.astype(F32)).astype(BF16)


def _mm_mask_lhs(mask_bf16, x, dims=NN):
    hi, lo = _split2(x)
    return _dot(mask_bf16, hi, dims) + _dot(mask_bf16, lo, dims)


def _mm_mask_rhs(x, mask_bf16, dims=NN):
    hi, lo = _split2(x)
    return _dot(hi, mask_bf16, dims) + _dot(lo, mask_bf16, dims)


_mm_acc = _mm


def _iota(shape, dim):
    return lax.broadcasted_iota(jnp.int32, shape, dim)


def _sigmoid(x):
    return 0.5 + 0.5 * jnp.tanh(0.5 * x)


def _silu(x):
    h = 0.5 * x
    return h + h * jnp.tanh(h)


def _softplus(x):
    return jnp.maximum(x, 0.0) + jnp.log1p(jnp.exp(-jnp.abs(x)))


def _rms_norm(x, w):
    return x * lax.rsqrt(jnp.mean(x * x, axis=-1, keepdims=True) + NORM_EPS) * w


def _cat(parts, axis):
    return parts[0] if len(parts) == 1 else jnp.concatenate(parts, axis=axis)


def _shift_rows_carry(x, s, carry):
    xr = pltpu.roll(x, s, 0)
    cr = pltpu.roll(carry, s, 0)
    row = _iota((SUBLANES, x.shape[1]), 0)
    head = jnp.where(row < s, cr, xr[:SUBLANES])
    return jnp.concatenate([head, xr[SUBLANES:]], axis=0)


def _shift_rows_fix(x, s, fix, seq_len):
    xr = pltpu.roll(x, s, 0)
    row = _iota(x.shape, 0)
    return jnp.where((row & (seq_len - 1)) < s, fix, xr)


def _unit_lower_inverses(ps, ti, tj, seq_len, mul):
    def joins(s):
        return ((ti ^ tj) < 2 * s) & ((ti & s) != 0) & ((tj & s) == 0)

    eye = jnp.where(ti == tj, 1.0, 0.0)
    ts = [eye + jnp.where(joins(1), p, 0.0) for p in ps]
    s = 2
    while s < seq_len:
        mask = joins(s)
        tbs = [mul(t, jnp.where(mask, p, 0.0)) for t, p in zip(ts, ps)]
        ts = [t + mul(tb, t) for tb, t in zip(tbs, ts)]
        s *= 2
    return ts


def _unit_lower_inverses_halved(ps):
    c = ps[0].shape[0]
    half = c // 2
    ti = _iota((half, c), 0)
    lane = _iota((half, c), 1)
    right = lane >= half
    zeros = jnp.zeros((half, c), F32)

    def block_diag(m):
        return jnp.concatenate([jnp.where(right, 0.0, m), jnp.where(right, m, 0.0)], axis=0)

    diag = [jnp.where(right, p[half:], p[:half]) for p in ps]
    t_ab = _unit_lower_inverses(diag, ti, lane & (half - 1), half, lambda a, b: _mm_acc(a, block_diag(b)))
    ys = [_mm_acc(jnp.where(right, 0.0, p[half:]), jnp.concatenate([jnp.where(right, 0.0, t), zeros], axis=0))
          for p, t in zip(ps, t_ab)]
    x21s = [_mm_acc(t, jnp.concatenate([zeros, y], axis=0)) for t, y in zip(t_ab, ys)]
    return [jnp.concatenate([jnp.where(right, 0.0, t), x21 + jnp.where(right, t, 0.0)], axis=0)
            for t, x21 in zip(t_ab, x21s)]


def _inproj_kernel(x_ref, nw_ref, w_ref, qkv_ref, z_ref, rkv_ref, lora_ref, ba_ref):
    h = _rms_norm(x_ref[...], nw_ref[...]).astype(BF16)
    for out_ref, off in ((qkv_ref, OFF_QKV), (z_ref, OFF_Z), (rkv_ref, OFF_RKV), (lora_ref, OFF_LORA),
                         (ba_ref, OFF_BA)):
        width = out_ref.shape[1]
        out_ref[...] = jnp.dot(h, w_ref[:, off:off + width], preferred_element_type=F32)


def _inproj(x2d, norm_w, w_perm):
    n_rows = x2d.shape[0]
    rows = min(INPROJ_ROWS, n_rows)
    widths = (GDN_QKV, GDN_WIDTH, 3 * RWKV_WIDTH, LORA_PAD, BA_PAD)
    return pl.pallas_call(
        _inproj_kernel,
        grid=(n_rows // rows,),
        in_specs=[
            pl.BlockSpec((rows, D_MODEL), lambda i: (i, 0)),
            pl.BlockSpec((1, D_MODEL), lambda i: (0, 0)),
            pl.BlockSpec((D_MODEL, P_PAD), lambda i: (0, 0), pipeline_mode=pl.Buffered(1)),
        ],
        out_specs=[pl.BlockSpec((rows, w), lambda i: (i, 0)) for w in widths],
        out_shape=[jax.ShapeDtypeStruct((n_rows, w), F32) for w in widths],
        compiler_params=pltpu.CompilerParams(dimension_semantics=("arbitrary",),
                                             vmem_limit_bytes=VMEM_LIMIT_BYTES),
        name="inproj",
    )(x2d, norm_w, w_perm)


def _gdn_kernel(*refs, rows, chunk, seq_len, carry_mode):
    qkv_ref, z_ref, ba_ref, cw_ref, gp_ref, nw_ref = refs[:6]
    n_halo = 1 if carry_mode else GDN_CONV - 1
    halo_refs = refs[6:6 + n_halo]
    s_in_ref, o_ref, s_out_ref = refs[6 + n_halo:9 + n_halo]
    scratch = refs[9 + n_halo:]
    nh, hd, c = GDN_HEADS, GDN_HEAD_DIM, chunk
    n_chunks = rows // chunk
    n_seg = chunk // seq_len
    t_idx = pl.program_id(1)

    if carry_mode:
        s_scr, c_scr = scratch

        @pl.when(t_idx == 0)
        def _():
            s_scr[...] = s_in_ref[...]
            c_scr[...] = halo_refs[0][...]

    x = qkv_ref[...]
    w = cw_ref[...]
    acc = x * w[GDN_CONV - 1:GDN_CONV]
    for s in range(1, GDN_CONV):
        if carry_mode:
            xs = _shift_rows_carry(x, s, c_scr[...])
        else:
            xs = _shift_rows_fix(x, s, halo_refs[s - 1][...], seq_len)
        acc = acc + xs * w[GDN_CONV - 1 - s:GDN_CONV - s]
    if carry_mode:
        c_scr[...] = x[rows - SUBLANES:]
    act = _silu(acc)

    ones = jnp.ones((hd, hd), BF16)
    lane_sum = lambda a: _mm(a, ones)
    l2n = lambda a: a * lax.rsqrt(lane_sum(a * a) + L2_EPS)
    q_h = [l2n(act[:, h * hd:(h + 1) * hd]) * (hd ** -0.5) for h in range(nh)]
    k_h = [l2n(act[:, (nh + h) * hd:(nh + h + 1) * hd]) for h in range(nh)]
    v_h = [act[:, (2 * nh + h) * hd:(2 * nh + h + 1) * hd] for h in range(nh)]

    ba = ba_ref[...]
    gp = gp_ref[...]
    beta_all = _sigmoid(ba)
    lg_all = -jnp.exp(gp[0:1]) * _softplus(ba + gp[1:2])
    lane = _iota((c, LANES), 1)
    pick = lambda arr, col: jnp.sum(jnp.where(lane == col, arr, 0.0), axis=-1, keepdims=True)

    ti = _iota((c, c), 0)
    tj = _iota((c, c), 1)
    same = (ti ^ tj) < seq_len
    incl = same & (tj <= ti)
    strict = same & (tj < ti)
    incl_bf = incl.astype(BF16)
    same_bf = same.astype(BF16)
    incl_t_bf = (same & (ti <= tj)).astype(BF16)

    probs = []
    for ci in range(n_chunks):
        r = slice(ci * c, (ci + 1) * c)
        lgc = lg_all[r]
        gb_all = _mm_mask_lhs(incl_bf, lgc)
        gtot_all = _mm_mask_lhs(same_bf, lgc)
        gb_rows = _mm_mask_rhs(lgc, incl_t_bf, TN)
        for h in range(nh):
            probs.append(dict(ci=ci, h=h, q=q_h[h][r], k=k_h[h][r], v=v_h[h][r], beta=pick(beta_all[r], h),
                              gb=pick(gb_all, nh + h), gtot=pick(gtot_all, nh + h),
                              gb_row=gb_rows[nh + h:nh + h + 1, :]))

    decays = [jnp.exp(jnp.where(incl, p["gb"] - p["gb_row"], -jnp.inf)) for p in probs]
    kbetas = [p["k"] * p["beta"] for p in probs]
    kqs = [_mm(jnp.concatenate([kb, p["q"]], axis=0), p["k"], NT) for kb, p in zip(kbetas, probs)]
    neg_ls = [jnp.where(strict, -(kq[:c] * d), 0.0) for kq, d in zip(kqs, decays)]
    attns = [kq[c:] * d for kq, d in zip(kqs, decays)]
    e_gs = [jnp.exp(p["gb"]) for p in probs]
    if seq_len == c:
        t_invs = _unit_lower_inverses_halved(neg_ls)
    else:
        t_invs = _unit_lower_inverses(neg_ls, ti, tj, seq_len, _mm_acc)
    sols = [_mm_acc(t, jnp.concatenate([p["v"] * p["beta"], kb * eg], axis=1))
            for t, p, kb, eg in zip(t_invs, probs, kbetas, e_gs)]
    atts = [_mm(a, sol) for a, sol in zip(attns, sols)]
    q_effs = [p["q"] * eg - att[:, hd:] for p, eg, att in zip(probs, e_gs, atts)]
    k_decs = [p["k"] * jnp.exp(p["gtot"] - p["gb"]) for p in probs]
    g_lasts = [jnp.exp(p["gtot"]) for p in probs]
    nms = [[_mm(kd[n * seq_len:(n + 1) * seq_len], sol[n * seq_len:(n + 1) * seq_len], TN) for n in range(n_seg)]
           for kd, sol in zip(k_decs, sols)]

    states = [s_scr[h] for h in range(nh)] if carry_mode else None
    o_rows = [[] for _ in range(nh)]
    for idx, p in enumerate(probs):
        h, ci = p["h"], p["ci"]
        for n in range(n_seg):
            rs = slice(n * seq_len, (n + 1) * seq_len)
            seq = ci * n_seg + n
            s = states[h] if carry_mode else s_in_ref[seq, h]
            o_rows[h].append(_mm(q_effs[idx][rs], s) + atts[idx][rs, :hd])
            nm = nms[idx][n]
            s_new = s * g_lasts[idx][n * seq_len:n * seq_len + 1] - _mm(nm[:, hd:], s) + nm[:, :hd]
            if carry_mode:
                states[h] = s_new
            else:
                s_out_ref[seq, h] = s_new

    nw = nw_ref[...]
    for h in range(nh):
        o = _cat(o_rows[h], 0)
        o = o * lax.rsqrt(lane_sum(o * o) * (1.0 / hd) + NORM_EPS) * nw
        o_ref[:, h * hd:(h + 1) * hd] = o * _silu(z_ref[:, h * hd:(h + 1) * hd])

    if carry_mode:
        for h in range(nh):
            s_scr[h] = states[h]

        @pl.when(t_idx == pl.num_programs(1) - 1)
        def _():
            for h in range(nh):
                s_out_ref[h] = states[h]


def _gdn_mixer(qkv, z, ba, conv_w, gparams, norm_w, halo, s_in, *, rows, chunk, seq_len, carry_mode):
    bsz, t_len, _ = qkv.shape
    grid = (bsz, t_len // rows)
    tok = lambda width: pl.BlockSpec((None, rows, width), lambda b, t: (b, t, 0))
    const = lambda shape: pl.BlockSpec(shape, lambda b, t: (0,) * len(shape))
    in_specs = [tok(GDN_QKV), tok(GDN_WIDTH), tok(BA_PAD), const((GDN_CONV, GDN_QKV)), const((2, BA_PAD)),
                const((1, GDN_HEAD_DIM))]
    args = [qkv, z, ba, conv_w, gparams, norm_w]
    state_tail = (GDN_HEADS, GDN_HEAD_DIM, GDN_HEAD_DIM)
    if carry_mode:
        in_specs.append(pl.BlockSpec((None, SUBLANES, GDN_QKV), lambda b, t: (b, 0, 0)))
        args.append(halo)
        state_spec = pl.BlockSpec((None,) + state_tail, lambda b, t: (b, 0, 0, 0))
        scratch = [pltpu.VMEM(state_tail, F32), pltpu.VMEM((SUBLANES, GDN_QKV), F32)]
    else:
        in_specs += [tok(GDN_QKV)] * (GDN_CONV - 1)
        args += list(halo)
        state_spec = pl.BlockSpec((rows // seq_len,) + state_tail, lambda b, t: (t, 0, 0, 0))
        scratch = []
    in_specs.append(state_spec)
    args.append(s_in)
    return pl.pallas_call(
        functools.partial(_gdn_kernel, rows=rows, chunk=chunk, seq_len=seq_len, carry_mode=carry_mode),
        grid=grid,
        in_specs=in_specs,
        out_specs=[tok(GDN_WIDTH), state_spec],
        out_shape=[jax.ShapeDtypeStruct((bsz, t_len, GDN_WIDTH), F32),
                   jax.ShapeDtypeStruct(s_in.shape, F32)],
        scratch_shapes=scratch,
        compiler_params=pltpu.CompilerParams(dimension_semantics=("arbitrary",) * 2,
                                             vmem_limit_bytes=VMEM_LIMIT_BYTES),
        name="gdn_mixer",
    )(*args)


def _rwkv_kernel(*refs, rows, chunk, seq_len, carry_mode):
    (rkv_ref, lora_ref, mur_ref, mul_ref, par_ref, w2_ref, a2_ref, g2_ref, hr_ref, hl_ref, s_in_ref,
     o_ref, s_out_ref) = refs[:13]
    scratch = refs[13:]
    hd, npair, c, wd_all = RWKV_HEAD_DIM, RWKV_PAIRS, chunk, RWKV_WIDTH
    pw = 2 * hd
    n_chunks = rows // chunk
    n_seg = chunk // seq_len
    t_idx = pl.program_id(1)

    if carry_mode:
        s_scr, cr_scr, cl_scr = scratch

        @pl.when(t_idx == 0)
        def _():
            s_scr[...] = s_in_ref[...]
            cr_scr[...] = hr_ref[...]
            cl_scr[...] = hl_ref[...]

    def token_shift(x_ref, mu_ref, h_ref, c_scr):
        x = x_ref[...]
        if carry_mode:
            prev = _shift_rows_carry(x, 1, c_scr[...])
            c_scr[...] = x[rows - SUBLANES:]
        else:
            prev = _shift_rows_fix(x, 1, h_ref[...], seq_len)
        return x + mu_ref[...] * (prev - x)

    xs = token_shift(rkv_ref, mur_ref, hr_ref, cr_scr if carry_mode else None)
    lora = token_shift(lora_ref, mul_ref, hl_ref, cl_scr if carry_mode else None)
    r, kr, vr = xs[:, :wd_all], xs[:, wd_all:2 * wd_all], xs[:, 2 * wd_all:]
    par = par_ref[...]
    w0, a0, k_k, k_a, r_k, ln_w, ln_b = (par[n:n + 1] for n in range(7))
    wd = lora[:, :DECAY_LORA]
    ad = lora[:, DECAY_LORA:DECAY_LORA + AAA_LORA]
    gd = lora[:, DECAY_LORA + AAA_LORA:]
    lw = -(DECAY_SCALE * _sigmoid(w0 + _mm(jnp.tanh(wd), w2_ref[...])))
    asig = _sigmoid(a0 + _mm(ad, a2_ref[...]))
    g = _mm(_sigmoid(gd), g2_ref[...])

    li = _iota((pw, pw), 0)
    lj = _iota((pw, pw), 1)
    pair_bd = (li >= hd) == (lj >= hd)
    head_ones = pair_bd.astype(BF16)

    def head_sum(a):
        return _cat([_mm(a[:, p * pw:(p + 1) * pw], head_ones) for p in range(a.shape[1] // pw)], 1)

    kk = kr * k_k
    kk = kk * lax.rsqrt(head_sum(kk * kk) + L2_EPS)
    kr = kr * (1.0 + (asig - 1.0) * k_a)
    a_vec = -kk
    b_vec = kk * asig

    i1 = _iota((c, c), 0)
    j1 = _iota((c, c), 1)
    same1 = (i1 ^ j1) < seq_len
    incl1_bf = (same1 & (j1 <= i1)).astype(BF16)
    same1_bf = same1.astype(BF16)
    ip = _iota((c, 2 * c), 0)
    jp_full = _iota((c, 2 * c), 1)
    jp = jp_full & (c - 1)
    head1_cols = jp_full >= c
    same_p = (ip ^ jp) < seq_len
    incl_p = same_p & (jp <= ip)
    strict_p = same_p & (jp < ip)

    def stack_heads(a):
        head1 = (_iota(a.shape, 1) & hd) != 0
        return jnp.concatenate([jnp.where(head1, 0.0, a), jnp.where(head1, a, 0.0)], axis=0)

    def block_diag(m):
        return jnp.concatenate([jnp.where(head1_cols, 0.0, m), jnp.where(head1_cols, m, 0.0)], axis=0)

    pair_mul = lambda a, b: _mm_acc(a, block_diag(b))

    probs = []
    for ci in range(n_chunks):
        rs = slice(ci * c, (ci + 1) * c)
        lwc = lw[rs]
        g_incl = _mm_mask_lhs(incl1_bf, lwc)
        g_mid = 0.5 * _mm_mask_lhs(same1_bf, lwc)
        e_mid = jnp.exp(g_mid)
        inv_n = jnp.exp(g_mid - g_incl)
        r_n = r[rs] * jnp.exp(g_incl - g_mid)
        a_n = a_vec[rs] * jnp.exp(g_incl - lwc - g_mid)
        to_end = inv_n * e_mid
        full = dict(r_n=r_n, a_n=a_n, b_n=b_vec[rs] * inv_n, k_n=kr[rs] * inv_n, r_t=r_n * e_mid,
                    a_t=a_n * e_mid, b_end=b_vec[rs] * to_end, k_end=kr[rs] * to_end,
                    gam=e_mid * e_mid, v=vr[rs])
        for p in range(npair):
            prob = {name: val[:, p * pw:(p + 1) * pw] for name, val in full.items()}
            prob.update(ci=ci, p=p)
            probs.append(prob)

    ars = [jnp.concatenate([p["a_n"], p["r_n"]], axis=0) for p in probs]
    m_bks = [_mm(ar, jnp.concatenate([stack_heads(p["b_n"]), stack_heads(p["k_n"])], axis=0), NT)
             for ar, p in zip(ars, probs)]
    a_abs = [jnp.where(strict_p, m[:c, :2 * c], 0.0) for m in m_bks]
    a_rbs = [jnp.where(incl_p, m[c:, :2 * c], 0.0) for m in m_bks]
    a_aks = [jnp.where(strict_p, m[:c, 2 * c:], 0.0) for m in m_bks]
    a_rks = [jnp.where(incl_p, m[c:, 2 * c:], 0.0) for m in m_bks]
    t_invs = _unit_lower_inverses(a_abs, ip, jp, seq_len, pair_mul)
    akvs = [_mm(jnp.concatenate([a_ak, a_rk], axis=0), stack_heads(p["v"]))
            for a_ak, a_rk, p in zip(a_aks, a_rks, probs)]
    uws = [_mm_acc(t, stack_heads(jnp.concatenate([akv[:c], p["a_t"]], axis=1)))
           for t, akv, p in zip(t_invs, akvs, probs)]
    rbs = [_mm(a_rb, stack_heads(uw)) for a_rb, uw in zip(a_rbs, uws)]
    y_frees = [rb[:, :pw] + akv[c:] for rb, akv in zip(rbs, akvs)]
    r_effs = [p["r_t"] + rb[:, pw:] for p, rb in zip(probs, rbs)]
    m_corrs, n_adds = [], []
    for p, uw in zip(probs, uws):
        ms, ns = [], []
        for n in range(n_seg):
            sr = slice(n * seq_len, (n + 1) * seq_len)
            zero = jnp.zeros((seq_len, pw), F32)
            lhs = jnp.concatenate([jnp.concatenate([uw[sr, pw:], uw[sr, :pw]], axis=1),
                                   jnp.concatenate([zero, p["v"][sr]], axis=1)], axis=0)
            rhs = jnp.concatenate([p["b_end"][sr], p["k_end"][sr]], axis=0)
            mn = _mm_acc(lhs, rhs, TN)
            ms.append(jnp.where(pair_bd, mn[:pw], 0.0))
            ns.append(jnp.where(pair_bd, mn[pw:], 0.0))
        m_corrs.append(ms)
        n_adds.append(ns)

    states = [s_scr[p] for p in range(npair)] if carry_mode else None
    y_rows = [[] for _ in range(npair)]
    for idx, prob in enumerate(probs):
        p, ci = prob["p"], prob["ci"]
        for n in range(n_seg):
            sr = slice(n * seq_len, (n + 1) * seq_len)
            seq = ci * n_seg + n
            s = states[p] if carry_mode else s_in_ref[seq, p]
            y_rows[p].append(_mm_acc(r_effs[idx][sr], s, NT) + y_frees[idx][sr])
            gam = prob["gam"][n * seq_len:n * seq_len + 1]
            s_new = s * gam + _mm_acc(s, m_corrs[idx][n]) + n_adds[idx][n]
            if carry_mode:
                states[p] = s_new
            else:
                s_out_ref[seq, p] = s_new

    y = _cat([_cat(y_rows[p], 0) for p in range(npair)], 1)
    y_c = y - head_sum(y) * (1.0 / hd)
    y_n = y_c * lax.rsqrt(head_sum(y_c * y_c) * (1.0 / hd) + RWKV_LN_EPS)
    y_n = y_n * ln_w + ln_b
    bonus = head_sum(r * kr * r_k) * vr
    o_ref[...] = (y_n + bonus) * g

    if carry_mode:
        for p in range(npair):
            s_scr[p] = states[p]

        @pl.when(t_idx == pl.num_programs(1) - 1)
        def _():
            for p in range(npair):
                s_out_ref[p] = states[p]


def _rwkv_mixer(rkv, lora, mu_rkv, mu_lora, params, w2, a2, g2, halo_rkv, halo_lora, s_in, *,
                rows, chunk, seq_len, carry_mode):
    bsz, t_len, _ = rkv.shape
    grid = (bsz, t_len // rows)
    tok = lambda width: pl.BlockSpec((None, rows, width), lambda b, t: (b, t, 0))
    const = lambda shape: pl.BlockSpec(shape, lambda b, t: (0,) * len(shape))
    in_specs = [tok(3 * RWKV_WIDTH), tok(LORA_PAD), const((1, 3 * RWKV_WIDTH)), const((1, LORA_PAD)),
                const((SUBLANES, RWKV_WIDTH)), const((DECAY_LORA, RWKV_WIDTH)), const((AAA_LORA, RWKV_WIDTH)),
                const((LORA_PAD - DECAY_LORA - AAA_LORA, RWKV_WIDTH))]
    state_tail = (RWKV_PAIRS, LANES, LANES)
    if carry_mode:
        in_specs += [pl.BlockSpec((None, SUBLANES, 3 * RWKV_WIDTH), lambda b, t: (b, 0, 0)),
                     pl.BlockSpec((None, SUBLANES, LORA_PAD), lambda b, t: (b, 0, 0))]
        state_spec = pl.BlockSpec((None,) + state_tail, lambda b, t: (b, 0, 0, 0))
        scratch = [pltpu.VMEM(state_tail, F32), pltpu.VMEM((SUBLANES, 3 * RWKV_WIDTH), F32),
                   pltpu.VMEM((SUBLANES, LORA_PAD), F32)]
    else:
        in_specs += [tok(3 * RWKV_WIDTH), tok(LORA_PAD)]
        state_spec = pl.BlockSpec((rows // seq_len,) + state_tail, lambda b, t: (t, 0, 0, 0))
        scratch = []
    in_specs.append(state_spec)
    args = [rkv, lora, mu_rkv, mu_lora, params, w2, a2, g2, halo_rkv, halo_lora, s_in]
    return pl.pallas_call(
        functools.partial(_rwkv_kernel, rows=rows, chunk=chunk, seq_len=seq_len, carry_mode=carry_mode),
        grid=grid,
        in_specs=in_specs,
        out_specs=[tok(RWKV_WIDTH), state_spec],
        out_shape=[jax.ShapeDtypeStruct((bsz, t_len, RWKV_WIDTH), F32),
                   jax.ShapeDtypeStruct(s_in.shape, F32)],
        scratch_shapes=scratch,
        compiler_params=pltpu.CompilerParams(dimension_semantics=("arbitrary",) * 2,
                                             vmem_limit_bytes=VMEM_LIMIT_BYTES),
        name="rwkv_mixer",
    )(*args)


def _ffn_kernel(*refs, rows, seq_len, blocks_per_seq, carry_mode):
    x_ref, og_ref, orw_ref, wo_ref, nf_ref, wu_ref, cw_ref, wd_ref, nfin_ref = refs[:9]
    if carry_mode:
        h_ref, y_ref, tail_ref, carry_scr = refs[9:]
    else:
        f1_ref, f2_ref, y_ref, tail_ref = refs[9:]
    blk = pl.program_id(0)

    if carry_mode:
        @pl.when(blk % blocks_per_seq == 0)
        def _():
            carry_scr[...] = h_ref[...]

    o = (jnp.dot(og_ref[...].astype(BF16), wo_ref[:GDN_WIDTH], preferred_element_type=F32)
         + jnp.dot(orw_ref[...].astype(BF16), wo_ref[GDN_WIDTH:], preferred_element_type=F32))
    x1 = x_ref[...] + o
    h2 = _rms_norm(x1, nf_ref[...]).astype(BF16)
    gate = jnp.dot(h2, wu_ref[:, :FFN_DIM], preferred_element_type=F32)
    val = jnp.dot(h2, wu_ref[:, FFN_DIM:], preferred_element_type=F32)
    cw = cw_ref[...]
    conv = gate * cw[FFN_CONV - 1:FFN_CONV]
    for s in range(1, FFN_CONV):
        if carry_mode:
            gs = _shift_rows_carry(gate, s, carry_scr[...])
        else:
            gs = _shift_rows_fix(gate, s, (f1_ref, f2_ref)[s - 1][...], seq_len)
        conv = conv + gs * cw[FFN_CONV - 1 - s:FFN_CONV - s]
    if carry_mode:
        carry_scr[...] = gate[rows - SUBLANES:]
        tail_ref[...] = gate[rows - SUBLANES:]
    else:
        tail_ref[...] = gate
    act = (_silu(conv) * val).astype(BF16)
    x2 = x1 + jnp.dot(act, wd_ref[...], preferred_element_type=F32)
    y_ref[...] = _rms_norm(x2, nfin_ref[...])


def _ffn(x2d, o_gdn, o_rwkv, w_out, norm_ffn, w_up, conv_w, w_down, norm_final, halo, *,
         rows, seq_len, carry_mode):
    n_rows = x2d.shape[0]
    n_blocks = n_rows // rows
    row_spec = lambda w: pl.BlockSpec((rows, w), lambda i: (i, 0))
    whole = lambda shape: pl.BlockSpec(shape, lambda i: (0,) * len(shape), pipeline_mode=pl.Buffered(1))
    in_specs = [row_spec(D_MODEL), row_spec(GDN_WIDTH), row_spec(RWKV_WIDTH),
                whole((D_MODEL, D_MODEL)), whole((1, D_MODEL)), whole((D_MODEL, 2 * FFN_DIM)),
                whole((FFN_CONV, FFN_DIM)), whole((FFN_DIM, D_MODEL)), whole((1, D_MODEL))]
    args = [x2d, o_gdn, o_rwkv, w_out, norm_ffn, w_up, conv_w, w_down, norm_final]
    if carry_mode:
        blocks_per_seq = seq_len // rows
        in_specs.append(pl.BlockSpec((None, SUBLANES, FFN_DIM), lambda i: (i // blocks_per_seq, 0, 0)))
        args.append(halo)
        tail_spec = pl.BlockSpec((None, SUBLANES, FFN_DIM), lambda i: (i, 0, 0))
        tail_shape = jax.ShapeDtypeStruct((n_blocks, SUBLANES, FFN_DIM), F32)
        scratch = [pltpu.VMEM((SUBLANES, FFN_DIM), F32)]
    else:
        blocks_per_seq = 1
        in_specs += [row_spec(FFN_DIM), row_spec(FFN_DIM)]
        args += [halo[0], halo[1]]
        tail_spec = row_spec(FFN_DIM)
        tail_shape = jax.ShapeDtypeStruct((n_rows, FFN_DIM), F32)
        scratch = []
    return pl.pallas_call(
        functools.partial(_ffn_kernel, rows=rows, seq_len=seq_len, blocks_per_seq=blocks_per_seq,
                          carry_mode=carry_mode),
        grid=(n_blocks,),
        in_specs=in_specs,
        out_specs=[row_spec(D_MODEL), tail_spec],
        out_shape=[jax.ShapeDtypeStruct((n_rows, D_MODEL), F32), tail_shape],
        scratch_shapes=scratch,
        compiler_params=pltpu.CompilerParams(dimension_semantics=("arbitrary",),
                                             vmem_limit_bytes=VMEM_LIMIT_BYTES),
        name="ffn",
    )(*args)


def _pad_cols(a, width):
    return jnp.pad(a, ((0, 0), (0, width - a.shape[1])))


def _regroup_in_weights(w_in):
    g_end = GDN_QKV + GDN_WIDTH
    ba = w_in[:, g_end:g_end + 2 * GDN_HEADS]
    rw = w_in[:, g_end + 2 * GDN_HEADS:]
    return jnp.concatenate([w_in[:, :g_end], rw[:, :3 * RWKV_WIDTH],
                            _pad_cols(rw[:, 3 * RWKV_WIDTH:], LORA_PAD), _pad_cols(ba, BA_PAD)],
                           axis=1).astype(BF16)


def _tail_tile(state_rows):
    return jnp.pad(state_rows, ((0, 0), (SUBLANES - state_rows.shape[1], 0), (0, 0)))


def _fix_rows(state_rows, s, seq_len):
    n = state_rows.shape[1]
    first = state_rows[:, n - s:, :]
    return jnp.pad(first, ((0, 0), (0, seq_len - s), (0, 0))).reshape(-1, state_rows.shape[2])


def _pair_states(s):
    bsz = s.shape[0]
    sp = s.reshape(bsz, RWKV_PAIRS, 2, RWKV_HEAD_DIM, RWKV_HEAD_DIM)
    zero = jnp.zeros_like(sp[:, :, 0])
    top = jnp.concatenate([sp[:, :, 0], zero], axis=-1)
    bot = jnp.concatenate([zero, sp[:, :, 1]], axis=-1)
    return jnp.concatenate([top, bot], axis=-2)


def _unpair_states(sp):
    hd = RWKV_HEAD_DIM
    bsz = sp.shape[0]
    return jnp.stack([sp[:, :, :hd, :hd], sp[:, :, hd:, hd:]], axis=2).reshape(bsz, RWKV_HEADS, hd, hd)


def _layer(x, st_gdn_conv, st_gdn, st_shift, st_rwkv, st_ffn, wts, *, carry_mode):
    bsz, t_len, _ = x.shape
    n_rows = bsz * t_len
    x2d = x.reshape(n_rows, D_MODEL)
    qkv, z, rkv, lora, ba = _inproj(x2d, wts["norm_mix"], wts["w_in"])

    shift_rkv = st_shift[:, :, :3 * RWKV_WIDTH]
    shift_lora = _pad_cols(st_shift[:, 0, 3 * RWKV_WIDTH:], LORA_PAD)[:, None, :]
    if carry_mode:
        shape3 = lambda a: a.reshape(bsz, t_len, a.shape[-1])
        o_gdn, s_gdn = _gdn_mixer(shape3(qkv), shape3(z), shape3(ba), wts["gdn_conv_w"], wts["gparams"],
                                  wts["gdn_norm"], _tail_tile(st_gdn_conv), st_gdn,
                                  rows=MIXER_ROWS, chunk=GDN_CHUNK, seq_len=GDN_CHUNK, carry_mode=True)
        o_rwkv, s_rwkv = _rwkv_mixer(shape3(rkv), shape3(lora), wts["mu_rkv"], wts["mu_lora"], wts["rparams"],
                                     wts["w2"], wts["a2"], wts["g2"], _tail_tile(shift_rkv),
                                     _tail_tile(shift_lora), _pair_states(st_rwkv),
                                     rows=MIXER_ROWS, chunk=RWKV_CHUNK, seq_len=RWKV_CHUNK, carry_mode=True)
        ffn_halo = _tail_tile(st_ffn)
        ffn_rows, ffn_seq = ROW_BLOCK, t_len
    else:
        shape3 = lambda a: a.reshape(1, n_rows, a.shape[-1])
        fix = lambda st, s: _fix_rows(st, s, t_len)[None]
        o_gdn, s_gdn = _gdn_mixer(shape3(qkv), shape3(z), shape3(ba), wts["gdn_conv_w"], wts["gparams"],
                                  wts["gdn_norm"], [fix(st_gdn_conv, s) for s in range(1, GDN_CONV)], st_gdn,
                                  rows=n_rows, chunk=GDN_CHUNK, seq_len=t_len, carry_mode=False)
        o_rwkv, s_rwkv = _rwkv_mixer(shape3(rkv), shape3(lora), wts["mu_rkv"], wts["mu_lora"], wts["rparams"],
                                     wts["w2"], wts["a2"], wts["g2"], fix(shift_rkv, 1), fix(shift_lora, 1),
                                     _pair_states(st_rwkv),
                                     rows=n_rows, chunk=RWKV_CHUNK, seq_len=t_len, carry_mode=False)
        ffn_halo = [_fix_rows(st_ffn, s, t_len) for s in range(1, FFN_CONV)]
        ffn_rows, ffn_seq = n_rows, t_len
    y2d, gate_tail = _ffn(x2d, o_gdn.reshape(n_rows, GDN_WIDTH), o_rwkv.reshape(n_rows, RWKV_WIDTH),
                          wts["w_out"], wts["norm_ffn"], wts["w_up"], wts["ffn_conv_w"], wts["w_down"],
                          wts["norm_final"], ffn_halo, rows=ffn_rows, seq_len=ffn_seq, carry_mode=carry_mode)

    qkv3 = qkv.reshape(bsz, t_len, GDN_QKV)
    gdn_conv_new = qkv3[:, t_len - (GDN_CONV - 1):, :]
    shift_new = jnp.concatenate([rkv.reshape(bsz, t_len, -1)[:, t_len - 1:, :],
                                 lora.reshape(bsz, t_len, -1)[:, t_len - 1:, :RWKV_PROJ - 3 * RWKV_WIDTH]], axis=-1)
    if carry_mode:
        blocks_per_seq = t_len // ROW_BLOCK
        tails = gate_tail.reshape(bsz, blocks_per_seq, SUBLANES, FFN_DIM)[:, -1]
        ffn_new = tails[:, SUBLANES - (FFN_CONV - 1):, :]
    else:
        ffn_new = gate_tail.reshape(bsz, t_len, FFN_DIM)[:, t_len - (FFN_CONV - 1):, :]
    return (y2d.reshape(bsz, t_len, D_MODEL), gdn_conv_new, s_gdn, shift_new, _unpair_states(s_rwkv), ffn_new)


def kernel(x_prompt, x_sample, state_gdn_conv, state_gdn, state_rwkv_shift, state_rwkv, state_ffn_conv,
           norm_mix, w_in, gdn_conv_w, gdn_a_log, gdn_dt_bias, gdn_norm,
           rwkv_mu, rwkv_w0, rwkv_w2, rwkv_a0, rwkv_a2, rwkv_g2, rwkv_k_k, rwkv_k_a, rwkv_r_k,
           rwkv_ln_w, rwkv_ln_b, w_out, norm_ffn, w_up, ffn_conv_w, w_down, norm_final):
    layer = 0
    gparams = jnp.zeros((2, BA_PAD), F32)
    gparams = gparams.at[0, GDN_HEADS:2 * GDN_HEADS].set(gdn_a_log[layer])
    gparams = gparams.at[1, GDN_HEADS:2 * GDN_HEADS].set(gdn_dt_bias[layer])
    mu = rwkv_mu[layer][None, :]
    rparams = jnp.stack([rwkv_w0[layer], rwkv_a0[layer], rwkv_k_k[layer], rwkv_k_a[layer],
                         rwkv_r_k[layer].reshape(-1), rwkv_ln_w[layer], rwkv_ln_b[layer],
                         jnp.zeros((RWKV_WIDTH,), F32)])
    gate_rows = LORA_PAD - DECAY_LORA - AAA_LORA
    wts = {
        "norm_mix": norm_mix[layer][None, :],
        "w_in": _regroup_in_weights(w_in[layer]),
        "gdn_conv_w": gdn_conv_w[layer],
        "gparams": gparams,
        "gdn_norm": gdn_norm[layer][None, :],
        "mu_rkv": mu[:, :3 * RWKV_WIDTH],
        "mu_lora": _pad_cols(mu[:, 3 * RWKV_WIDTH:], LORA_PAD),
        "rparams": rparams,
        "w2": rwkv_w2[layer].astype(BF16),
        "a2": rwkv_a2[layer].astype(BF16),
        "g2": jnp.pad(rwkv_g2[layer], ((0, gate_rows - GATE_LORA), (0, 0))).astype(BF16),
        "w_out": w_out[layer].astype(BF16),
        "norm_ffn": norm_ffn[layer][None, :],
        "w_up": w_up[layer].astype(BF16),
        "ffn_conv_w": ffn_conv_w[layer],
        "w_down": w_down[layer].astype(BF16),
        "norm_final": norm_final[None, :],
    }
    n_prompt = x_prompt.shape[0]
    zeros = lambda c: jnp.zeros((n_prompt,) + c.shape[2:], c.dtype)
    outs_p = _layer(x_prompt, zeros(state_gdn_conv), zeros(state_gdn), zeros(state_rwkv_shift),
                    zeros(state_rwkv), zeros(state_ffn_conv), wts, carry_mode=True)
    outs_s = _layer(x_sample, state_gdn_conv[layer], state_gdn[layer], state_rwkv_shift[layer],
                    state_rwkv[layer], state_ffn_conv[layer], wts, carry_mode=False)
    y_p, y_s = outs_p[0], outs_s[0]
    states = []
    for n in range(1, 6):
        states += [outs_p[n][None], outs_s[n][None]]
    return (y_p, y_s, *states)
```

```python
import functools
import math

import jax
import jax.numpy as jnp
from jax import lax
from jax.experimental import pallas as pl
from jax.experimental.pallas import tpu as pltpu

F32 = jnp.float32
BF16 = jnp.bfloat16

D_MODEL = 1024
GDN_HEAD_DIM = 128
GDN_HEADS = 4
GDN_WIDTH = GDN_HEADS * GDN_HEAD_DIM
GDN_QKV = 3 * GDN_WIDTH
GDN_CONV = 4
RWKV_HEAD_DIM = 64
RWKV_HEADS = 8
RWKV_WIDTH = RWKV_HEADS * RWKV_HEAD_DIM
RWKV_PAIRS = RWKV_HEADS // 2
DECAY_LORA = 64
AAA_LORA = 64
GATE_LORA = 160
RWKV_PROJ = 3 * RWKV_WIDTH + DECAY_LORA + AAA_LORA + GATE_LORA
FFN_DIM = 2816
FFN_CONV = 3
NORM_EPS = 1e-6
L2_EPS = 1e-6
RWKV_LN_EPS = 64e-5
DECAY_SCALE = math.exp(-0.5)

LANES = 128
SUBLANES = 8
VMEM_LIMIT_BYTES = 56 * 1024 * 1024

LORA_PAD = 384
BA_PAD = LANES
P_PAD = GDN_QKV + GDN_WIDTH + 3 * RWKV_WIDTH + LORA_PAD + BA_PAD
OFF_QKV = 0
OFF_Z = OFF_QKV + GDN_QKV
OFF_RKV = OFF_Z + GDN_WIDTH
OFF_LORA = OFF_RKV + 3 * RWKV_WIDTH
OFF_BA = OFF_LORA + LORA_PAD

INPROJ_ROWS = 1024
ROW_BLOCK = 512
FFN_COL_CHUNKS = (768, 768, 768, 512)
assert sum(FFN_COL_CHUNKS) == FFN_DIM
MIXER_ROWS = 512
GDN_CHUNK = 128
RWKV_CHUNK = 64

NN = (((1,), (0,)), ((), ()))
NT = (((1,), (1,)), ((), ()))
TN = (((0,), (0,)), ((), ()))


def _dot(a, b, dims):
    return lax.dot_general(a, b, dims, preferred_element_type=F32)


def _mm(a, b, dims=NN):
    return _dot(a.astype(BF16), b.astype(BF16), dims)


def _split2(a):
    hi = a.astype(BF16)
    return hi, (a - hi.astype(F32)).astype(BF16)


def _mm_mask_lhs(mask_bf16, x, dims=NN):
    hi, lo = _split2(x)
    return _dot(mask_bf16, hi, dims) + _dot(mask_bf16, lo, dims)


def _mm_mask_rhs(x, mask_bf16, dims=NN):
    hi, lo = _split2(x)
    return _dot(hi, mask_bf16, dims) + _dot(lo, mask_bf16, dims)


_mm_acc = _mm


def _iota(shape, dim):
    return lax.broadcasted_iota(jnp.int32, shape, dim)


def _sigmoid(x):
    return 0.5 + 0.5 * jnp.tanh(0.5 * x)


def _silu(x):
    h = 0.5 * x
    return h + h * jnp.tanh(h)


def _softplus(x):
    return jnp.maximum(x, 0.0) + jnp.log1p(jnp.exp(-jnp.abs(x)))


def _rms_norm(x, w):
    return x * lax.rsqrt(jnp.mean(x * x, axis=-1, keepdims=True) + NORM_EPS) * w


def _cat(parts, axis):
    return parts[0] if len(parts) == 1 else jnp.concatenate(parts, axis=axis)


def _shift_rows_carry(x, s, carry):
    xr = pltpu.roll(x, s, 0)
    cr = pltpu.roll(carry, s, 0)
    row = _iota((SUBLANES, x.shape[1]), 0)
    head = jnp.where(row < s, cr, xr[:SUBLANES])
    return jnp.concatenate([head, xr[SUBLANES:]], axis=0)


def _shift_rows_fix(x, s, fix, seq_len):
    xr = pltpu.roll(x, s, 0)
    row = _iota(x.shape, 0)
    return jnp.where((row & (seq_len - 1)) < s, fix, xr)


def _unit_lower_inverses(ps, ti, tj, seq_len, mul):
    def joins(s):
        return ((ti ^ tj) < 2 * s) & ((ti & s) != 0) & ((tj & s) == 0)

    eye = jnp.where(ti == tj, 1.0, 0.0)
    ts = [eye + jnp.where(joins(1), p, 0.0) for p in ps]
    s = 2
    while s < seq_len:
        mask = joins(s)
        tbs = [mul(t, jnp.where(mask, p, 0.0)) for t, p in zip(ts, ps)]
        ts = [t + mul(tb, t) for tb, t in zip(tbs, ts)]
        s *= 2
    return ts


def _unit_lower_inverses_halved(ps):
    c = ps[0].shape[0]
    half = c // 2
    ti = _iota((half, c), 0)
    lane = _iota((half, c), 1)
    right = lane >= half
    zeros = jnp.zeros((half, c), F32)

    def block_diag(m):
        return jnp.concatenate([jnp.where(right, 0.0, m), jnp.where(right, m, 0.0)], axis=0)

    diag = [jnp.where(right, p[half:], p[:half]) for p in ps]
    t_ab = _unit_lower_inverses(diag, ti, lane & (half - 1), half, lambda a, b: _mm_acc(a, block_diag(b)))
    ys = [_mm_acc(jnp.where(right, 0.0, p[half:]), jnp.concatenate([jnp.where(right, 0.0, t), zeros], axis=0))
          for p, t in zip(ps, t_ab)]
    x21s = [_mm_acc(t, jnp.concatenate([zeros, y], axis=0)) for t, y in zip(t_ab, ys)]
    return [jnp.concatenate([jnp.where(right, 0.0, t), x21 + jnp.where(right, t, 0.0)], axis=0)
            for t, x21 in zip(t_ab, x21s)]


def _inproj_kernel(x_ref, nw_ref, w_ref, qkv_ref, z_ref, rkv_ref, lora_ref, ba_ref):
    h = _rms_norm(x_ref[...], nw_ref[...]).astype(BF16)
    for out_ref, off in ((qkv_ref, OFF_QKV), (z_ref, OFF_Z), (rkv_ref, OFF_RKV), (lora_ref, OFF_LORA),
                         (ba_ref, OFF_BA)):
        width = out_ref.shape[1]
        out_ref[...] = jnp.dot(h, w_ref[:, off:off + width], preferred_element_type=F32)


def _inproj(x2d, norm_w, w_perm):
    n_rows = x2d.shape[0]
    rows = min(INPROJ_ROWS, n_rows)
    widths = (GDN_QKV, GDN_WIDTH, 3 * RWKV_WIDTH, LORA_PAD, BA_PAD)
    return pl.pallas_call(
        _inproj_kernel,
        grid=(n_rows // rows,),
        in_specs=[
            pl.BlockSpec((rows, D_MODEL), lambda i: (i, 0)),
            pl.BlockSpec((1, D_MODEL), lambda i: (0, 0)),
            pl.BlockSpec((D_MODEL, P_PAD), lambda i: (0, 0), pipeline_mode=pl.Buffered(1)),
        ],
        out_specs=[pl.BlockSpec((rows, w), lambda i: (i, 0)) for w in widths],
        out_shape=[jax.ShapeDtypeStruct((n_rows, w), F32) for w in widths],
        compiler_params=pltpu.CompilerParams(dimension_semantics=("arbitrary",),
                                             vmem_limit_bytes=VMEM_LIMIT_BYTES),
        name="inproj",
    )(x2d, norm_w, w_perm)


def _gdn_kernel(*refs, rows, chunk, seq_len, carry_mode):
    qkv_ref, z_ref, ba_ref, cw_ref, gp_ref, nw_ref = refs[:6]
    n_halo = 1 if carry_mode else GDN_CONV - 1
    halo_refs = refs[6:6 + n_halo]
    s_in_ref, o_ref, s_out_ref = refs[6 + n_halo:9 + n_halo]
    scratch = refs[9 + n_halo:]
    nh, hd, c = GDN_HEADS, GDN_HEAD_DIM, chunk
    n_chunks = rows // chunk
    n_seg = chunk // seq_len
    t_idx = pl.program_id(1)

    if carry_mode:
        s_scr, c_scr = scratch

        @pl.when(t_idx == 0)
        def _():
            s_scr[...] = s_in_ref[...]
            c_scr[...] = halo_refs[0][...]

    x = qkv_ref[...]
    w = cw_ref[...]
    acc = x * w[GDN_CONV - 1:GDN_CONV]
    for s in range(1, GDN_CONV):
        if carry_mode:
            xs = _shift_rows_carry(x, s, c_scr[...])
        else:
            xs = _shift_rows_fix(x, s, halo_refs[s - 1][...], seq_len)
        acc = acc + xs * w[GDN_CONV - 1 - s:GDN_CONV - s]
    if carry_mode:
        c_scr[...] = x[rows - SUBLANES:]
    act = _silu(acc)

    ones = jnp.ones((hd, hd), BF16)
    lane_sum = lambda a: _mm(a, ones)
    l2n = lambda a: a * lax.rsqrt(lane_sum(a * a) + L2_EPS)
    q_h = [l2n(act[:, h * hd:(h + 1) * hd]) * (hd ** -0.5) for h in range(nh)]
    k_h = [l2n(act[:, (nh + h) * hd:(nh + h + 1) * hd]) for h in range(nh)]
    v_h = [act[:, (2 * nh + h) * hd:(2 * nh + h + 1) * hd] for h in range(nh)]

    ba = ba_ref[...]
    gp = gp_ref[...]
    beta_all = _sigmoid(ba)
    lg_all = -jnp.exp(gp[0:1]) * _softplus(ba + gp[1:2])
    lane = _iota((c, LANES), 1)
    pick = lambda arr, col: jnp.sum(jnp.where(lane == col, arr, 0.0), axis=-1, keepdims=True)

    ti = _iota((c, c), 0)
    tj = _iota((c, c), 1)
    same = (ti ^ tj) < seq_len
    incl = same & (tj <= ti)
    strict = same & (tj < ti)
    incl_bf = incl.astype(BF16)
    same_bf = same.astype(BF16)
    incl_t_bf = (same & (ti <= tj)).astype(BF16)

    probs = []
    for ci in range(n_chunks):
        r = slice(ci * c, (ci + 1) * c)
        lgc = lg_all[r]
        gb_all = _mm_mask_lhs(incl_bf, lgc)
        gtot_all = _mm_mask_lhs(same_bf, lgc)
        gb_rows = _mm_mask_rhs(lgc, incl_t_bf, TN)
        for h in range(nh):
            probs.append(dict(ci=ci, h=h, q=q_h[h][r], k=k_h[h][r], v=v_h[h][r], beta=pick(beta_all[r], h),
                              gb=pick(gb_all, nh + h), gtot=pick(gtot_all, nh + h),
                              gb_row=gb_rows[nh + h:nh + h + 1, :]))

    decays = [jnp.exp(jnp.where(incl, p["gb"] - p["gb_row"], -jnp.inf)) for p in probs]
    kbetas = [p["k"] * p["beta"] for p in probs]
    kqs = [_mm(jnp.concatenate([kb, p["q"]], axis=0), p["k"], NT) for kb, p in zip(kbetas, probs)]
    neg_ls = [jnp.where(strict, -(kq[:c] * d), 0.0) for kq, d in zip(kqs, decays)]
    attns = [kq[c:] * d for kq, d in zip(kqs, decays)]
    e_gs = [jnp.exp(p["gb"]) for p in probs]
    if seq_len == c:
        t_invs = _unit_lower_inverses_halved(neg_ls)
    else:
        t_invs = _unit_lower_inverses(neg_ls, ti, tj, seq_len, _mm_acc)
    sols = [_mm_acc(t, jnp.concatenate([p["v"] * p["beta"], kb * eg], axis=1))
            for t, p, kb, eg in zip(t_invs, probs, kbetas, e_gs)]
    atts = [_mm(a, sol) for a, sol in zip(attns, sols)]
    q_effs = [p["q"] * eg - att[:, hd:] for p, eg, att in zip(probs, e_gs, atts)]
    k_decs = [p["k"] * jnp.exp(p["gtot"] - p["gb"]) for p in probs]
    g_lasts = [jnp.exp(p["gtot"]) for p in probs]
    nms = [[_mm(kd[n * seq_len:(n + 1) * seq_len], sol[n * seq_len:(n + 1) * seq_len], TN) for n in range(n_seg)]
           for kd, sol in zip(k_decs, sols)]

    states = [s_scr[h] for h in range(nh)] if carry_mode else None
    o_rows = [[] for _ in range(nh)]
    for idx, p in enumerate(probs):
        h, ci = p["h"], p["ci"]
        for n in range(n_seg):
            rs = slice(n * seq_len, (n + 1) * seq_len)
            seq = ci * n_seg + n
            s = states[h] if carry_mode else s_in_ref[seq, h]
            o_rows[h].append(_mm(q_effs[idx][rs], s) + atts[idx][rs, :hd])
            nm = nms[idx][n]
            s_new = s * g_lasts[idx][n * seq_len:n * seq_len + 1] - _mm(nm[:, hd:], s) + nm[:, :hd]
            if carry_mode:
                states[h] = s_new
            else:
                s_out_ref[seq, h] = s_new

    nw = nw_ref[...]
    for h in range(nh):
        o = _cat(o_rows[h], 0)
        o = o * lax.rsqrt(lane_sum(o * o) * (1.0 / hd) + NORM_EPS) * nw
        o_ref[:, h * hd:(h + 1) * hd] = o * _silu(z_ref[:, h * hd:(h + 1) * hd])

    if carry_mode:
        for h in range(nh):
            s_scr[h] = states[h]

        @pl.when(t_idx == pl.num_programs(1) - 1)
        def _():
            for h in range(nh):
                s_out_ref[h] = states[h]


def _gdn_mixer(qkv, z, ba, conv_w, gparams, norm_w, halo, s_in, *, rows, chunk, seq_len, carry_mode):
    bsz, t_len, _ = qkv.shape
    grid = (bsz, t_len // rows)
    tok = lambda width: pl.BlockSpec((None, rows, width), lambda b, t: (b, t, 0))
    const = lambda shape: pl.BlockSpec(shape, lambda b, t: (0,) * len(shape))
    in_specs = [tok(GDN_QKV), tok(GDN_WIDTH), tok(BA_PAD), const((GDN_CONV, GDN_QKV)), const((2, BA_PAD)),
                const((1, GDN_HEAD_DIM))]
    args = [qkv, z, ba, conv_w, gparams, norm_w]
    state_tail = (GDN_HEADS, GDN_HEAD_DIM, GDN_HEAD_DIM)
    if carry_mode:
        in_specs.append(pl.BlockSpec((None, SUBLANES, GDN_QKV), lambda b, t: (b, 0, 0)))
        args.append(halo)
        state_spec = pl.BlockSpec((None,) + state_tail, lambda b, t: (b, 0, 0, 0))
        scratch = [pltpu.VMEM(state_tail, F32), pltpu.VMEM((SUBLANES, GDN_QKV), F32)]
    else:
        in_specs += [tok(GDN_QKV)] * (GDN_CONV - 1)
        args += list(halo)
        state_spec = pl.BlockSpec((rows // seq_len,) + state_tail, lambda b, t: (t, 0, 0, 0))
        scratch = []
    in_specs.append(state_spec)
    args.append(s_in)
    return pl.pallas_call(
        functools.partial(_gdn_kernel, rows=rows, chunk=chunk, seq_len=seq_len, carry_mode=carry_mode),
        grid=grid,
        in_specs=in_specs,
        out_specs=[tok(GDN_WIDTH), state_spec],
        out_shape=[jax.ShapeDtypeStruct((bsz, t_len, GDN_WIDTH), F32),
                   jax.ShapeDtypeStruct(s_in.shape, F32)],
        scratch_shapes=scratch,
        compiler_params=pltpu.CompilerParams(dimension_semantics=("arbitrary",) * 2,
                                             vmem_limit_bytes=VMEM_LIMIT_BYTES),
        name="gdn_mixer",
    )(*args)


def _rwkv_kernel(*refs, rows, chunk, seq_len, carry_mode):
    (rkv_ref, lora_ref, mur_ref, mul_ref, par_ref, w2_ref, a2_ref, g2_ref, hr_ref, hl_ref, s_in_ref,
     o_ref, s_out_ref) = refs[:13]
    scratch = refs[13:]
    hd, npair, c, wd_all = RWKV_HEAD_DIM, RWKV_PAIRS, chunk, RWKV_WIDTH
    pw = 2 * hd
    n_chunks = rows // chunk
    n_seg = chunk // seq_len
    t_idx = pl.program_id(1)

    if carry_mode:
        s_scr, cr_scr, cl_scr = scratch

        @pl.when(t_idx == 0)
        def _():
            s_scr[...] = s_in_ref[...]
            cr_scr[...] = hr_ref[...]
            cl_scr[...] = hl_ref[...]

    def token_shift(x_ref, mu_ref, h_ref, c_scr):
        x = x_ref[...]
        if carry_mode:
            prev = _shift_rows_carry(x, 1, c_scr[...])
            c_scr[...] = x[rows - SUBLANES:]
        else:
            prev = _shift_rows_fix(x, 1, h_ref[...], seq_len)
        return x + mu_ref[...] * (prev - x)

    xs = token_shift(rkv_ref, mur_ref, hr_ref, cr_scr if carry_mode else None)
    lora = token_shift(lora_ref, mul_ref, hl_ref, cl_scr if carry_mode else None)
    r, kr, vr = xs[:, :wd_all], xs[:, wd_all:2 * wd_all], xs[:, 2 * wd_all:]
    par = par_ref[...]
    w0, a0, k_k, k_a, r_k, ln_w, ln_b = (par[n:n + 1] for n in range(7))
    wd = lora[:, :DECAY_LORA]
    ad = lora[:, DECAY_LORA:DECAY_LORA + AAA_LORA]
    gd = lora[:, DECAY_LORA + AAA_LORA:]
    lw = -(DECAY_SCALE * _sigmoid(w0 + _mm(jnp.tanh(wd), w2_ref[...])))
    asig = _sigmoid(a0 + _mm(ad, a2_ref[...]))
    g = _mm(_sigmoid(gd), g2_ref[...])

    li = _iota((pw, pw), 0)
    lj = _iota((pw, pw), 1)
    pair_bd = (li >= hd) == (lj >= hd)
    head_ones = pair_bd.astype(BF16)

    def head_sum(a):
        return _cat([_mm(a[:, p * pw:(p + 1) * pw], head_ones) for p in range(a.shape[1] // pw)], 1)

    kk = kr * k_k
    kk = kk * lax.rsqrt(head_sum(kk * kk) + L2_EPS)
    kr = kr * (1.0 + (asig - 1.0) * k_a)
    a_vec = -kk
    b_vec = kk * asig

    i1 = _iota((c, c), 0)
    j1 = _iota((c, c), 1)
    same1 = (i1 ^ j1) < seq_len
    incl1_bf = (same1 & (j1 <= i1)).astype(BF16)
    same1_bf = same1.astype(BF16)
    ip = _iota((c, 2 * c), 0)
    jp_full = _iota((c, 2 * c), 1)
    jp = jp_full & (c - 1)
    head1_cols = jp_full >= c
    same_p = (ip ^ jp) < seq_len
    incl_p = same_p & (jp <= ip)
    strict_p = same_p & (jp < ip)

    def stack_heads(a):
        head1 = (_iota(a.shape, 1) & hd) != 0
        return jnp.concatenate([jnp.where(head1, 0.0, a), jnp.where(head1, a, 0.0)], axis=0)

    def block_diag(m):
        return jnp.concatenate([jnp.where(head1_cols, 0.0, m), jnp.where(head1_cols, m, 0.0)], axis=0)

    pair_mul = lambda a, b: _mm_acc(a, block_diag(b))

    probs = []
    for ci in range(n_chunks):
        rs = slice(ci * c, (ci + 1) * c)
        lwc = lw[rs]
        g_incl = _mm_mask_lhs(incl1_bf, lwc)
        g_mid = 0.5 * _mm_mask_lhs(same1_bf, lwc)
        e_mid = jnp.exp(g_mid)
        inv_n = jnp.exp(g_mid - g_incl)
        r_n = r[rs] * jnp.exp(g_incl - g_mid)
        a_n = a_vec[rs] * jnp.exp(g_incl - lwc - g_mid)
        to_end = inv_n * e_mid
        full = dict(r_n=r_n, a_n=a_n, b_n=b_vec[rs] * inv_n, k_n=kr[rs] * inv_n, r_t=r_n * e_mid,
                    a_t=a_n * e_mid, b_end=b_vec[rs] * to_end, k_end=kr[rs] * to_end,
                    gam=e_mid * e_mid, v=vr[rs])
        for p in range(npair):
            prob = {name: val[:, p * pw:(p + 1) * pw] for name, val in full.items()}
            prob.update(ci=ci, p=p)
            probs.append(prob)

    ars = [jnp.concatenate([p["a_n"], p["r_n"]], axis=0) for p in probs]
    m_bks = [_mm(ar, jnp.concatenate([stack_heads(p["b_n"]), stack_heads(p["k_n"])], axis=0), NT)
             for ar, p in zip(ars, probs)]
    a_abs = [jnp.where(strict_p, m[:c, :2 * c], 0.0) for m in m_bks]
    a_rbs = [jnp.where(incl_p, m[c:, :2 * c], 0.0) for m in m_bks]
    a_aks = [jnp.where(strict_p, m[:c, 2 * c:], 0.0) for m in m_bks]
    a_rks = [jnp.where(incl_p, m[c:, 2 * c:], 0.0) for m in m_bks]
    t_invs = _unit_lower_inverses(a_abs, ip, jp, seq_len, pair_mul)
    akvs = [_mm(jnp.concatenate([a_ak, a_rk], axis=0), stack_heads(p["v"]))
            for a_ak, a_rk, p in zip(a_aks, a_rks, probs)]
    uws = [_mm_acc(t, stack_heads(jnp.concatenate([akv[:c], p["a_t"]], axis=1)))
           for t, akv, p in zip(t_invs, akvs, probs)]
    rbs = [_mm(a_rb, stack_heads(uw)) for a_rb, uw in zip(a_rbs, uws)]
    y_frees = [rb[:, :pw] + akv[c:] for rb, akv in zip(rbs, akvs)]
    r_effs = [p["r_t"] + rb[:, pw:] for p, rb in zip(probs, rbs)]
    m_corrs, n_adds = [], []
    for p, uw in zip(probs, uws):
        ms, ns = [], []
        for n in range(n_seg):
            sr = slice(n * seq_len, (n + 1) * seq_len)
            zero = jnp.zeros((seq_len, pw), F32)
            lhs = jnp.concatenate([jnp.concatenate([uw[sr, pw:], uw[sr, :pw]], axis=1),
                                   jnp.concatenate([zero, p["v"][sr]], axis=1)], axis=0)
            rhs = jnp.concatenate([p["b_end"][sr], p["k_end"][sr]], axis=0)
            mn = _mm_acc(lhs, rhs, TN)
            ms.append(jnp.where(pair_bd, mn[:pw], 0.0))
            ns.append(jnp.where(pair_bd, mn[pw:], 0.0))
        m_corrs.append(ms)
        n_adds.append(ns)

    states = [s_scr[p] for p in range(npair)] if carry_mode else None
    y_rows = [[] for _ in range(npair)]
    for idx, prob in enumerate(probs):
        p, ci = prob["p"], prob["ci"]
        for n in range(n_seg):
            sr = slice(n * seq_len, (n + 1) * seq_len)
            seq = ci * n_seg + n
            s = states[p] if carry_mode else s_in_ref[seq, p]
            y_rows[p].append(_mm_acc(r_effs[idx][sr], s, NT) + y_frees[idx][sr])
            gam = prob["gam"][n * seq_len:n * seq_len + 1]
            s_new = s * gam + _mm_acc(s, m_corrs[idx][n]) + n_adds[idx][n]
            if carry_mode:
                states[p] = s_new
            else:
                s_out_ref[seq, p] = s_new

    y = _cat([_cat(y_rows[p], 0) for p in range(npair)], 1)
    y_c = y - head_sum(y) * (1.0 / hd)
    y_n = y_c * lax.rsqrt(head_sum(y_c * y_c) * (1.0 / hd) + RWKV_LN_EPS)
    y_n = y_n * ln_w + ln_b
    bonus = head_sum(r * kr * r_k) * vr
    o_ref[...] = (y_n + bonus) * g

    if carry_mode:
        for p in range(npair):
            s_scr[p] = states[p]

        @pl.when(t_idx == pl.num_programs(1) - 1)
        def _():
            for p in range(npair):
                s_out_ref[p] = states[p]


def _rwkv_mixer(rkv, lora, mu_rkv, mu_lora, params, w2, a2, g2, halo_rkv, halo_lora, s_in, *,
                rows, chunk, seq_len, carry_mode):
    bsz, t_len, _ = rkv.shape
    grid = (bsz, t_len // rows)
    tok = lambda width: pl.BlockSpec((None, rows, width), lambda b, t: (b, t, 0))
    const = lambda shape: pl.BlockSpec(shape, lambda b, t: (0,) * len(shape))
    in_specs = [tok(3 * RWKV_WIDTH), tok(LORA_PAD), const((1, 3 * RWKV_WIDTH)), const((1, LORA_PAD)),
                const((SUBLANES, RWKV_WIDTH)), const((DECAY_LORA, RWKV_WIDTH)), const((AAA_LORA, RWKV_WIDTH)),
                const((LORA_PAD - DECAY_LORA - AAA_LORA, RWKV_WIDTH))]
    state_tail = (RWKV_PAIRS, LANES, LANES)
    if carry_mode:
        in_specs += [pl.BlockSpec((None, SUBLANES, 3 * RWKV_WIDTH), lambda b, t: (b, 0, 0)),
                     pl.BlockSpec((None, SUBLANES, LORA_PAD), lambda b, t: (b, 0, 0))]
        state_spec = pl.BlockSpec((None,) + state_tail, lambda b, t: (b, 0, 0, 0))
        scratch = [pltpu.VMEM(state_tail, F32), pltpu.VMEM((SUBLANES, 3 * RWKV_WIDTH), F32),
                   pltpu.VMEM((SUBLANES, LORA_PAD), F32)]
    else:
        in_specs += [tok(3 * RWKV_WIDTH), tok(LORA_PAD)]
        state_spec = pl.BlockSpec((rows // seq_len,) + state_tail, lambda b, t: (t, 0, 0, 0))
        scratch = []
    in_specs.append(state_spec)
    args = [rkv, lora, mu_rkv, mu_lora, params, w2, a2, g2, halo_rkv, halo_lora, s_in]
    return pl.pallas_call(
        functools.partial(_rwkv_kernel, rows=rows, chunk=chunk, seq_len=seq_len, carry_mode=carry_mode),
        grid=grid,
        in_specs=in_specs,
        out_specs=[tok(RWKV_WIDTH), state_spec],
        out_shape=[jax.ShapeDtypeStruct((bsz, t_len, RWKV_WIDTH), F32),
                   jax.ShapeDtypeStruct(s_in.shape, F32)],
        scratch_shapes=scratch,
        compiler_params=pltpu.CompilerParams(dimension_semantics=("arbitrary",) * 2,
                                             vmem_limit_bytes=VMEM_LIMIT_BYTES),
        name="rwkv_mixer",
    )(*args)


def _ffn_kernel(*refs, rows, seq_len, blocks_per_seq, carry_mode):
    x_ref, og_ref, orw_ref, wo_ref, nf_ref, wu_ref, cw_ref, wd_ref, nfin_ref = refs[:9]
    if carry_mode:
        h_ref, y_ref, tail_ref, carry_scr = refs[9:]
    else:
        f1_ref, f2_ref, y_ref, tail_ref = refs[9:]
    blk = pl.program_id(0)

    if carry_mode:
        @pl.when(blk % blocks_per_seq == 0)
        def _():
            carry_scr[...] = h_ref[...]

    o = (jnp.dot(og_ref[...].astype(BF16), wo_ref[:GDN_WIDTH], preferred_element_type=F32)
         + jnp.dot(orw_ref[...].astype(BF16), wo_ref[GDN_WIDTH:], preferred_element_type=F32))
    x1 = x_ref[...] + o
    h2 = _rms_norm(x1, nf_ref[...]).astype(BF16)
    cw = cw_ref[...]
    x2 = x1
    off = 0
    for width in FFN_COL_CHUNKS:
        cols = slice(off, off + width)
        gate = jnp.dot(h2, wu_ref[:, cols], preferred_element_type=F32)
        val = jnp.dot(h2, wu_ref[:, FFN_DIM + off:FFN_DIM + off + width], preferred_element_type=F32)
        conv = gate * cw[FFN_CONV - 1:FFN_CONV, cols]
        for s in range(1, FFN_CONV):
            if carry_mode:
                gs = _shift_rows_carry(gate, s, carry_scr[:, cols])
            else:
                gs = _shift_rows_fix(gate, s, (f1_ref, f2_ref)[s - 1][:, cols], seq_len)
            conv = conv + gs * cw[FFN_CONV - 1 - s:FFN_CONV - s, cols]
        if carry_mode:
            carry_scr[:, cols] = gate[rows - SUBLANES:]
            tail_ref[:, cols] = gate[rows - SUBLANES:]
        else:
            tail_ref[:, cols] = gate
        act = (_silu(conv) * val).astype(BF16)
        x2 = x2 + jnp.dot(act, wd_ref[cols, :], preferred_element_type=F32)
        off += width
    y_ref[...] = _rms_norm(x2, nfin_ref[...])


def _ffn(x2d, o_gdn, o_rwkv, w_out, norm_ffn, w_up, conv_w, w_down, norm_final, halo, *,
         rows, seq_len, carry_mode):
    n_rows = x2d.shape[0]
    n_blocks = n_rows // rows
    row_spec = lambda w: pl.BlockSpec((rows, w), lambda i: (i, 0))
    whole = lambda shape: pl.BlockSpec(shape, lambda i: (0,) * len(shape), pipeline_mode=pl.Buffered(1))
    in_specs = [row_spec(D_MODEL), row_spec(GDN_WIDTH), row_spec(RWKV_WIDTH),
                whole((D_MODEL, D_MODEL)), whole((1, D_MODEL)), whole((D_MODEL, 2 * FFN_DIM)),
                whole((FFN_CONV, FFN_DIM)), whole((FFN_DIM, D_MODEL)), whole((1, D_MODEL))]
    args = [x2d, o_gdn, o_rwkv, w_out, norm_ffn, w_up, conv_w, w_down, norm_final]
    if carry_mode:
        blocks_per_seq = seq_len // rows
        in_specs.append(pl.BlockSpec((None, SUBLANES, FFN_DIM), lambda i: (i // blocks_per_seq, 0, 0)))
        args.append(halo)
        tail_spec = pl.BlockSpec((None, SUBLANES, FFN_DIM), lambda i: (i, 0, 0))
        tail_shape = jax.ShapeDtypeStruct((n_blocks, SUBLANES, FFN_DIM), F32)
        scratch = [pltpu.VMEM((SUBLANES, FFN_DIM), F32)]
    else:
        blocks_per_seq = 1
        in_specs += [row_spec(FFN_DIM), row_spec(FFN_DIM)]
        args += [halo[0], halo[1]]
        tail_spec = row_spec(FFN_DIM)
        tail_shape = jax.ShapeDtypeStruct((n_rows, FFN_DIM), F32)
        scratch = []
    return pl.pallas_call(
        functools.partial(_ffn_kernel, rows=rows, seq_len=seq_len, blocks_per_seq=blocks_per_seq,
                          carry_mode=carry_mode),
        grid=(n_blocks,),
        in_specs=in_specs,
        out_specs=[row_spec(D_MODEL), tail_spec],
        out_shape=[jax.ShapeDtypeStruct((n_rows, D_MODEL), F32), tail_shape],
        scratch_shapes=scratch,
        compiler_params=pltpu.CompilerParams(dimension_semantics=("arbitrary",),
                                             vmem_limit_bytes=VMEM_LIMIT_BYTES),
        name="ffn",
    )(*args)


def _pad_cols(a, width):
    return jnp.pad(a, ((0, 0), (0, width - a.shape[1])))


def _regroup_in_weights(w_in):
    g_end = GDN_QKV + GDN_WIDTH
    ba = w_in[:, g_end:g_end + 2 * GDN_HEADS]
    rw = w_in[:, g_end + 2 * GDN_HEADS:]
    return jnp.concatenate([w_in[:, :g_end], rw[:, :3 * RWKV_WIDTH],
                            _pad_cols(rw[:, 3 * RWKV_WIDTH:], LORA_PAD), _pad_cols(ba, BA_PAD)],
                           axis=1).astype(BF16)


def _tail_tile(state_rows):
    return jnp.pad(state_rows, ((0, 0), (SUBLANES - state_rows.shape[1], 0), (0, 0)))


def _fix_rows(state_rows, s, seq_len):
    n = state_rows.shape[1]
    first = state_rows[:, n - s:, :]
    return jnp.pad(first, ((0, 0), (0, seq_len - s), (0, 0))).reshape(-1, state_rows.shape[2])


def _pair_states(s):
    bsz = s.shape[0]
    sp = s.reshape(bsz, RWKV_PAIRS, 2, RWKV_HEAD_DIM, RWKV_HEAD_DIM)
    zero = jnp.zeros_like(sp[:, :, 0])
    top = jnp.concatenate([sp[:, :, 0], zero], axis=-1)
    bot = jnp.concatenate([zero, sp[:, :, 1]], axis=-1)
    return jnp.concatenate([top, bot], axis=-2)


def _unpair_states(sp):
    hd = RWKV_HEAD_DIM
    bsz = sp.shape[0]
    return jnp.stack([sp[:, :, :hd, :hd], sp[:, :, hd:, hd:]], axis=2).reshape(bsz, RWKV_HEADS, hd, hd)


def _layer(x, st_gdn_conv, st_gdn, st_shift, st_rwkv, st_ffn, wts, *, carry_mode):
    bsz, t_len, _ = x.shape
    n_rows = bsz * t_len
    x2d = x.reshape(n_rows, D_MODEL)
    qkv, z, rkv, lora, ba = _inproj(x2d, wts["norm_mix"], wts["w_in"])

    shift_rkv = st_shift[:, :, :3 * RWKV_WIDTH]
    shift_lora = _pad_cols(st_shift[:, 0, 3 * RWKV_WIDTH:], LORA_PAD)[:, None, :]
    if carry_mode:
        shape3 = lambda a: a.reshape(bsz, t_len, a.shape[-1])
        o_gdn, s_gdn = _gdn_mixer(shape3(qkv), shape3(z), shape3(ba), wts["gdn_conv_w"], wts["gparams"],
                                  wts["gdn_norm"], _tail_tile(st_gdn_conv), st_gdn,
                                  rows=MIXER_ROWS, chunk=GDN_CHUNK, seq_len=GDN_CHUNK, carry_mode=True)
        o_rwkv, s_rwkv = _rwkv_mixer(shape3(rkv), shape3(lora), wts["mu_rkv"], wts["mu_lora"], wts["rparams"],
                                     wts["w2"], wts["a2"], wts["g2"], _tail_tile(shift_rkv),
                                     _tail_tile(shift_lora), _pair_states(st_rwkv),
                                     rows=MIXER_ROWS, chunk=RWKV_CHUNK, seq_len=RWKV_CHUNK, carry_mode=True)
        ffn_halo = _tail_tile(st_ffn)
        ffn_rows, ffn_seq = ROW_BLOCK, t_len
    else:
        shape3 = lambda a: a.reshape(1, n_rows, a.shape[-1])
        fix = lambda st, s: _fix_rows(st, s, t_len)[None]
        o_gdn, s_gdn = _gdn_mixer(shape3(qkv), shape3(z), shape3(ba), wts["gdn_conv_w"], wts["gparams"],
                                  wts["gdn_norm"], [fix(st_gdn_conv, s) for s in range(1, GDN_CONV)], st_gdn,
                                  rows=n_rows, chunk=GDN_CHUNK, seq_len=t_len, carry_mode=False)
        o_rwkv, s_rwkv = _rwkv_mixer(shape3(rkv), shape3(lora), wts["mu_rkv"], wts["mu_lora"], wts["rparams"],
                                     wts["w2"], wts["a2"], wts["g2"], fix(shift_rkv, 1), fix(shift_lora, 1),
                                     _pair_states(st_rwkv),
                                     rows=n_rows, chunk=RWKV_CHUNK, seq_len=t_len, carry_mode=False)
        ffn_halo = [_fix_rows(st_ffn, s, t_len) for s in range(1, FFN_CONV)]
        ffn_rows, ffn_seq = n_rows, t_len
    y2d, gate_tail = _ffn(x2d, o_gdn.reshape(n_rows, GDN_WIDTH), o_rwkv.reshape(n_rows, RWKV_WIDTH),
                          wts["w_out"], wts["norm_ffn"], wts["w_up"], wts["ffn_conv_w"], wts["w_down"],
                          wts["norm_final"], ffn_halo, rows=ffn_rows, seq_len=ffn_seq, carry_mode=carry_mode)

    qkv3 = qkv.reshape(bsz, t_len, GDN_QKV)
    gdn_conv_new = qkv3[:, t_len - (GDN_CONV - 1):, :]
    shift_new = jnp.concatenate([rkv.reshape(bsz, t_len, -1)[:, t_len - 1:, :],
                                 lora.reshape(bsz, t_len, -1)[:, t_len - 1:, :RWKV_PROJ - 3 * RWKV_WIDTH]], axis=-1)
    if carry_mode:
        blocks_per_seq = t_len // ROW_BLOCK
        tails = gate_tail.reshape(bsz, blocks_per_seq, SUBLANES, FFN_DIM)[:, -1]
        ffn_new = tails[:, SUBLANES - (FFN_CONV - 1):, :]
    else:
        ffn_new = gate_tail.reshape(bsz, t_len, FFN_DIM)[:, t_len - (FFN_CONV - 1):, :]
    return (y2d.reshape(bsz, t_len, D_MODEL), gdn_conv_new, s_gdn, shift_new, _unpair_states(s_rwkv), ffn_new)


def kernel(x_prompt, x_sample, state_gdn_conv, state_gdn, state_rwkv_shift, state_rwkv, state_ffn_conv,
           norm_mix, w_in, gdn_conv_w, gdn_a_log, gdn_dt_bias, gdn_norm,
           rwkv_mu, rwkv_w0, rwkv_w2, rwkv_a0, rwkv_a2, rwkv_g2, rwkv_k_k, rwkv_k_a, rwkv_r_k,
           rwkv_ln_w, rwkv_ln_b, w_out, norm_ffn, w_up, ffn_conv_w, w_down, norm_final):
    layer = 0
    gparams = jnp.zeros((2, BA_PAD), F32)
    gparams = gparams.at[0, GDN_HEADS:2 * GDN_HEADS].set(gdn_a_log[layer])
    gparams = gparams.at[1, GDN_HEADS:2 * GDN_HEADS].set(gdn_dt_bias[layer])
    mu = rwkv_mu[layer][None, :]
    rparams = jnp.stack([rwkv_w0[layer], rwkv_a0[layer], rwkv_k_k[layer], rwkv_k_a[layer],
                         rwkv_r_k[layer].reshape(-1), rwkv_ln_w[layer], rwkv_ln_b[layer],
                         jnp.zeros((RWKV_WIDTH,), F32)])
    gate_rows = LORA_PAD - DECAY_LORA - AAA_LORA
    wts = {
        "norm_mix": norm_mix[layer][None, :],
        "w_in": _regroup_in_weights(w_in[layer]),
        "gdn_conv_w": gdn_conv_w[layer],
        "gparams": gparams,
        "gdn_norm": gdn_norm[layer][None, :],
        "mu_rkv": mu[:, :3 * RWKV_WIDTH],
        "mu_lora": _pad_cols(mu[:, 3 * RWKV_WIDTH:], LORA_PAD),
        "rparams": rparams,
        "w2": rwkv_w2[layer].astype(BF16),
        "a2": rwkv_a2[layer].astype(BF16),
        "g2": jnp.pad(rwkv_g2[layer], ((0, gate_rows - GATE_LORA), (0, 0))).astype(BF16),
        "w_out": w_out[layer].astype(BF16),
        "norm_ffn": norm_ffn[layer][None, :],
        "w_up": w_up[layer].astype(BF16),
        "ffn_conv_w": ffn_conv_w[layer],
        "w_down": w_down[layer].astype(BF16),
        "norm_final": norm_final[None, :],
    }
    n_prompt = x_prompt.shape[0]
    zeros = lambda c: jnp.zeros((n_prompt,) + c.shape[2:], c.dtype)
    outs_p = _layer(x_prompt, zeros(state_gdn_conv), zeros(state_gdn), zeros(state_rwkv_shift),
                    zeros(state_rwkv), zeros(state_ffn_conv), wts, carry_mode=True)
    outs_s = _layer(x_sample, state_gdn_conv[layer], state_gdn[layer], state_rwkv_shift[layer],
                    state_rwkv[layer], state_ffn_conv[layer], wts, carry_mode=False)
    y_p, y_s = outs_p[0], outs_s[0]
    states = []
    for n in range(1, 6):
        states += [outs_p[n][None], outs_s[n][None]]
    return (y_p, y_s, *states)
```

```python
import functools
import math

import jax
import jax.numpy as jnp
from jax import lax
from jax.experimental import pallas as pl
from jax.experimental.pallas import tpu as pltpu

F32 = jnp.float32
BF16 = jnp.bfloat16

D_MODEL = 1024
GDN_HEAD_DIM = 128
GDN_HEADS = 4
GDN_WIDTH = GDN_HEADS * GDN_HEAD_DIM
GDN_QKV = 3 * GDN_WIDTH
GDN_CONV = 4
RWKV_HEAD_DIM = 64
RWKV_HEADS = 8
RWKV_WIDTH = RWKV_HEADS * RWKV_HEAD_DIM
RWKV_PAIRS = RWKV_HEADS // 2
DECAY_LORA = 64
AAA_LORA = 64
GATE_LORA = 160
RWKV_PROJ = 3 * RWKV_WIDTH + DECAY_LORA + AAA_LORA + GATE_LORA
FFN_DIM = 2816
FFN_CONV = 3
NORM_EPS = 1e-6
L2_EPS = 1e-6
RWKV_LN_EPS = 64e-5
DECAY_SCALE = math.exp(-0.5)

LANES = 128
SUBLANES = 8
VMEM_LIMIT_BYTES = 56 * 1024 * 1024

LORA_PAD = 384
BA_PAD = LANES
P_PAD = GDN_QKV + GDN_WIDTH + 3 * RWKV_WIDTH + LORA_PAD + BA_PAD
OFF_QKV = 0
OFF_Z = OFF_QKV + GDN_QKV
OFF_RKV = OFF_Z + GDN_WIDTH
OFF_LORA = OFF_RKV + 3 * RWKV_WIDTH
OFF_BA = OFF_LORA + LORA_PAD

INPROJ_ROWS = 1024
ROW_BLOCK = 256
MIXER_ROWS = 1024
GDN_CHUNK = 128
RWKV_CHUNK = 64

NN = (((1,), (0,)), ((), ()))
NT = (((1,), (1,)), ((), ()))
TN = (((0,), (0,)), ((), ()))


def _dot(a, b, dims):
    return lax.dot_general(a, b, dims, preferred_element_type=F32)


def _mm(a, b, dims=NN):
    return _dot(a.astype(BF16), b.astype(BF16), dims)


def _split2(a):
    hi = a.astype(BF16)
    return hi, (a - hi.astype(F32)).astype(BF16)


def _mm_mask_lhs(mask_bf16, x, dims=NN):
    hi, lo = _split2(x)
    return _dot(mask_bf16, hi, dims) + _dot(mask_bf16, lo, dims)


def _mm_mask_rhs(x, mask_bf16, dims=NN):
    hi, lo = _split2(x)
    return _dot(hi, mask_bf16, dims) + _dot(lo, mask_bf16, dims)


_mm_acc = _mm


def _iota(shape, dim):
    return lax.broadcasted_iota(jnp.int32, shape, dim)


def _sigmoid(x):
    return 0.5 + 0.5 * jnp.tanh(0.5 * x)


def _silu(x):
    h = 0.5 * x
    return h + h * jnp.tanh(h)


def _softplus(x):
    return jnp.maximum(x, 0.0) + jnp.log1p(jnp.exp(-jnp.abs(x)))


def _rms_norm(x, w):
    return x * lax.rsqrt(jnp.mean(x * x, axis=-1, keepdims=True) + NORM_EPS) * w


def _cat(parts, axis):
    return parts[0] if len(parts) == 1 else jnp.concatenate(parts, axis=axis)


def _shift_rows_carry(x, s, carry):
    xr = pltpu.roll(x, s, 0)
    cr = pltpu.roll(carry, s, 0)
    row = _iota((SUBLANES, x.shape[1]), 0)
    head = jnp.where(row < s, cr, xr[:SUBLANES])
    return jnp.concatenate([head, xr[SUBLANES:]], axis=0)


def _shift_rows_fix(x, s, fix, seq_len):
    xr = pltpu.roll(x, s, 0)
    row = _iota(x.shape, 0)
    return jnp.where((row & (seq_len - 1)) < s, fix, xr)


def _unit_lower_inverses(ps, ti, tj, seq_len, mul):
    def joins(s):
        return ((ti ^ tj) < 2 * s) & ((ti & s) != 0) & ((tj & s) == 0)

    eye = jnp.where(ti == tj, 1.0, 0.0)
    ts = [eye + jnp.where(joins(1), p, 0.0) for p in ps]
    s = 2
    while s < seq_len:
        mask = joins(s)
        tbs = [mul(t, jnp.where(mask, p, 0.0)) for t, p in zip(ts, ps)]
        ts = [t + mul(tb, t) for tb, t in zip(tbs, ts)]
        s *= 2
    return ts


def _unit_lower_inverses_halved(ps):
    c = ps[0].shape[0]
    half = c // 2
    ti = _iota((half, c), 0)
    lane = _iota((half, c), 1)
    right = lane >= half
    zeros = jnp.zeros((half, c), F32)

    def block_diag(m):
        return jnp.concatenate([jnp.where(right, 0.0, m), jnp.where(right, m, 0.0)], axis=0)

    diag = [jnp.where(right, p[half:], p[:half]) for p in ps]
    t_ab = _unit_lower_inverses(diag, ti, lane & (half - 1), half, lambda a, b: _mm_acc(a, block_diag(b)))
    ys = [_mm_acc(jnp.where(right, 0.0, p[half:]), jnp.concatenate([jnp.where(right, 0.0, t), zeros], axis=0))
          for p, t in zip(ps, t_ab)]
    x21s = [_mm_acc(t, jnp.concatenate([zeros, y], axis=0)) for t, y in zip(t_ab, ys)]
    return [jnp.concatenate([jnp.where(right, 0.0, t), x21 + jnp.where(right, t, 0.0)], axis=0)
            for t, x21 in zip(t_ab, x21s)]


def _inproj_kernel(x_ref, nw_ref, w_ref, qkv_ref, z_ref, rkv_ref, lora_ref, ba_ref):
    h = _rms_norm(x_ref[...], nw_ref[...]).astype(BF16)
    for out_ref, off in ((qkv_ref, OFF_QKV), (z_ref, OFF_Z), (rkv_ref, OFF_RKV), (lora_ref, OFF_LORA),
                         (ba_ref, OFF_BA)):
        width = out_ref.shape[1]
        out_ref[...] = jnp.dot(h, w_ref[:, off:off + width], preferred_element_type=F32)


def _inproj(x2d, norm_w, w_perm):
    n_rows = x2d.shape[0]
    rows = min(INPROJ_ROWS, n_rows)
    widths = (GDN_QKV, GDN_WIDTH, 3 * RWKV_WIDTH, LORA_PAD, BA_PAD)
    return pl.pallas_call(
        _inproj_kernel,
        grid=(n_rows // rows,),
        in_specs=[
            pl.BlockSpec((rows, D_MODEL), lambda i: (i, 0)),
            pl.BlockSpec((1, D_MODEL), lambda i: (0, 0)),
            pl.BlockSpec((D_MODEL, P_PAD), lambda i: (0, 0), pipeline_mode=pl.Buffered(1)),
        ],
        out_specs=[pl.BlockSpec((rows, w), lambda i: (i, 0)) for w in widths],
        out_shape=[jax.ShapeDtypeStruct((n_rows, w), F32) for w in widths],
        compiler_params=pltpu.CompilerParams(dimension_semantics=("arbitrary",),
                                             vmem_limit_bytes=VMEM_LIMIT_BYTES),
        name="inproj",
    )(x2d, norm_w, w_perm)


def _gdn_kernel(*refs, rows, chunk, seq_len, carry_mode):
    qkv_ref, z_ref, ba_ref, cw_ref, gp_ref, nw_ref = refs[:6]
    n_halo = 1 if carry_mode else GDN_CONV - 1
    halo_refs = refs[6:6 + n_halo]
    s_in_ref, o_ref, s_out_ref = refs[6 + n_halo:9 + n_halo]
    scratch = refs[9 + n_halo:]
    nh, hd, c = GDN_HEADS, GDN_HEAD_DIM, chunk
    n_chunks = rows // chunk
    n_seg = chunk // seq_len
    t_idx = pl.program_id(1)

    if carry_mode:
        s_scr, c_scr = scratch

        @pl.when(t_idx == 0)
        def _():
            s_scr[...] = s_in_ref[...]
            c_scr[...] = halo_refs[0][...]

    x = qkv_ref[...]
    w = cw_ref[...]
    acc = x * w[GDN_CONV - 1:GDN_CONV]
    for s in range(1, GDN_CONV):
        if carry_mode:
            xs = _shift_rows_carry(x, s, c_scr[...])
        else:
            xs = _shift_rows_fix(x, s, halo_refs[s - 1][...], seq_len)
        acc = acc + xs * w[GDN_CONV - 1 - s:GDN_CONV - s]
    if carry_mode:
        c_scr[...] = x[rows - SUBLANES:]
    act = _silu(acc)

    ones = jnp.ones((hd, hd), BF16)
    lane_sum = lambda a: _mm(a, ones)
    l2n = lambda a: a * lax.rsqrt(lane_sum(a * a) + L2_EPS)
    q_h = [l2n(act[:, h * hd:(h + 1) * hd]) * (hd ** -0.5) for h in range(nh)]
    k_h = [l2n(act[:, (nh + h) * hd:(nh + h + 1) * hd]) for h in range(nh)]
    v_h = [act[:, (2 * nh + h) * hd:(2 * nh + h + 1) * hd] for h in range(nh)]

    ba = ba_ref[...]
    gp = gp_ref[...]
    beta_all = _sigmoid(ba)
    lg_all = -jnp.exp(gp[0:1]) * _softplus(ba + gp[1:2])
    lane = _iota((c, LANES), 1)
    pick = lambda arr, col: jnp.sum(jnp.where(lane == col, arr, 0.0), axis=-1, keepdims=True)

    ti = _iota((c, c), 0)
    tj = _iota((c, c), 1)
    same = (ti ^ tj) < seq_len
    incl = same & (tj <= ti)
    strict = same & (tj < ti)
    incl_bf = incl.astype(BF16)
    same_bf = same.astype(BF16)
    incl_t_bf = (same & (ti <= tj)).astype(BF16)

    probs = []
    for ci in range(n_chunks):
        r = slice(ci * c, (ci + 1) * c)
        lgc = lg_all[r]
        gb_all = _mm_mask_lhs(incl_bf, lgc)
        gtot_all = _mm_mask_lhs(same_bf, lgc)
        gb_rows = _mm_mask_rhs(lgc, incl_t_bf, TN)
        for h in range(nh):
            probs.append(dict(ci=ci, h=h, q=q_h[h][r], k=k_h[h][r], v=v_h[h][r], beta=pick(beta_all[r], h),
                              gb=pick(gb_all, nh + h), gtot=pick(gtot_all, nh + h),
                              gb_row=gb_rows[nh + h:nh + h + 1, :]))

    decays = [jnp.exp(jnp.where(incl, p["gb"] - p["gb_row"], -jnp.inf)) for p in probs]
    kbetas = [p["k"] * p["beta"] for p in probs]
    kqs = [_mm(jnp.concatenate([kb, p["q"]], axis=0), p["k"], NT) for kb, p in zip(kbetas, probs)]
    neg_ls = [jnp.where(strict, -(kq[:c] * d), 0.0) for kq, d in zip(kqs, decays)]
    attns = [kq[c:] * d for kq, d in zip(kqs, decays)]
    e_gs = [jnp.exp(p["gb"]) for p in probs]
    if seq_len == c:
        t_invs = _unit_lower_inverses_halved(neg_ls)
    else:
        t_invs = _unit_lower_inverses(neg_ls, ti, tj, seq_len, _mm_acc)
    sols = [_mm_acc(t, jnp.concatenate([p["v"] * p["beta"], kb * eg], axis=1))
            for t, p, kb, eg in zip(t_invs, probs, kbetas, e_gs)]
    atts = [_mm(a, sol) for a, sol in zip(attns, sols)]
    q_effs = [p["q"] * eg - att[:, hd:] for p, eg, att in zip(probs, e_gs, atts)]
    k_decs = [p["k"] * jnp.exp(p["gtot"] - p["gb"]) for p in probs]
    g_lasts = [jnp.exp(p["gtot"]) for p in probs]
    nms = [[_mm(kd[n * seq_len:(n + 1) * seq_len], sol[n * seq_len:(n + 1) * seq_len], TN) for n in range(n_seg)]
           for kd, sol in zip(k_decs, sols)]

    states = [s_scr[h] for h in range(nh)] if carry_mode else None
    o_rows = [[] for _ in range(nh)]
    for idx, p in enumerate(probs):
        h, ci = p["h"], p["ci"]
        for n in range(n_seg):
            rs = slice(n * seq_len, (n + 1) * seq_len)
            seq = ci * n_seg + n
            s = states[h] if carry_mode else s_in_ref[seq, h]
            o_rows[h].append(_mm(q_effs[idx][rs], s) + atts[idx][rs, :hd])
            nm = nms[idx][n]
            s_new = s * g_lasts[idx][n * seq_len:n * seq_len + 1] - _mm(nm[:, hd:], s) + nm[:, :hd]
            if carry_mode:
                states[h] = s_new
            else:
                s_out_ref[seq, h] = s_new

    nw = nw_ref[...]
    for h in range(nh):
        o = _cat(o_rows[h], 0)
        o = o * lax.rsqrt(lane_sum(o * o) * (1.0 / hd) + NORM_EPS) * nw
        o_ref[:, h * hd:(h + 1) * hd] = o * _silu(z_ref[:, h * hd:(h + 1) * hd])

    if carry_mode:
        for h in range(nh):
            s_scr[h] = states[h]

        @pl.when(t_idx == pl.num_programs(1) - 1)
        def _():
            for h in range(nh):
                s_out_ref[h] = states[h]


def _gdn_mixer(qkv, z, ba, conv_w, gparams, norm_w, halo, s_in, *, rows, chunk, seq_len, carry_mode):
    bsz, t_len, _ = qkv.shape
    grid = (bsz, t_len // rows)
    tok = lambda width: pl.BlockSpec((None, rows, width), lambda b, t: (b, t, 0))
    const = lambda shape: pl.BlockSpec(shape, lambda b, t: (0,) * len(shape))
    in_specs = [tok(GDN_QKV), tok(GDN_WIDTH), tok(BA_PAD), const((GDN_CONV, GDN_QKV)), const((2, BA_PAD)),
                const((1, GDN_HEAD_DIM))]
    args = [qkv, z, ba, conv_w, gparams, norm_w]
    state_tail = (GDN_HEADS, GDN_HEAD_DIM, GDN_HEAD_DIM)
    if carry_mode:
        in_specs.append(pl.BlockSpec((None, SUBLANES, GDN_QKV), lambda b, t: (b, 0, 0)))
        args.append(halo)
        state_spec = pl.BlockSpec((None,) + state_tail, lambda b, t: (b, 0, 0, 0))
        scratch = [pltpu.VMEM(state_tail, F32), pltpu.VMEM((SUBLANES, GDN_QKV), F32)]
    else:
        in_specs += [tok(GDN_QKV)] * (GDN_CONV - 1)
        args += list(halo)
        state_spec = pl.BlockSpec((rows // seq_len,) + state_tail, lambda b, t: (t, 0, 0, 0))
        scratch = []
    in_specs.append(state_spec)
    args.append(s_in)
    return pl.pallas_call(
        functools.partial(_gdn_kernel, rows=rows, chunk=chunk, seq_len=seq_len, carry_mode=carry_mode),
        grid=grid,
        in_specs=in_specs,
        out_specs=[tok(GDN_WIDTH), state_spec],
        out_shape=[jax.ShapeDtypeStruct((bsz, t_len, GDN_WIDTH), F32),
                   jax.ShapeDtypeStruct(s_in.shape, F32)],
        scratch_shapes=scratch,
        compiler_params=pltpu.CompilerParams(dimension_semantics=("arbitrary",) * 2,
                                             vmem_limit_bytes=VMEM_LIMIT_BYTES),
        name="gdn_mixer",
    )(*args)


def _rwkv_kernel(*refs, rows, chunk, seq_len, carry_mode):
    (rkv_ref, lora_ref, mur_ref, mul_ref, par_ref, w2_ref, a2_ref, g2_ref, hr_ref, hl_ref, s_in_ref,
     o_ref, s_out_ref) = refs[:13]
    scratch = refs[13:]
    hd, npair, c, wd_all = RWKV_HEAD_DIM, RWKV_PAIRS, chunk, RWKV_WIDTH
    pw = 2 * hd
    n_chunks = rows // chunk
    n_seg = chunk // seq_len
    t_idx = pl.program_id(1)

    if carry_mode:
        s_scr, cr_scr, cl_scr = scratch

        @pl.when(t_idx == 0)
        def _():
            s_scr[...] = s_in_ref[...]
            cr_scr[...] = hr_ref[...]
            cl_scr[...] = hl_ref[...]

    def token_shift(x_ref, mu_ref, h_ref, c_scr):
        x = x_ref[...]
        if carry_mode:
            prev = _shift_rows_carry(x, 1, c_scr[...])
            c_scr[...] = x[rows - SUBLANES:]
        else:
            prev = _shift_rows_fix(x, 1, h_ref[...], seq_len)
        return x + mu_ref[...] * (prev - x)

    xs = token_shift(rkv_ref, mur_ref, hr_ref, cr_scr if carry_mode else None)
    lora = token_shift(lora_ref, mul_ref, hl_ref, cl_scr if carry_mode else None)
    r, kr, vr = xs[:, :wd_all], xs[:, wd_all:2 * wd_all], xs[:, 2 * wd_all:]
    par = par_ref[...]
    w0, a0, k_k, k_a, r_k, ln_w, ln_b = (par[n:n + 1] for n in range(7))
    wd = lora[:, :DECAY_LORA]
    ad = lora[:, DECAY_LORA:DECAY_LORA + AAA_LORA]
    gd = lora[:, DECAY_LORA + AAA_LORA:]
    lw = -(DECAY_SCALE * _sigmoid(w0 + _mm(jnp.tanh(wd), w2_ref[...])))
    asig = _sigmoid(a0 + _mm(ad, a2_ref[...]))
    g = _mm(_sigmoid(gd), g2_ref[...])

    li = _iota((pw, pw), 0)
    lj = _iota((pw, pw), 1)
    pair_bd = (li >= hd) == (lj >= hd)
    head_ones = pair_bd.astype(BF16)

    def head_sum(a):
        return _cat([_mm(a[:, p * pw:(p + 1) * pw], head_ones) for p in range(a.shape[1] // pw)], 1)

    kk = kr * k_k
    kk = kk * lax.rsqrt(head_sum(kk * kk) + L2_EPS)
    kr = kr * (1.0 + (asig - 1.0) * k_a)
    a_vec = -kk
    b_vec = kk * asig

    i1 = _iota((c, c), 0)
    j1 = _iota((c, c), 1)
    same1 = (i1 ^ j1) < seq_len
    incl1_bf = (same1 & (j1 <= i1)).astype(BF16)
    same1_bf = same1.astype(BF16)
    ip = _iota((c, 2 * c), 0)
    jp_full = _iota((c, 2 * c), 1)
    jp = jp_full & (c - 1)
    head1_cols = jp_full >= c
    same_p = (ip ^ jp) < seq_len
    incl_p = same_p & (jp <= ip)
    strict_p = same_p & (jp < ip)

    def stack_heads(a):
        head1 = (_iota(a.shape, 1) & hd) != 0
        return jnp.concatenate([jnp.where(head1, 0.0, a), jnp.where(head1, a, 0.0)], axis=0)

    def block_diag(m):
        return jnp.concatenate([jnp.where(head1_cols, 0.0, m), jnp.where(head1_cols, m, 0.0)], axis=0)

    pair_mul = lambda a, b: _mm_acc(a, block_diag(b))

    probs = []
    for ci in range(n_chunks):
        rs = slice(ci * c, (ci + 1) * c)
        lwc = lw[rs]
        g_incl = _mm_mask_lhs(incl1_bf, lwc)
        g_mid = 0.5 * _mm_mask_lhs(same1_bf, lwc)
        e_mid = jnp.exp(g_mid)
        inv_n = jnp.exp(g_mid - g_incl)
        r_n = r[rs] * jnp.exp(g_incl - g_mid)
        a_n = a_vec[rs] * jnp.exp(g_incl - lwc - g_mid)
        to_end = inv_n * e_mid
        full = dict(r_n=r_n, a_n=a_n, b_n=b_vec[rs] * inv_n, k_n=kr[rs] * inv_n, r_t=r_n * e_mid,
                    a_t=a_n * e_mid, b_end=b_vec[rs] * to_end, k_end=kr[rs] * to_end,
                    gam=e_mid * e_mid, v=vr[rs])
        for p in range(npair):
            prob = {name: val[:, p * pw:(p + 1) * pw] for name, val in full.items()}
            prob.update(ci=ci, p=p)
            probs.append(prob)

    ars = [jnp.concatenate([p["a_n"], p["r_n"]], axis=0) for p in probs]
    m_bks = [_mm(ar, jnp.concatenate([stack_heads(p["b_n"]), stack_heads(p["k_n"])], axis=0), NT)
             for ar, p in zip(ars, probs)]
    a_abs = [jnp.where(strict_p, m[:c, :2 * c], 0.0) for m in m_bks]
    a_rbs = [jnp.where(incl_p, m[c:, :2 * c], 0.0) for m in m_bks]
    a_aks = [jnp.where(strict_p, m[:c, 2 * c:], 0.0) for m in m_bks]
    a_rks = [jnp.where(incl_p, m[c:, 2 * c:], 0.0) for m in m_bks]
    t_invs = _unit_lower_inverses(a_abs, ip, jp, seq_len, pair_mul)
    akvs = [_mm(jnp.concatenate([a_ak, a_rk], axis=0), stack_heads(p["v"]))
            for a_ak, a_rk, p in zip(a_aks, a_rks, probs)]
    uws = [_mm_acc(t, stack_heads(jnp.concatenate([akv[:c], p["a_t"]], axis=1)))
           for t, akv, p in zip(t_invs, akvs, probs)]
    rbs = [_mm(a_rb, stack_heads(uw)) for a_rb, uw in zip(a_rbs, uws)]
    y_frees = [rb[:, :pw] + akv[c:] for rb, akv in zip(rbs, akvs)]
    r_effs = [p["r_t"] + rb[:, pw:] for p, rb in zip(probs, rbs)]
    m_corrs, n_adds = [], []
    for p, uw in zip(probs, uws):
        ms, ns = [], []
        for n in range(n_seg):
            sr = slice(n * seq_len, (n + 1) * seq_len)
            zero = jnp.zeros((seq_len, pw), F32)
            lhs = jnp.concatenate([jnp.concatenate([uw[sr, pw:], uw[sr, :pw]], axis=1),
                                   jnp.concatenate([zero, p["v"][sr]], axis=1)], axis=0)
            rhs = jnp.concatenate([p["b_end"][sr], p["k_end"][sr]], axis=0)
            mn = _mm_acc(lhs, rhs, TN)
            ms.append(jnp.where(pair_bd, mn[:pw], 0.0))
            ns.append(jnp.where(pair_bd, mn[pw:], 0.0))
        m_corrs.append(ms)
        n_adds.append(ns)

    states = [s_scr[p] for p in range(npair)] if carry_mode else None
    y_rows = [[] for _ in range(npair)]
    for idx, prob in enumerate(probs):
        p, ci = prob["p"], prob["ci"]
        for n in range(n_seg):
            sr = slice(n * seq_len, (n + 1) * seq_len)
            seq = ci * n_seg + n
            s = states[p] if carry_mode else s_in_ref[seq, p]
            y_rows[p].append(_mm_acc(r_effs[idx][sr], s, NT) + y_frees[idx][sr])
            gam = prob["gam"][n * seq_len:n * seq_len + 1]
            s_new = s * gam + _mm_acc(s, m_corrs[idx][n]) + n_adds[idx][n]
            if carry_mode:
                states[p] = s_new
            else:
                s_out_ref[seq, p] = s_new

    y = _cat([_cat(y_rows[p], 0) for p in range(npair)], 1)
    y_c = y - head_sum(y) * (1.0 / hd)
    y_n = y_c * lax.rsqrt(head_sum(y_c * y_c) * (1.0 / hd) + RWKV_LN_EPS)
    y_n = y_n * ln_w + ln_b
    bonus = head_sum(r * kr * r_k) * vr
    o_ref[...] = (y_n + bonus) * g

    if carry_mode:
        for p in range(npair):
            s_scr[p] = states[p]

        @pl.when(t_idx == pl.num_programs(1) - 1)
        def _():
            for p in range(npair):
                s_out_ref[p] = states[p]


def _rwkv_mixer(rkv, lora, mu_rkv, mu_lora, params, w2, a2, g2, halo_rkv, halo_lora, s_in, *,
                rows, chunk, seq_len, carry_mode):
    bsz, t_len, _ = rkv.shape
    grid = (bsz, t_len // rows)
    tok = lambda width: pl.BlockSpec((None, rows, width), lambda b, t: (b, t, 0))
    const = lambda shape: pl.BlockSpec(shape, lambda b, t: (0,) * len(shape))
    in_specs = [tok(3 * RWKV_WIDTH), tok(LORA_PAD), const((1, 3 * RWKV_WIDTH)), const((1, LORA_PAD)),
                const((SUBLANES, RWKV_WIDTH)), const((DECAY_LORA, RWKV_WIDTH)), const((AAA_LORA, RWKV_WIDTH)),
                const((LORA_PAD - DECAY_LORA - AAA_LORA, RWKV_WIDTH))]
    state_tail = (RWKV_PAIRS, LANES, LANES)
    if carry_mode:
        in_specs += [pl.BlockSpec((None, SUBLANES, 3 * RWKV_WIDTH), lambda b, t: (b, 0, 0)),
                     pl.BlockSpec((None, SUBLANES, LORA_PAD), lambda b, t: (b, 0, 0))]
        state_spec = pl.BlockSpec((None,) + state_tail, lambda b, t: (b, 0, 0, 0))
        scratch = [pltpu.VMEM(state_tail, F32), pltpu.VMEM((SUBLANES, 3 * RWKV_WIDTH), F32),
                   pltpu.VMEM((SUBLANES, LORA_PAD), F32)]
    else:
        in_specs += [tok(3 * RWKV_WIDTH), tok(LORA_PAD)]
        state_spec = pl.BlockSpec((rows // seq_len,) + state_tail, lambda b, t: (t, 0, 0, 0))
        scratch = []
    in_specs.append(state_spec)
    args = [rkv, lora, mu_rkv, mu_lora, params, w2, a2, g2, halo_rkv, halo_lora, s_in]
    return pl.pallas_call(
        functools.partial(_rwkv_kernel, rows=rows, chunk=chunk, seq_len=seq_len, carry_mode=carry_mode),
        grid=grid,
        in_specs=in_specs,
        out_specs=[tok(RWKV_WIDTH), state_spec],
        out_shape=[jax.ShapeDtypeStruct((bsz, t_len, RWKV_WIDTH), F32),
                   jax.ShapeDtypeStruct(s_in.shape, F32)],
        scratch_shapes=scratch,
        compiler_params=pltpu.CompilerParams(dimension_semantics=("arbitrary",) * 2,
                                             vmem_limit_bytes=VMEM_LIMIT_BYTES),
        name="rwkv_mixer",
    )(*args)


def _ffn_kernel(*refs, rows, seq_len, blocks_per_seq, carry_mode):
    x_ref, og_ref, orw_ref, wo_ref, nf_ref, wu_ref, cw_ref, wd_ref, nfin_ref = refs[:9]
    if carry_mode:
        h_ref, y_ref, tail_ref, carry_scr = refs[9:]
    else:
        f1_ref, f2_ref, y_ref, tail_ref = refs[9:]
    blk = pl.program_id(0)

    if carry_mode:
        @pl.when(blk % blocks_per_seq == 0)
        def _():
            carry_scr[...] = h_ref[...]

    o = (jnp.dot(og_ref[...].astype(BF16), wo_ref[:GDN_WIDTH], preferred_element_type=F32)
         + jnp.dot(orw_ref[...].astype(BF16), wo_ref[GDN_WIDTH:], preferred_element_type=F32))
    x1 = x_ref[...] + o
    h2 = _rms_norm(x1, nf_ref[...]).astype(BF16)
    gate = jnp.dot(h2, wu_ref[:, :FFN_DIM], preferred_element_type=F32)
    val = jnp.dot(h2, wu_ref[:, FFN_DIM:], preferred_element_type=F32)
    cw = cw_ref[...]
    conv = gate * cw[FFN_CONV - 1:FFN_CONV]
    for s in range(1, FFN_CONV):
        if carry_mode:
            gs = _shift_rows_carry(gate, s, carry_scr[...])
        else:
            gs = _shift_rows_fix(gate, s, (f1_ref, f2_ref)[s - 1][...], seq_len)
        conv = conv + gs * cw[FFN_CONV - 1 - s:FFN_CONV - s]
    if carry_mode:
        carry_scr[...] = gate[rows - SUBLANES:]
        tail_ref[...] = gate[rows - SUBLANES:]
    else:
        tail_ref[...] = gate
    act = (_silu(conv) * val).astype(BF16)
    x2 = x1 + jnp.dot(act, wd_ref[...], preferred_element_type=F32)
    y_ref[...] = _rms_norm(x2, nfin_ref[...])


def _ffn(x2d, o_gdn, o_rwkv, w_out, norm_ffn, w_up, conv_w, w_down, norm_final, halo, *,
         rows, seq_len, carry_mode):
    n_rows = x2d.shape[0]
    n_blocks = n_rows // rows
    row_spec = lambda w: pl.BlockSpec((rows, w), lambda i: (i, 0))
    whole = lambda shape: pl.BlockSpec(shape, lambda i: (0,) * len(shape), pipeline_mode=pl.Buffered(1))
    in_specs = [row_spec(D_MODEL), row_spec(GDN_WIDTH), row_spec(RWKV_WIDTH),
                whole((D_MODEL, D_MODEL)), whole((1, D_MODEL)), whole((D_MODEL, 2 * FFN_DIM)),
                whole((FFN_CONV, FFN_DIM)), whole((FFN_DIM, D_MODEL)), whole((1, D_MODEL))]
    args = [x2d, o_gdn, o_rwkv, w_out, norm_ffn, w_up, conv_w, w_down, norm_final]
    if carry_mode:
        blocks_per_seq = seq_len // rows
        in_specs.append(pl.BlockSpec((None, SUBLANES, FFN_DIM), lambda i: (i // blocks_per_seq, 0, 0)))
        args.append(halo)
        tail_spec = pl.BlockSpec((None, SUBLANES, FFN_DIM), lambda i: (i, 0, 0))
        tail_shape = jax.ShapeDtypeStruct((n_blocks, SUBLANES, FFN_DIM), F32)
        scratch = [pltpu.VMEM((SUBLANES, FFN_DIM), F32)]
    else:
        blocks_per_seq = 1
        in_specs += [row_spec(FFN_DIM), row_spec(FFN_DIM)]
        args += [halo[0], halo[1]]
        tail_spec = row_spec(FFN_DIM)
        tail_shape = jax.ShapeDtypeStruct((n_rows, FFN_DIM), F32)
        scratch = []
    return pl.pallas_call(
        functools.partial(_ffn_kernel, rows=rows, seq_len=seq_len, blocks_per_seq=blocks_per_seq,
                          carry_mode=carry_mode),
        grid=(n_blocks,),
        in_specs=in_specs,
        out_specs=[row_spec(D_MODEL), tail_spec],
        out_shape=[jax.ShapeDtypeStruct((n_rows, D_MODEL), F32), tail_shape],
        scratch_shapes=scratch,
        compiler_params=pltpu.CompilerParams(dimension_semantics=("arbitrary",),
                                             vmem_limit_bytes=VMEM_LIMIT_BYTES),
        name="ffn",
    )(*args)


def _pad_cols(a, width):
    return jnp.pad(a, ((0, 0), (0, width - a.shape[1])))


def _regroup_in_weights(w_in):
    g_end = GDN_QKV + GDN_WIDTH
    ba = w_in[:, g_end:g_end + 2 * GDN_HEADS]
    rw = w_in[:, g_end + 2 * GDN_HEADS:]
    return jnp.concatenate([w_in[:, :g_end], rw[:, :3 * RWKV_WIDTH],
                            _pad_cols(rw[:, 3 * RWKV_WIDTH:], LORA_PAD), _pad_cols(ba, BA_PAD)],
                           axis=1).astype(BF16)


def _tail_tile(state_rows):
    return jnp.pad(state_rows, ((0, 0), (SUBLANES - state_rows.shape[1], 0), (0, 0)))


def _fix_rows(state_rows, s, seq_len):
    n = state_rows.shape[1]
    first = state_rows[:, n - s:, :]
    return jnp.pad(first, ((0, 0), (0, seq_len - s), (0, 0))).reshape(-1, state_rows.shape[2])


def _pair_states(s):
    bsz = s.shape[0]
    sp = s.reshape(bsz, RWKV_PAIRS, 2, RWKV_HEAD_DIM, RWKV_HEAD_DIM)
    zero = jnp.zeros_like(sp[:, :, 0])
    top = jnp.concatenate([sp[:, :, 0], zero], axis=-1)
    bot = jnp.concatenate([zero, sp[:, :, 1]], axis=-1)
    return jnp.concatenate([top, bot], axis=-2)


def _unpair_states(sp):
    hd = RWKV_HEAD_DIM
    bsz = sp.shape[0]
    return jnp.stack([sp[:, :, :hd, :hd], sp[:, :, hd:, hd:]], axis=2).reshape(bsz, RWKV_HEADS, hd, hd)


def _layer(x, st_gdn_conv, st_gdn, st_shift, st_rwkv, st_ffn, wts, *, carry_mode):
    bsz, t_len, _ = x.shape
    n_rows = bsz * t_len
    x2d = x.reshape(n_rows, D_MODEL)
    qkv, z, rkv, lora, ba = _inproj(x2d, wts["norm_mix"], wts["w_in"])

    shift_rkv = st_shift[:, :, :3 * RWKV_WIDTH]
    shift_lora = _pad_cols(st_shift[:, 0, 3 * RWKV_WIDTH:], LORA_PAD)[:, None, :]
    if carry_mode:
        shape3 = lambda a: a.reshape(bsz, t_len, a.shape[-1])
        o_gdn, s_gdn = _gdn_mixer(shape3(qkv), shape3(z), shape3(ba), wts["gdn_conv_w"], wts["gparams"],
                                  wts["gdn_norm"], _tail_tile(st_gdn_conv), st_gdn,
                                  rows=MIXER_ROWS, chunk=GDN_CHUNK, seq_len=GDN_CHUNK, carry_mode=True)
        o_rwkv, s_rwkv = _rwkv_mixer(shape3(rkv), shape3(lora), wts["mu_rkv"], wts["mu_lora"], wts["rparams"],
                                     wts["w2"], wts["a2"], wts["g2"], _tail_tile(shift_rkv),
                                     _tail_tile(shift_lora), _pair_states(st_rwkv),
                                     rows=MIXER_ROWS, chunk=RWKV_CHUNK, seq_len=RWKV_CHUNK, carry_mode=True)
        ffn_halo = _tail_tile(st_ffn)
        ffn_rows, ffn_seq = ROW_BLOCK, t_len
    else:
        shape3 = lambda a: a.reshape(1, n_rows, a.shape[-1])
        fix = lambda st, s: _fix_rows(st, s, t_len)[None]
        o_gdn, s_gdn = _gdn_mixer(shape3(qkv), shape3(z), shape3(ba), wts["gdn_conv_w"], wts["gparams"],
                                  wts["gdn_norm"], [fix(st_gdn_conv, s) for s in range(1, GDN_CONV)], st_gdn,
                                  rows=n_rows, chunk=GDN_CHUNK, seq_len=t_len, carry_mode=False)
        o_rwkv, s_rwkv = _rwkv_mixer(shape3(rkv), shape3(lora), wts["mu_rkv"], wts["mu_lora"], wts["rparams"],
                                     wts["w2"], wts["a2"], wts["g2"], fix(shift_rkv, 1), fix(shift_lora, 1),
                                     _pair_states(st_rwkv),
                                     rows=n_rows, chunk=RWKV_CHUNK, seq_len=t_len, carry_mode=False)
        ffn_halo = [_fix_rows(st_ffn, s, t_len) for s in range(1, FFN_CONV)]
        ffn_rows, ffn_seq = n_rows, t_len
    y2d, gate_tail = _ffn(x2d, o_gdn.reshape(n_rows, GDN_WIDTH), o_rwkv.reshape(n_rows, RWKV_WIDTH),
                          wts["w_out"], wts["norm_ffn"], wts["w_up"], wts["ffn_conv_w"], wts["w_down"],
                          wts["norm_final"], ffn_halo, rows=ffn_rows, seq_len=ffn_seq, carry_mode=carry_mode)

    qkv3 = qkv.reshape(bsz, t_len, GDN_QKV)
    gdn_conv_new = qkv3[:, t_len - (GDN_CONV - 1):, :]
    shift_new = jnp.concatenate([rkv.reshape(bsz, t_len, -1)[:, t_len - 1:, :],
                                 lora.reshape(bsz, t_len, -1)[:, t_len - 1:, :RWKV_PROJ - 3 * RWKV_WIDTH]], axis=-1)
    if carry_mode:
        blocks_per_seq = t_len // ROW_BLOCK
        tails = gate_tail.reshape(bsz, blocks_per_seq, SUBLANES, FFN_DIM)[:, -1]
        ffn_new = tails[:, SUBLANES - (FFN_CONV - 1):, :]
    else:
        ffn_new = gate_tail.reshape(bsz, t_len, FFN_DIM)[:, t_len - (FFN_CONV - 1):, :]
    return (y2d.reshape(bsz, t_len, D_MODEL), gdn_conv_new, s_gdn, shift_new, _unpair_states(s_rwkv), ffn_new)


def kernel(x_prompt, x_sample, state_gdn_conv, state_gdn, state_rwkv_shift, state_rwkv, state_ffn_conv,
           norm_mix, w_in, gdn_conv_w, gdn_a_log, gdn_dt_bias, gdn_norm,
           rwkv_mu, rwkv_w0, rwkv_w2, rwkv_a0, rwkv_a2, rwkv_g2, rwkv_k_k, rwkv_k_a, rwkv_r_k,
           rwkv_ln_w, rwkv_ln_b, w_out, norm_ffn, w_up, ffn_conv_w, w_down, norm_final):
    layer = 0
    gparams = jnp.zeros((2, BA_PAD), F32)
    gparams = gparams.at[0, GDN_HEADS:2 * GDN_HEADS].set(gdn_a_log[layer])
    gparams = gparams.at[1, GDN_HEADS:2 * GDN_HEADS].set(gdn_dt_bias[layer])
    mu = rwkv_mu[layer][None, :]
    rparams = jnp.stack([rwkv_w0[layer], rwkv_a0[layer], rwkv_k_k[layer], rwkv_k_a[layer],
                         rwkv_r_k[layer].reshape(-1), rwkv_ln_w[layer], rwkv_ln_b[layer],
                         jnp.zeros((RWKV_WIDTH,), F32)])
    gate_rows = LORA_PAD - DECAY_LORA - AAA_LORA
    wts = {
        "norm_mix": norm_mix[layer][None, :],
        "w_in": _regroup_in_weights(w_in[layer]),
        "gdn_conv_w": gdn_conv_w[layer],
        "gparams": gparams,
        "gdn_norm": gdn_norm[layer][None, :],
        "mu_rkv": mu[:, :3 * RWKV_WIDTH],
        "mu_lora": _pad_cols(mu[:, 3 * RWKV_WIDTH:], LORA_PAD),
        "rparams": rparams,
        "w2": rwkv_w2[layer].astype(BF16),
        "a2": rwkv_a2[layer].astype(BF16),
        "g2": jnp.pad(rwkv_g2[layer], ((0, gate_rows - GATE_LORA), (0, 0))).astype(BF16),
        "w_out": w_out[layer].astype(BF16),
        "norm_ffn": norm_ffn[layer][None, :],
        "w_up": w_up[layer].astype(BF16),
        "ffn_conv_w": ffn_conv_w[layer],
        "w_down": w_down[layer].astype(BF16),
        "norm_final": norm_final[None, :],
    }
    n_prompt = x_prompt.shape[0]
    zeros = lambda c: jnp.zeros((n_prompt,) + c.shape[2:], c.dtype)
    outs_p = _layer(x_prompt, zeros(state_gdn_conv), zeros(state_gdn), zeros(state_rwkv_shift),
                    zeros(state_rwkv), zeros(state_ffn_conv), wts, carry_mode=True)
    outs_s = _layer(x_sample, state_gdn_conv[layer], state_gdn[layer], state_rwkv_shift[layer],
                    state_rwkv[layer], state_ffn_conv[layer], wts, carry_mode=False)
    y_p, y_s = outs_p[0], outs_s[0]
    states = []
    for n in range(1, 6):
        states += [outs_p[n][None], outs_s[n][None]]
    return (y_p, y_s, *states)
```

```python
import functools
import math

import jax
import jax.numpy as jnp
from jax import lax
from jax.experimental import pallas as pl
from jax.experimental.pallas import tpu as pltpu

F32 = jnp.float32
BF16 = jnp.bfloat16

D_MODEL = 1024
GDN_HEAD_DIM = 128
GDN_HEADS = 4
GDN_WIDTH = GDN_HEADS * GDN_HEAD_DIM
GDN_QKV = 3 * GDN_WIDTH
GDN_CONV = 4
RWKV_HEAD_DIM = 64
RWKV_HEADS = 8
RWKV_WIDTH = RWKV_HEADS * RWKV_HEAD_DIM
RWKV_PAIRS = RWKV_HEADS // 2
DECAY_LORA = 64
AAA_LORA = 64
GATE_LORA = 160
RWKV_PROJ = 3 * RWKV_WIDTH + DECAY_LORA + AAA_LORA + GATE_LORA
FFN_DIM = 2816
FFN_CONV = 3
NORM_EPS = 1e-6
L2_EPS = 1e-6
RWKV_LN_EPS = 64e-5
DECAY_SCALE = math.exp(-0.5)

LANES = 128
SUBLANES = 8
VMEM_LIMIT_BYTES = 56 * 1024 * 1024

LORA_PAD = 384
BA_PAD = LANES
P_PAD = GDN_QKV + GDN_WIDTH + 3 * RWKV_WIDTH + LORA_PAD + BA_PAD
OFF_QKV = 0
OFF_Z = OFF_QKV + GDN_QKV
OFF_RKV = OFF_Z + GDN_WIDTH
OFF_LORA = OFF_RKV + 3 * RWKV_WIDTH
OFF_BA = OFF_LORA + LORA_PAD

INPROJ_ROWS = 1024
ROW_BLOCK = 512
FFN_COL_CHUNKS = (1536, 1280)
assert sum(FFN_COL_CHUNKS) == FFN_DIM
MIXER_ROWS = 512
GDN_CHUNK = 128
RWKV_CHUNK = 64

NN = (((1,), (0,)), ((), ()))
NT = (((1,), (1,)), ((), ()))
TN = (((0,), (0,)), ((), ()))


def _dot(a, b, dims):
    return lax.dot_general(a, b, dims, preferred_element_type=F32)


def _mm(a, b, dims=NN):
    return _dot(a.astype(BF16), b.astype(BF16), dims)


def _split2(a):
    hi = a.astype(BF16)
    return hi, (a - hi.astype(F32)).astype(BF16)


def _mm_mask_lhs(mask_bf16, x, dims=NN):
    hi, lo = _split2(x)
    return _dot(mask_bf16, hi, dims) + _dot(mask_bf16, lo, dims)


def _mm_mask_rhs(x, mask_bf16, dims=NN):
    hi, lo = _split2(x)
    return _dot(hi, mask_bf16, dims) + _dot(lo, mask_bf16, dims)


_mm_acc = _mm


def _iota(shape, dim):
    return lax.broadcasted_iota(jnp.int32, shape, dim)


def _sigmoid(x):
    return 0.5 + 0.5 * jnp.tanh(0.5 * x)


def _silu(x):
    h = 0.5 * x
    return h + h * jnp.tanh(h)


def _softplus(x):
    return jnp.maximum(x, 0.0) + jnp.log1p(jnp.exp(-jnp.abs(x)))


def _rms_norm(x, w):
    return x * lax.rsqrt(jnp.mean(x * x, axis=-1, keepdims=True) + NORM_EPS) * w


def _cat(parts, axis):
    return parts[0] if len(parts) == 1 else jnp.concatenate(parts, axis=axis)


def _shift_rows_carry(x, s, carry):
    xr = pltpu.roll(x, s, 0)
    cr = pltpu.roll(carry, s, 0)
    row = _iota((SUBLANES, x.shape[1]), 0)
    head = jnp.where(row < s, cr, xr[:SUBLANES])
    return jnp.concatenate([head, xr[SUBLANES:]], axis=0)


def _shift_rows_fix(x, s, fix, seq_len):
    xr = pltpu.roll(x, s, 0)
    row = _iota(x.shape, 0)
    return jnp.where((row & (seq_len - 1)) < s, fix, xr)


def _unit_lower_inverses(ps, ti, tj, seq_len, mul):
    def joins(s):
        return ((ti ^ tj) < 2 * s) & ((ti & s) != 0) & ((tj & s) == 0)

    eye = jnp.where(ti == tj, 1.0, 0.0)
    ts = [eye + jnp.where(joins(1), p, 0.0) for p in ps]
    s = 2
    while s < seq_len:
        mask = joins(s)
        tbs = [mul(t, jnp.where(mask, p, 0.0)) for t, p in zip(ts, ps)]
        ts = [t + mul(tb, t) for tb, t in zip(tbs, ts)]
        s *= 2
    return ts


def _unit_lower_inverses_halved(ps):
    c = ps[0].shape[0]
    half = c // 2
    ti = _iota((half, c), 0)
    lane = _iota((half, c), 1)
    right = lane >= half
    zeros = jnp.zeros((half, c), F32)

    def block_diag(m):
        return jnp.concatenate([jnp.where(right, 0.0, m), jnp.where(right, m, 0.0)], axis=0)

    diag = [jnp.where(right, p[half:], p[:half]) for p in ps]
    t_ab = _unit_lower_inverses(diag, ti, lane & (half - 1), half, lambda a, b: _mm_acc(a, block_diag(b)))
    ys = [_mm_acc(jnp.where(right, 0.0, p[half:]), jnp.concatenate([jnp.where(right, 0.0, t), zeros], axis=0))
          for p, t in zip(ps, t_ab)]
    x21s = [_mm_acc(t, jnp.concatenate([zeros, y], axis=0)) for t, y in zip(t_ab, ys)]
    return [jnp.concatenate([jnp.where(right, 0.0, t), x21 + jnp.where(right, t, 0.0)], axis=0)
            for t, x21 in zip(t_ab, x21s)]


def _inproj_kernel(x_ref, nw_ref, w_ref, qkv_ref, z_ref, rkv_ref, lora_ref, ba_ref):
    h = _rms_norm(x_ref[...], nw_ref[...]).astype(BF16)
    for out_ref, off in ((qkv_ref, OFF_QKV), (z_ref, OFF_Z), (rkv_ref, OFF_RKV), (lora_ref, OFF_LORA),
                         (ba_ref, OFF_BA)):
        width = out_ref.shape[1]
        out_ref[...] = jnp.dot(h, w_ref[:, off:off + width], preferred_element_type=F32)


def _inproj(x2d, norm_w, w_perm):
    n_rows = x2d.shape[0]
    rows = min(INPROJ_ROWS, n_rows)
    widths = (GDN_QKV, GDN_WIDTH, 3 * RWKV_WIDTH, LORA_PAD, BA_PAD)
    return pl.pallas_call(
        _inproj_kernel,
        grid=(n_rows // rows,),
        in_specs=[
            pl.BlockSpec((rows, D_MODEL), lambda i: (i, 0)),
            pl.BlockSpec((1, D_MODEL), lambda i: (0, 0)),
            pl.BlockSpec((D_MODEL, P_PAD), lambda i: (0, 0), pipeline_mode=pl.Buffered(1)),
        ],
        out_specs=[pl.BlockSpec((rows, w), lambda i: (i, 0)) for w in widths],
        out_shape=[jax.ShapeDtypeStruct((n_rows, w), F32) for w in widths],
        compiler_params=pltpu.CompilerParams(dimension_semantics=("arbitrary",),
                                             vmem_limit_bytes=VMEM_LIMIT_BYTES),
        name="inproj",
    )(x2d, norm_w, w_perm)


def _gdn_kernel(*refs, rows, chunk, seq_len, carry_mode):
    qkv_ref, z_ref, ba_ref, cw_ref, gp_ref, nw_ref = refs[:6]
    n_halo = 1 if carry_mode else GDN_CONV - 1
    halo_refs = refs[6:6 + n_halo]
    s_in_ref, o_ref, s_out_ref = refs[6 + n_halo:9 + n_halo]
    scratch = refs[9 + n_halo:]
    nh, hd, c = GDN_HEADS, GDN_HEAD_DIM, chunk
    n_chunks = rows // chunk
    n_seg = chunk // seq_len
    t_idx = pl.program_id(1)

    if carry_mode:
        s_scr, c_scr = scratch

        @pl.when(t_idx == 0)
        def _():
            s_scr[...] = s_in_ref[...]
            c_scr[...] = halo_refs[0][...]

    x = qkv_ref[...]
    w = cw_ref[...]
    acc = x * w[GDN_CONV - 1:GDN_CONV]
    for s in range(1, GDN_CONV):
        if carry_mode:
            xs = _shift_rows_carry(x, s, c_scr[...])
        else:
            xs = _shift_rows_fix(x, s, halo_refs[s - 1][...], seq_len)
        acc = acc + xs * w[GDN_CONV - 1 - s:GDN_CONV - s]
    if carry_mode:
        c_scr[...] = x[rows - SUBLANES:]
    act = _silu(acc)

    ones = jnp.ones((hd, hd), BF16)
    lane_sum = lambda a: _mm(a, ones)
    l2n = lambda a: a * lax.rsqrt(lane_sum(a * a) + L2_EPS)
    q_h = [l2n(act[:, h * hd:(h + 1) * hd]) * (hd ** -0.5) for h in range(nh)]
    k_h = [l2n(act[:, (nh + h) * hd:(nh + h + 1) * hd]) for h in range(nh)]
    v_h = [act[:, (2 * nh + h) * hd:(2 * nh + h + 1) * hd] for h in range(nh)]

    ba = ba_ref[...]
    gp = gp_ref[...]
    beta_all = _sigmoid(ba)
    lg_all = -jnp.exp(gp[0:1]) * _softplus(ba + gp[1:2])
    lane = _iota((c, LANES), 1)
    pick = lambda arr, col: jnp.sum(jnp.where(lane == col, arr, 0.0), axis=-1, keepdims=True)

    ti = _iota((c, c), 0)
    tj = _iota((c, c), 1)
    same = (ti ^ tj) < seq_len
    incl = same & (tj <= ti)
    strict = same & (tj < ti)
    incl_bf = incl.astype(BF16)
    same_bf = same.astype(BF16)
    incl_t_bf = (same & (ti <= tj)).astype(BF16)

    probs = []
    for ci in range(n_chunks):
        r = slice(ci * c, (ci + 1) * c)
        lgc = lg_all[r]
        gb_all = _mm_mask_lhs(incl_bf, lgc)
        gtot_all = _mm_mask_lhs(same_bf, lgc)
        gb_rows = _mm_mask_rhs(lgc, incl_t_bf, TN)
        for h in range(nh):
            probs.append(dict(ci=ci, h=h, q=q_h[h][r], k=k_h[h][r], v=v_h[h][r], beta=pick(beta_all[r], h),
                              gb=pick(gb_all, nh + h), gtot=pick(gtot_all, nh + h),
                              gb_row=gb_rows[nh + h:nh + h + 1, :]))

    decays = [jnp.exp(jnp.where(incl, p["gb"] - p["gb_row"], -jnp.inf)) for p in probs]
    kbetas = [p["k"] * p["beta"] for p in probs]
    kqs = [_mm(jnp.concatenate([kb, p["q"]], axis=0), p["k"], NT) for kb, p in zip(kbetas, probs)]
    neg_ls = [jnp.where(strict, -(kq[:c] * d), 0.0) for kq, d in zip(kqs, decays)]
    attns = [kq[c:] * d for kq, d in zip(kqs, decays)]
    e_gs = [jnp.exp(p["gb"]) for p in probs]
    if seq_len == c:
        t_invs = _unit_lower_inverses_halved(neg_ls)
    else:
        t_invs = _unit_lower_inverses(neg_ls, ti, tj, seq_len, _mm_acc)
    sols = [_mm_acc(t, jnp.concatenate([p["v"] * p["beta"], kb * eg], axis=1))
            for t, p, kb, eg in zip(t_invs, probs, kbetas, e_gs)]
    atts = [_mm(a, sol) for a, sol in zip(attns, sols)]
    q_effs = [p["q"] * eg - att[:, hd:] for p, eg, att in zip(probs, e_gs, atts)]
    k_decs = [p["k"] * jnp.exp(p["gtot"] - p["gb"]) for p in probs]
    g_lasts = [jnp.exp(p["gtot"]) for p in probs]
    nms = [[_mm(kd[n * seq_len:(n + 1) * seq_len], sol[n * seq_len:(n + 1) * seq_len], TN) for n in range(n_seg)]
           for kd, sol in zip(k_decs, sols)]

    states = [s_scr[h] for h in range(nh)] if carry_mode else None
    o_rows = [[] for _ in range(nh)]
    for idx, p in enumerate(probs):
        h, ci = p["h"], p["ci"]
        for n in range(n_seg):
            rs = slice(n * seq_len, (n + 1) * seq_len)
            seq = ci * n_seg + n
            s = states[h] if carry_mode else s_in_ref[seq, h]
            o_rows[h].append(_mm(q_effs[idx][rs], s) + atts[idx][rs, :hd])
            nm = nms[idx][n]
            s_new = s * g_lasts[idx][n * seq_len:n * seq_len + 1] - _mm(nm[:, hd:], s) + nm[:, :hd]
            if carry_mode:
                states[h] = s_new
            else:
                s_out_ref[seq, h] = s_new

    nw = nw_ref[...]
    for h in range(nh):
        o = _cat(o_rows[h], 0)
        o = o * lax.rsqrt(lane_sum(o * o) * (1.0 / hd) + NORM_EPS) * nw
        o_ref[:, h * hd:(h + 1) * hd] = o * _silu(z_ref[:, h * hd:(h + 1) * hd])

    if carry_mode:
        for h in range(nh):
            s_scr[h] = states[h]

        @pl.when(t_idx == pl.num_programs(1) - 1)
        def _():
            for h in range(nh):
                s_out_ref[h] = states[h]


def _gdn_mixer(qkv, z, ba, conv_w, gparams, norm_w, halo, s_in, *, rows, chunk, seq_len, carry_mode):
    bsz, t_len, _ = qkv.shape
    grid = (bsz, t_len // rows)
    tok = lambda width: pl.BlockSpec((None, rows, width), lambda b, t: (b, t, 0))
    const = lambda shape: pl.BlockSpec(shape, lambda b, t: (0,) * len(shape))
    in_specs = [tok(GDN_QKV), tok(GDN_WIDTH), tok(BA_PAD), const((GDN_CONV, GDN_QKV)), const((2, BA_PAD)),
                const((1, GDN_HEAD_DIM))]
    args = [qkv, z, ba, conv_w, gparams, norm_w]
    state_tail = (GDN_HEADS, GDN_HEAD_DIM, GDN_HEAD_DIM)
    if carry_mode:
        in_specs.append(pl.BlockSpec((None, SUBLANES, GDN_QKV), lambda b, t: (b, 0, 0)))
        args.append(halo)
        state_spec = pl.BlockSpec((None,) + state_tail, lambda b, t: (b, 0, 0, 0))
        scratch = [pltpu.VMEM(state_tail, F32), pltpu.VMEM((SUBLANES, GDN_QKV), F32)]
    else:
        in_specs += [tok(GDN_QKV)] * (GDN_CONV - 1)
        args += list(halo)
        state_spec = pl.BlockSpec((rows // seq_len,) + state_tail, lambda b, t: (t, 0, 0, 0))
        scratch = []
    in_specs.append(state_spec)
    args.append(s_in)
    return pl.pallas_call(
        functools.partial(_gdn_kernel, rows=rows, chunk=chunk, seq_len=seq_len, carry_mode=carry_mode),
        grid=grid,
        in_specs=in_specs,
        out_specs=[tok(GDN_WIDTH), state_spec],
        out_shape=[jax.ShapeDtypeStruct((bsz, t_len, GDN_WIDTH), F32),
                   jax.ShapeDtypeStruct(s_in.shape, F32)],
        scratch_shapes=scratch,
        compiler_params=pltpu.CompilerParams(dimension_semantics=("arbitrary",) * 2,
                                             vmem_limit_bytes=VMEM_LIMIT_BYTES),
        name="gdn_mixer",
    )(*args)


def _rwkv_kernel(*refs, rows, chunk, seq_len, carry_mode):
    (rkv_ref, lora_ref, mur_ref, mul_ref, par_ref, w2_ref, a2_ref, g2_ref, hr_ref, hl_ref, s_in_ref,
     o_ref, s_out_ref) = refs[:13]
    scratch = refs[13:]
    hd, npair, c, wd_all = RWKV_HEAD_DIM, RWKV_PAIRS, chunk, RWKV_WIDTH
    pw = 2 * hd
    n_chunks = rows // chunk
    n_seg = chunk // seq_len
    t_idx = pl.program_id(1)

    if carry_mode:
        s_scr, cr_scr, cl_scr = scratch

        @pl.when(t_idx == 0)
        def _():
            s_scr[...] = s_in_ref[...]
            cr_scr[...] = hr_ref[...]
            cl_scr[...] = hl_ref[...]

    def token_shift(x_ref, mu_ref, h_ref, c_scr):
        x = x_ref[...]
        if carry_mode:
            prev = _shift_rows_carry(x, 1, c_scr[...])
            c_scr[...] = x[rows - SUBLANES:]
        else:
            prev = _shift_rows_fix(x, 1, h_ref[...], seq_len)
        return x + mu_ref[...] * (prev - x)

    xs = token_shift(rkv_ref, mur_ref, hr_ref, cr_scr if carry_mode else None)
    lora = token_shift(lora_ref, mul_ref, hl_ref, cl_scr if carry_mode else None)
    r, kr, vr = xs[:, :wd_all], xs[:, wd_all:2 * wd_all], xs[:, 2 * wd_all:]
    par = par_ref[...]
    w0, a0, k_k, k_a, r_k, ln_w, ln_b = (par[n:n + 1] for n in range(7))
    wd = lora[:, :DECAY_LORA]
    ad = lora[:, DECAY_LORA:DECAY_LORA + AAA_LORA]
    gd = lora[:, DECAY_LORA + AAA_LORA:]
    lw = -(DECAY_SCALE * _sigmoid(w0 + _mm(jnp.tanh(wd), w2_ref[...])))
    asig = _sigmoid(a0 + _mm(ad, a2_ref[...]))
    g = _mm(_sigmoid(gd), g2_ref[...])

    li = _iota((pw, pw), 0)
    lj = _iota((pw, pw), 1)
    pair_bd = (li >= hd) == (lj >= hd)
    head_ones = pair_bd.astype(BF16)

    def head_sum(a):
        return _cat([_mm(a[:, p * pw:(p + 1) * pw], head_ones) for p in range(a.shape[1] // pw)], 1)

    kk = kr * k_k
    kk = kk * lax.rsqrt(head_sum(kk * kk) + L2_EPS)
    kr = kr * (1.0 + (asig - 1.0) * k_a)
    a_vec = -kk
    b_vec = kk * asig

    i1 = _iota((c, c), 0)
    j1 = _iota((c, c), 1)
    same1 = (i1 ^ j1) < seq_len
    incl1_bf = (same1 & (j1 <= i1)).astype(BF16)
    same1_bf = same1.astype(BF16)
    ip = _iota((c, 2 * c), 0)
    jp_full = _iota((c, 2 * c), 1)
    jp = jp_full & (c - 1)
    head1_cols = jp_full >= c
    same_p = (ip ^ jp) < seq_len
    incl_p = same_p & (jp <= ip)
    strict_p = same_p & (jp < ip)

    def stack_heads(a):
        head1 = (_iota(a.shape, 1) & hd) != 0
        return jnp.concatenate([jnp.where(head1, 0.0, a), jnp.where(head1, a, 0.0)], axis=0)

    def block_diag(m):
        return jnp.concatenate([jnp.where(head1_cols, 0.0, m), jnp.where(head1_cols, m, 0.0)], axis=0)

    pair_mul = lambda a, b: _mm_acc(a, block_diag(b))

    probs = []
    for ci in range(n_chunks):
        rs = slice(ci * c, (ci + 1) * c)
        lwc = lw[rs]
        g_incl = _mm_mask_lhs(incl1_bf, lwc)
        g_mid = 0.5 * _mm_mask_lhs(same1_bf, lwc)
        e_mid = jnp.exp(g_mid)
        inv_n = jnp.exp(g_mid - g_incl)
        r_n = r[rs] * jnp.exp(g_incl - g_mid)
        a_n = a_vec[rs] * jnp.exp(g_incl - lwc - g_mid)
        to_end = inv_n * e_mid
        full = dict(r_n=r_n, a_n=a_n, b_n=b_vec[rs] * inv_n, k_n=kr[rs] * inv_n, r_t=r_n * e_mid,
                    a_t=a_n * e_mid, b_end=b_vec[rs] * to_end, k_end=kr[rs] * to_end,
                    gam=e_mid * e_mid, v=vr[rs])
        for p in range(npair):
            prob = {name: val[:, p * pw:(p + 1) * pw] for name, val in full.items()}
            prob.update(ci=ci, p=p)
            probs.append(prob)

    ars = [jnp.concatenate([p["a_n"], p["r_n"]], axis=0) for p in probs]
    m_bks = [_mm(ar, jnp.concatenate([stack_heads(p["b_n"]), stack_heads(p["k_n"])], axis=0), NT)
             for ar, p in zip(ars, probs)]
    a_abs = [jnp.where(strict_p, m[:c, :2 * c], 0.0) for m in m_bks]
    a_rbs = [jnp.where(incl_p, m[c:, :2 * c], 0.0) for m in m_bks]
    a_aks = [jnp.where(strict_p, m[:c, 2 * c:], 0.0) for m in m_bks]
    a_rks = [jnp.where(incl_p, m[c:, 2 * c:], 0.0) for m in m_bks]
    t_invs = _unit_lower_inverses(a_abs, ip, jp, seq_len, pair_mul)
    akvs = [_mm(jnp.concatenate([a_ak, a_rk], axis=0), stack_heads(p["v"]))
            for a_ak, a_rk, p in zip(a_aks, a_rks, probs)]
    uws = [_mm_acc(t, stack_heads(jnp.concatenate([akv[:c], p["a_t"]], axis=1)))
           for t, akv, p in zip(t_invs, akvs, probs)]
    rbs = [_mm(a_rb, stack_heads(uw)) for a_rb, uw in zip(a_rbs, uws)]
    y_frees = [rb[:, :pw] + akv[c:] for rb, akv in zip(rbs, akvs)]
    r_effs = [p["r_t"] + rb[:, pw:] for p, rb in zip(probs, rbs)]
    m_corrs, n_adds = [], []
    for p, uw in zip(probs, uws):
        ms, ns = [], []
        for n in range(n_seg):
            sr = slice(n * seq_len, (n + 1) * seq_len)
            zero = jnp.zeros((seq_len, pw), F32)
            lhs = jnp.concatenate([jnp.concatenate([uw[sr, pw:], uw[sr, :pw]], axis=1),
                                   jnp.concatenate([zero, p["v"][sr]], axis=1)], axis=0)
            rhs = jnp.concatenate([p["b_end"][sr], p["k_end"][sr]], axis=0)
            mn = _mm_acc(lhs, rhs, TN)
            ms.append(jnp.where(pair_bd, mn[:pw], 0.0))
            ns.append(jnp.where(pair_bd, mn[pw:], 0.0))
        m_corrs.append(ms)
        n_adds.append(ns)

    states = [s_scr[p] for p in range(npair)] if carry_mode else None
    y_rows = [[] for _ in range(npair)]
    for idx, prob in enumerate(probs):
        p, ci = prob["p"], prob["ci"]
        for n in range(n_seg):
            sr = slice(n * seq_len, (n + 1) * seq_len)
            seq = ci * n_seg + n
            s = states[p] if carry_mode else s_in_ref[seq, p]
            y_rows[p].append(_mm_acc(r_effs[idx][sr], s, NT) + y_frees[idx][sr])
            gam = prob["gam"][n * seq_len:n * seq_len + 1]
            s_new = s * gam + _mm_acc(s, m_corrs[idx][n]) + n_adds[idx][n]
            if carry_mode:
                states[p] = s_new
            else:
                s_out_ref[seq, p] = s_new

    y = _cat([_cat(y_rows[p], 0) for p in range(npair)], 1)
    y_c = y - head_sum(y) * (1.0 / hd)
    y_n = y_c * lax.rsqrt(head_sum(y_c * y_c) * (1.0 / hd) + RWKV_LN_EPS)
    y_n = y_n * ln_w + ln_b
    bonus = head_sum(r * kr * r_k) * vr
    o_ref[...] = (y_n + bonus) * g

    if carry_mode:
        for p in range(npair):
            s_scr[p] = states[p]

        @pl.when(t_idx == pl.num_programs(1) - 1)
        def _():
            for p in range(npair):
                s_out_ref[p] = states[p]


def _rwkv_mixer(rkv, lora, mu_rkv, mu_lora, params, w2, a2, g2, halo_rkv, halo_lora, s_in, *,
                rows, chunk, seq_len, carry_mode):
    bsz, t_len, _ = rkv.shape
    grid = (bsz, t_len // rows)
    tok = lambda width: pl.BlockSpec((None, rows, width), lambda b, t: (b, t, 0))
    const = lambda shape: pl.BlockSpec(shape, lambda b, t: (0,) * len(shape))
    in_specs = [tok(3 * RWKV_WIDTH), tok(LORA_PAD), const((1, 3 * RWKV_WIDTH)), const((1, LORA_PAD)),
                const((SUBLANES, RWKV_WIDTH)), const((DECAY_LORA, RWKV_WIDTH)), const((AAA_LORA, RWKV_WIDTH)),
                const((LORA_PAD - DECAY_LORA - AAA_LORA, RWKV_WIDTH))]
    state_tail = (RWKV_PAIRS, LANES, LANES)
    if carry_mode:
        in_specs += [pl.BlockSpec((None, SUBLANES, 3 * RWKV_WIDTH), lambda b, t: (b, 0, 0)),
                     pl.BlockSpec((None, SUBLANES, LORA_PAD), lambda b, t: (b, 0, 0))]
        state_spec = pl.BlockSpec((None,) + state_tail, lambda b, t: (b, 0, 0, 0))
        scratch = [pltpu.VMEM(state_tail, F32), pltpu.VMEM((SUBLANES, 3 * RWKV_WIDTH), F32),
                   pltpu.VMEM((SUBLANES, LORA_PAD), F32)]
    else:
        in_specs += [tok(3 * RWKV_WIDTH), tok(LORA_PAD)]
        state_spec = pl.BlockSpec((rows // seq_len,) + state_tail, lambda b, t: (t, 0, 0, 0))
        scratch = []
    in_specs.append(state_spec)
    args = [rkv, lora, mu_rkv, mu_lora, params, w2, a2, g2, halo_rkv, halo_lora, s_in]
    return pl.pallas_call(
        functools.partial(_rwkv_kernel, rows=rows, chunk=chunk, seq_len=seq_len, carry_mode=carry_mode),
        grid=grid,
        in_specs=in_specs,
        out_specs=[tok(RWKV_WIDTH), state_spec],
        out_shape=[jax.ShapeDtypeStruct((bsz, t_len, RWKV_WIDTH), F32),
                   jax.ShapeDtypeStruct(s_in.shape, F32)],
        scratch_shapes=scratch,
        compiler_params=pltpu.CompilerParams(dimension_semantics=("arbitrary",) * 2,
                                             vmem_limit_bytes=VMEM_LIMIT_BYTES),
        name="rwkv_mixer",
    )(*args)


def _ffn_kernel(*refs, rows, seq_len, blocks_per_seq, carry_mode):
    x_ref, og_ref, orw_ref, wo_ref, nf_ref, wu_ref, cw_ref, wd_ref, nfin_ref = refs[:9]
    if carry_mode:
        h_ref, y_ref, tail_ref, carry_scr = refs[9:]
    else:
        f1_ref, f2_ref, y_ref, tail_ref = refs[9:]
    blk = pl.program_id(0)

    if carry_mode:
        @pl.when(blk % blocks_per_seq == 0)
        def _():
            carry_scr[...] = h_ref[...]

    o = (jnp.dot(og_ref[...].astype(BF16), wo_ref[:GDN_WIDTH], preferred_element_type=F32)
         + jnp.dot(orw_ref[...].astype(BF16), wo_ref[GDN_WIDTH:], preferred_element_type=F32))
    x1 = x_ref[...] + o
    h2 = _rms_norm(x1, nf_ref[...]).astype(BF16)
    cw = cw_ref[...]
    x2 = x1
    offs = [sum(FFN_COL_CHUNKS[:n]) for n in range(len(FFN_COL_CHUNKS))]

    def up_proj(n):
        lo, hi = offs[n], offs[n] + FFN_COL_CHUNKS[n]
        return (jnp.dot(h2, wu_ref[:, lo:hi], preferred_element_type=F32),
                jnp.dot(h2, wu_ref[:, FFN_DIM + lo:FFN_DIM + hi], preferred_element_type=F32))

    nxt = up_proj(0)
    for n, width in enumerate(FFN_COL_CHUNKS):
        cols = slice(offs[n], offs[n] + width)
        gate, val = nxt
        if n + 1 < len(FFN_COL_CHUNKS):
            nxt = up_proj(n + 1)
        conv = gate * cw[FFN_CONV - 1:FFN_CONV, cols]
        for s in range(1, FFN_CONV):
            if carry_mode:
                gs = _shift_rows_carry(gate, s, carry_scr[:, cols])
            else:
                gs = _shift_rows_fix(gate, s, (f1_ref, f2_ref)[s - 1][:, cols], seq_len)
            conv = conv + gs * cw[FFN_CONV - 1 - s:FFN_CONV - s, cols]
        if carry_mode:
            carry_scr[:, cols] = gate[rows - SUBLANES:]
            tail_ref[:, cols] = gate[rows - SUBLANES:]
        else:
            tail_ref[:, cols] = gate
        act = (_silu(conv) * val).astype(BF16)
        x2 = x2 + jnp.dot(act, wd_ref[cols, :], preferred_element_type=F32)
    y_ref[...] = _rms_norm(x2, nfin_ref[...])


def _ffn(x2d, o_gdn, o_rwkv, w_out, norm_ffn, w_up, conv_w, w_down, norm_final, halo, *,
         rows, seq_len, carry_mode):
    n_rows = x2d.shape[0]
    n_blocks = n_rows // rows
    row_spec = lambda w: pl.BlockSpec((rows, w), lambda i: (i, 0))
    whole = lambda shape: pl.BlockSpec(shape, lambda i: (0,) * len(shape), pipeline_mode=pl.Buffered(1))
    in_specs = [row_spec(D_MODEL), row_spec(GDN_WIDTH), row_spec(RWKV_WIDTH),
                whole((D_MODEL, D_MODEL)), whole((1, D_MODEL)), whole((D_MODEL, 2 * FFN_DIM)),
                whole((FFN_CONV, FFN_DIM)), whole((FFN_DIM, D_MODEL)), whole((1, D_MODEL))]
    args = [x2d, o_gdn, o_rwkv, w_out, norm_ffn, w_up, conv_w, w_down, norm_final]
    if carry_mode:
        blocks_per_seq = seq_len // rows
        in_specs.append(pl.BlockSpec((None, SUBLANES, FFN_DIM), lambda i: (i // blocks_per_seq, 0, 0)))
        args.append(halo)
        tail_spec = pl.BlockSpec((None, SUBLANES, FFN_DIM), lambda i: (i, 0, 0))
        tail_shape = jax.ShapeDtypeStruct((n_blocks, SUBLANES, FFN_DIM), F32)
        scratch = [pltpu.VMEM((SUBLANES, FFN_DIM), F32)]
    else:
        blocks_per_seq = 1
        in_specs += [row_spec(FFN_DIM), row_spec(FFN_DIM)]
        args += [halo[0], halo[1]]
        tail_spec = row_spec(FFN_DIM)
        tail_shape = jax.ShapeDtypeStruct((n_rows, FFN_DIM), F32)
        scratch = []
    return pl.pallas_call(
        functools.partial(_ffn_kernel, rows=rows, seq_len=seq_len, blocks_per_seq=blocks_per_seq,
                          carry_mode=carry_mode),
        grid=(n_blocks,),
        in_specs=in_specs,
        out_specs=[row_spec(D_MODEL), tail_spec],
        out_shape=[jax.ShapeDtypeStruct((n_rows, D_MODEL), F32), tail_shape],
        scratch_shapes=scratch,
        compiler_params=pltpu.CompilerParams(dimension_semantics=("arbitrary",),
                                             vmem_limit_bytes=VMEM_LIMIT_BYTES),
        name="ffn",
    )(*args)


def _pad_cols(a, width):
    return jnp.pad(a, ((0, 0), (0, width - a.shape[1])))


def _regroup_in_weights(w_in):
    g_end = GDN_QKV + GDN_WIDTH
    ba = w_in[:, g_end:g_end + 2 * GDN_HEADS]
    rw = w_in[:, g_end + 2 * GDN_HEADS:]
    return jnp.concatenate([w_in[:, :g_end], rw[:, :3 * RWKV_WIDTH],
                            _pad_cols(rw[:, 3 * RWKV_WIDTH:], LORA_PAD), _pad_cols(ba, BA_PAD)],
                           axis=1).astype(BF16)


def _tail_tile(state_rows):
    return jnp.pad(state_rows, ((0, 0), (SUBLANES - state_rows.shape[1], 0), (0, 0)))


def _fix_rows(state_rows, s, seq_len):
    n = state_rows.shape[1]
    first = state_rows[:, n - s:, :]
    return jnp.pad(first, ((0, 0), (0, seq_len - s), (0, 0))).reshape(-1, state_rows.shape[2])


def _pair_states(s):
    bsz = s.shape[0]
    sp = s.reshape(bsz, RWKV_PAIRS, 2, RWKV_HEAD_DIM, RWKV_HEAD_DIM)
    zero = jnp.zeros_like(sp[:, :, 0])
    top = jnp.concatenate([sp[:, :, 0], zero], axis=-1)
    bot = jnp.concatenate([zero, sp[:, :, 1]], axis=-1)
    return jnp.concatenate([top, bot], axis=-2)


def _unpair_states(sp):
    hd = RWKV_HEAD_DIM
    bsz = sp.shape[0]
    return jnp.stack([sp[:, :, :hd, :hd], sp[:, :, hd:, hd:]], axis=2).reshape(bsz, RWKV_HEADS, hd, hd)


def _layer(x, st_gdn_conv, st_gdn, st_shift, st_rwkv, st_ffn, wts, *, carry_mode):
    bsz, t_len, _ = x.shape
    n_rows = bsz * t_len
    x2d = x.reshape(n_rows, D_MODEL)
    qkv, z, rkv, lora, ba = _inproj(x2d, wts["norm_mix"], wts["w_in"])

    shift_rkv = st_shift[:, :, :3 * RWKV_WIDTH]
    shift_lora = _pad_cols(st_shift[:, 0, 3 * RWKV_WIDTH:], LORA_PAD)[:, None, :]
    if carry_mode:
        shape3 = lambda a: a.reshape(bsz, t_len, a.shape[-1])
        o_gdn, s_gdn = _gdn_mixer(shape3(qkv), shape3(z), shape3(ba), wts["gdn_conv_w"], wts["gparams"],
                                  wts["gdn_norm"], _tail_tile(st_gdn_conv), st_gdn,
                                  rows=MIXER_ROWS, chunk=GDN_CHUNK, seq_len=GDN_CHUNK, carry_mode=True)
        o_rwkv, s_rwkv = _rwkv_mixer(shape3(rkv), shape3(lora), wts["mu_rkv"], wts["mu_lora"], wts["rparams"],
                                     wts["w2"], wts["a2"], wts["g2"], _tail_tile(shift_rkv),
                                     _tail_tile(shift_lora), _pair_states(st_rwkv),
                                     rows=MIXER_ROWS, chunk=RWKV_CHUNK, seq_len=RWKV_CHUNK, carry_mode=True)
        ffn_halo = _tail_tile(st_ffn)
        ffn_rows, ffn_seq = ROW_BLOCK, t_len
    else:
        shape3 = lambda a: a.reshape(1, n_rows, a.shape[-1])
        fix = lambda st, s: _fix_rows(st, s, t_len)[None]
        o_gdn, s_gdn = _gdn_mixer(shape3(qkv), shape3(z), shape3(ba), wts["gdn_conv_w"], wts["gparams"],
                                  wts["gdn_norm"], [fix(st_gdn_conv, s) for s in range(1, GDN_CONV)], st_gdn,
                                  rows=n_rows, chunk=GDN_CHUNK, seq_len=t_len, carry_mode=False)
        o_rwkv, s_rwkv = _rwkv_mixer(shape3(rkv), shape3(lora), wts["mu_rkv"], wts["mu_lora"], wts["rparams"],
                                     wts["w2"], wts["a2"], wts["g2"], fix(shift_rkv, 1), fix(shift_lora, 1),
                                     _pair_states(st_rwkv),
                                     rows=n_rows, chunk=RWKV_CHUNK, seq_len=t_len, carry_mode=False)
        ffn_halo = [_fix_rows(st_ffn, s, t_len) for s in range(1, FFN_CONV)]
        ffn_rows, ffn_seq = n_rows, t_len
    y2d, gate_tail = _ffn(x2d, o_gdn.reshape(n_rows, GDN_WIDTH), o_rwkv.reshape(n_rows, RWKV_WIDTH),
                          wts["w_out"], wts["norm_ffn"], wts["w_up"], wts["ffn_conv_w"], wts["w_down"],
                          wts["norm_final"], ffn_halo, rows=ffn_rows, seq_len=ffn_seq, carry_mode=carry_mode)

    qkv3 = qkv.reshape(bsz, t_len, GDN_QKV)
    gdn_conv_new = qkv3[:, t_len - (GDN_CONV - 1):, :]
    shift_new = jnp.concatenate([rkv.reshape(bsz, t_len, -1)[:, t_len - 1:, :],
                                 lora.reshape(bsz, t_len, -1)[:, t_len - 1:, :RWKV_PROJ - 3 * RWKV_WIDTH]], axis=-1)
    if carry_mode:
        blocks_per_seq = t_len // ROW_BLOCK
        tails = gate_tail.reshape(bsz, blocks_per_seq, SUBLANES, FFN_DIM)[:, -1]
        ffn_new = tails[:, SUBLANES - (FFN_CONV - 1):, :]
    else:
        ffn_new = gate_tail.reshape(bsz, t_len, FFN_DIM)[:, t_len - (FFN_CONV - 1):, :]
    return (y2d.reshape(bsz, t_len, D_MODEL), gdn_conv_new, s_gdn, shift_new, _unpair_states(s_rwkv), ffn_new)


def kernel(x_prompt, x_sample, state_gdn_conv, state_gdn, state_rwkv_shift, state_rwkv, state_ffn_conv,
           norm_mix, w_in, gdn_conv_w, gdn_a_log, gdn_dt_bias, gdn_norm,
           rwkv_mu, rwkv_w0, rwkv_w2, rwkv_a0, rwkv_a2, rwkv_g2, rwkv_k_k, rwkv_k_a, rwkv_r_k,
           rwkv_ln_w, rwkv_ln_b, w_out, norm_ffn, w_up, ffn_conv_w, w_down, norm_final):
    layer = 0
    gparams = jnp.zeros((2, BA_PAD), F32)
    gparams = gparams.at[0, GDN_HEADS:2 * GDN_HEADS].set(gdn_a_log[layer])
    gparams = gparams.at[1, GDN_HEADS:2 * GDN_HEADS].set(gdn_dt_bias[layer])
    mu = rwkv_mu[layer][None, :]
    rparams = jnp.stack([rwkv_w0[layer], rwkv_a0[layer], rwkv_k_k[layer], rwkv_k_a[layer],
                         rwkv_r_k[layer].reshape(-1), rwkv_ln_w[layer], rwkv_ln_b[layer],
                         jnp.zeros((RWKV_WIDTH,), F32)])
    gate_rows = LORA_PAD - DECAY_LORA - AAA_LORA
    wts = {
        "norm_mix": norm_mix[layer][None, :],
        "w_in": _regroup_in_weights(w_in[layer]),
        "gdn_conv_w": gdn_conv_w[layer],
        "gparams": gparams,
        "gdn_norm": gdn_norm[layer][None, :],
        "mu_rkv": mu[:, :3 * RWKV_WIDTH],
        "mu_lora": _pad_cols(mu[:, 3 * RWKV_WIDTH:], LORA_PAD),
        "rparams": rparams,
        "w2": rwkv_w2[layer].astype(BF16),
        "a2": rwkv_a2[layer].astype(BF16),
        "g2": jnp.pad(rwkv_g2[layer], ((0, gate_rows - GATE_LORA), (0, 0))).astype(BF16),
        "w_out": w_out[layer].astype(BF16),
        "norm_ffn": norm_ffn[layer][None, :],
        "w_up": w_up[layer].astype(BF16),
        "ffn_conv_w": ffn_conv_w[layer],
        "w_down": w_down[layer].astype(BF16),
        "norm_final": norm_final[None, :],
    }
    n_prompt = x_prompt.shape[0]
    zeros = lambda c: jnp.zeros((n_prompt,) + c.shape[2:], c.dtype)
    outs_p = _layer(x_prompt, zeros(state_gdn_conv), zeros(state_gdn), zeros(state_rwkv_shift),
                    zeros(state_rwkv), zeros(state_ffn_conv), wts, carry_mode=True)
    outs_s = _layer(x_sample, state_gdn_conv[layer], state_gdn[layer], state_rwkv_shift[layer],
                    state_rwkv[layer], state_ffn_conv[layer], wts, carry_mode=False)
    y_p, y_s = outs_p[0], outs_s[0]
    states = []
    for n in range(1, 6):
        states += [outs_p[n][None], outs_s[n][None]]
    return (y_p, y_s, *states)
```

```python
import functools
import math

import jax
import jax.numpy as jnp
from jax import lax
from jax.experimental import pallas as pl
from jax.experimental.pallas import tpu as pltpu

F32 = jnp.float32
BF16 = jnp.bfloat16

D_MODEL = 1024
GDN_HEAD_DIM = 128
GDN_HEADS = 4
GDN_WIDTH = GDN_HEADS * GDN_HEAD_DIM
GDN_QKV = 3 * GDN_WIDTH
GDN_CONV = 4
RWKV_HEAD_DIM = 64
RWKV_HEADS = 8
RWKV_WIDTH = RWKV_HEADS * RWKV_HEAD_DIM
RWKV_PAIRS = RWKV_HEADS // 2
DECAY_LORA = 64
AAA_LORA = 64
GATE_LORA = 160
RWKV_PROJ = 3 * RWKV_WIDTH + DECAY_LORA + AAA_LORA + GATE_LORA
FFN_DIM = 2816
FFN_CONV = 3
NORM_EPS = 1e-6
L2_EPS = 1e-6
RWKV_LN_EPS = 64e-5
DECAY_SCALE = math.exp(-0.5)

LANES = 128
SUBLANES = 8
VMEM_LIMIT_BYTES = 56 * 1024 * 1024

LORA_PAD = 384
BA_PAD = LANES
P_PAD = GDN_QKV + GDN_WIDTH + 3 * RWKV_WIDTH + LORA_PAD + BA_PAD
OFF_QKV = 0
OFF_Z = OFF_QKV + GDN_QKV
OFF_RKV = OFF_Z + GDN_WIDTH
OFF_LORA = OFF_RKV + 3 * RWKV_WIDTH
OFF_BA = OFF_LORA + LORA_PAD

INPROJ_ROWS = 1024
ROW_BLOCK = 512
FFN_COL_CHUNKS = (1536, 1280)
assert sum(FFN_COL_CHUNKS) == FFN_DIM
MIXER_ROWS = 512
GDN_CHUNK = 128
RWKV_CHUNK = 64

NN = (((1,), (0,)), ((), ()))
NT = (((1,), (1,)), ((), ()))
TN = (((0,), (0,)), ((), ()))


def _dot(a, b, dims):
    return lax.dot_general(a, b, dims, preferred_element_type=F32)


def _mm(a, b, dims=NN):
    return _dot(a.astype(BF16), b.astype(BF16), dims)


def _split2(a):
    hi = a.astype(BF16)
    return hi, (a - hi.astype(F32)).astype(BF16)


def _mm_mask_lhs(mask_bf16, x, dims=NN):
    hi, lo = _split2(x)
    return _dot(mask_bf16, hi, dims) + _dot(mask_bf16, lo, dims)


def _mm_mask_rhs(x, mask_bf16, dims=NN):
    hi, lo = _split2(x)
    return _dot(hi, mask_bf16, dims) + _dot(lo, mask_bf16, dims)


_mm_acc = _mm


def _iota(shape, dim):
    return lax.broadcasted_iota(jnp.int32, shape, dim)


def _sigmoid(x):
    return 0.5 + 0.5 * jnp.tanh(0.5 * x)


def _silu(x):
    h = 0.5 * x
    return h + h * jnp.tanh(h)


def _softplus(x):
    return jnp.maximum(x, 0.0) + jnp.log1p(jnp.exp(-jnp.abs(x)))


def _rms_norm(x, w):
    return x * lax.rsqrt(jnp.mean(x * x, axis=-1, keepdims=True) + NORM_EPS) * w


def _cat(parts, axis):
    return parts[0] if len(parts) == 1 else jnp.concatenate(parts, axis=axis)


def _shift_rows_carry(x, s, carry):
    xr = pltpu.roll(x, s, 0)
    cr = pltpu.roll(carry, s, 0)
    row = _iota((SUBLANES, x.shape[1]), 0)
    head = jnp.where(row < s, cr, xr[:SUBLANES])
    return jnp.concatenate([head, xr[SUBLANES:]], axis=0)


def _shift_rows_fix(x, s, fix, seq_len):
    xr = pltpu.roll(x, s, 0)
    row = _iota(x.shape, 0)
    return jnp.where((row & (seq_len - 1)) < s, fix, xr)


def _unit_lower_inverses(ps, ti, tj, seq_len, mul):
    def joins(s):
        return ((ti ^ tj) < 2 * s) & ((ti & s) != 0) & ((tj & s) == 0)

    eye = jnp.where(ti == tj, 1.0, 0.0)
    ts = [eye + jnp.where(joins(1), p, 0.0) for p in ps]
    s = 2
    while s < seq_len:
        mask = joins(s)
        tbs = [mul(t, jnp.where(mask, p, 0.0)) for t, p in zip(ts, ps)]
        ts = [t + mul(tb, t) for tb, t in zip(tbs, ts)]
        s *= 2
    return ts


def _unit_lower_inverses_halved(ps):
    c = ps[0].shape[0]
    half = c // 2
    ti = _iota((half, c), 0)
    lane = _iota((half, c), 1)
    right = lane >= half
    zeros = jnp.zeros((half, c), F32)

    def block_diag(m):
        return jnp.concatenate([jnp.where(right, 0.0, m), jnp.where(right, m, 0.0)], axis=0)

    diag = [jnp.where(right, p[half:], p[:half]) for p in ps]
    t_ab = _unit_lower_inverses(diag, ti, lane & (half - 1), half, lambda a, b: _mm_acc(a, block_diag(b)))
    ys = [_mm_acc(jnp.where(right, 0.0, p[half:]), jnp.concatenate([jnp.where(right, 0.0, t), zeros], axis=0))
          for p, t in zip(ps, t_ab)]
    x21s = [_mm_acc(t, jnp.concatenate([zeros, y], axis=0)) for t, y in zip(t_ab, ys)]
    return [jnp.concatenate([jnp.where(right, 0.0, t), x21 + jnp.where(right, t, 0.0)], axis=0)
            for t, x21 in zip(t_ab, x21s)]


def _inproj_kernel(x_ref, nw_ref, w_ref, qkv_ref, z_ref, rkv_ref, lora_ref, ba_ref):
    h = _rms_norm(x_ref[...], nw_ref[...]).astype(BF16)
    for out_ref, off in ((qkv_ref, OFF_QKV), (z_ref, OFF_Z), (rkv_ref, OFF_RKV), (lora_ref, OFF_LORA),
                         (ba_ref, OFF_BA)):
        width = out_ref.shape[1]
        out_ref[...] = jnp.dot(h, w_ref[:, off:off + width], preferred_element_type=F32)


def _inproj(x2d, norm_w, w_perm):
    n_rows = x2d.shape[0]
    rows = min(INPROJ_ROWS, n_rows)
    widths = (GDN_QKV, GDN_WIDTH, 3 * RWKV_WIDTH, LORA_PAD, BA_PAD)
    return pl.pallas_call(
        _inproj_kernel,
        grid=(n_rows // rows,),
        in_specs=[
            pl.BlockSpec((rows, D_MODEL), lambda i: (i, 0)),
            pl.BlockSpec((1, D_MODEL), lambda i: (0, 0)),
            pl.BlockSpec((D_MODEL, P_PAD), lambda i: (0, 0), pipeline_mode=pl.Buffered(1)),
        ],
        out_specs=[pl.BlockSpec((rows, w), lambda i: (i, 0)) for w in widths],
        out_shape=[jax.ShapeDtypeStruct((n_rows, w), F32) for w in widths],
        compiler_params=pltpu.CompilerParams(dimension_semantics=("arbitrary",),
                                             vmem_limit_bytes=VMEM_LIMIT_BYTES),
        name="inproj",
    )(x2d, norm_w, w_perm)


def _gdn_kernel(*refs, rows, chunk, seq_len, carry_mode):
    qkv_ref, z_ref, ba_ref, cw_ref, gp_ref, nw_ref = refs[:6]
    n_halo = 1 if carry_mode else GDN_CONV - 1
    halo_refs = refs[6:6 + n_halo]
    s_in_ref, o_ref, s_out_ref = refs[6 + n_halo:9 + n_halo]
    scratch = refs[9 + n_halo:]
    nh, hd, c = GDN_HEADS, GDN_HEAD_DIM, chunk
    n_chunks = rows // chunk
    n_seg = chunk // seq_len
    t_idx = pl.program_id(1)

    if carry_mode:
        s_scr, c_scr = scratch

        @pl.when(t_idx == 0)
        def _():
            s_scr[...] = s_in_ref[...]
            c_scr[...] = halo_refs[0][...]

    x = qkv_ref[...]
    w = cw_ref[...]
    if carry_mode:
        tail = c_scr[...]
        x1 = _shift_rows_carry(x, 1, tail)
        u = x * w[1:2] + x1 * w[0:1]
        u_tail = tail * w[1:2] + pltpu.roll(tail, 1, 0) * w[0:1]
        acc = x * w[3:4] + x1 * w[2:3] + _shift_rows_carry(u, 2, u_tail)
        c_scr[...] = x[rows - SUBLANES:]
    else:
        acc = x * w[GDN_CONV - 1:GDN_CONV]
        for s in range(1, GDN_CONV):
            xs = _shift_rows_fix(x, s, halo_refs[s - 1][...], seq_len)
            acc = acc + xs * w[GDN_CONV - 1 - s:GDN_CONV - s]
    act = _silu(acc)

    ones = jnp.ones((hd, hd), BF16)
    lane_sum = lambda a: _mm(a, ones)
    l2n = lambda a: a * lax.rsqrt(lane_sum(a * a) + L2_EPS)
    q_h = [l2n(act[:, h * hd:(h + 1) * hd]) * (hd ** -0.5) for h in range(nh)]
    k_h = [l2n(act[:, (nh + h) * hd:(nh + h + 1) * hd]) for h in range(nh)]
    v_h = [act[:, (2 * nh + h) * hd:(2 * nh + h + 1) * hd] for h in range(nh)]

    ba = ba_ref[...]
    gp = gp_ref[...]
    beta_all = _sigmoid(ba)
    lg_all = -jnp.exp(gp[0:1]) * _softplus(ba + gp[1:2])
    lane = _iota((c, LANES), 1)
    pick = lambda arr, col: jnp.sum(jnp.where(lane == col, arr, 0.0), axis=-1, keepdims=True)

    ti = _iota((c, c), 0)
    tj = _iota((c, c), 1)
    same = (ti ^ tj) < seq_len
    incl = same & (tj <= ti)
    strict = same & (tj < ti)
    incl_bf = incl.astype(BF16)
    same_bf = same.astype(BF16)
    incl_t_bf = (same & (ti <= tj)).astype(BF16)

    probs = []
    for ci in range(n_chunks):
        r = slice(ci * c, (ci + 1) * c)
        lgc = lg_all[r]
        gb_all = _mm_mask_lhs(incl_bf, lgc)
        gtot_all = _mm_mask_lhs(same_bf, lgc)
        gb_rows = _mm_mask_rhs(lgc, incl_t_bf, TN)
        for h in range(nh):
            probs.append(dict(ci=ci, h=h, q=q_h[h][r], k=k_h[h][r], v=v_h[h][r], beta=pick(beta_all[r], h),
                              gb=pick(gb_all, nh + h), gtot=pick(gtot_all, nh + h),
                              gb_row=gb_rows[nh + h:nh + h + 1, :]))

    decays = [jnp.exp(jnp.where(incl, p["gb"] - p["gb_row"], -jnp.inf)) for p in probs]
    kbetas = [p["k"] * p["beta"] for p in probs]
    kqs = [_mm(jnp.concatenate([kb, p["q"]], axis=0), p["k"], NT) for kb, p in zip(kbetas, probs)]
    neg_ls = [jnp.where(strict, -(kq[:c] * d), 0.0) for kq, d in zip(kqs, decays)]
    attns = [kq[c:] * d for kq, d in zip(kqs, decays)]
    e_gs = [jnp.exp(p["gb"]) for p in probs]
    if seq_len == c:
        t_invs = _unit_lower_inverses_halved(neg_ls)
    else:
        t_invs = _unit_lower_inverses(neg_ls, ti, tj, seq_len, _mm_acc)
    sols = [_mm_acc(t, jnp.concatenate([p["v"] * p["beta"], kb * eg], axis=1))
            for t, p, kb, eg in zip(t_invs, probs, kbetas, e_gs)]
    atts = [_mm(a, sol) for a, sol in zip(attns, sols)]
    q_effs = [p["q"] * eg - att[:, hd:] for p, eg, att in zip(probs, e_gs, atts)]
    k_decs = [p["k"] * jnp.exp(p["gtot"] - p["gb"]) for p in probs]
    g_lasts = [jnp.exp(p["gtot"]) for p in probs]
    nms = [[_mm(kd[n * seq_len:(n + 1) * seq_len], sol[n * seq_len:(n + 1) * seq_len], TN) for n in range(n_seg)]
           for kd, sol in zip(k_decs, sols)]

    states = [s_scr[h] for h in range(nh)] if carry_mode else None
    o_rows = [[] for _ in range(nh)]
    for idx, p in enumerate(probs):
        h, ci = p["h"], p["ci"]
        for n in range(n_seg):
            rs = slice(n * seq_len, (n + 1) * seq_len)
            seq = ci * n_seg + n
            s = states[h] if carry_mode else s_in_ref[seq, h]
            o_rows[h].append(_mm(q_effs[idx][rs], s) + atts[idx][rs, :hd])
            nm = nms[idx][n]
            s_new = s * g_lasts[idx][n * seq_len:n * seq_len + 1] - _mm(nm[:, hd:], s) + nm[:, :hd]
            if carry_mode:
                states[h] = s_new
            else:
                s_out_ref[seq, h] = s_new

    nw = nw_ref[...]
    for h in range(nh):
        o = _cat(o_rows[h], 0)
        o = o * lax.rsqrt(lane_sum(o * o) * (1.0 / hd) + NORM_EPS) * nw
        o_ref[:, h * hd:(h + 1) * hd] = o * _silu(z_ref[:, h * hd:(h + 1) * hd])

    if carry_mode:
        for h in range(nh):
            s_scr[h] = states[h]

        @pl.when(t_idx == pl.num_programs(1) - 1)
        def _():
            for h in range(nh):
                s_out_ref[h] = states[h]


def _gdn_mixer(qkv, z, ba, conv_w, gparams, norm_w, halo, s_in, *, rows, chunk, seq_len, carry_mode):
    bsz, t_len, _ = qkv.shape
    grid = (bsz, t_len // rows)
    tok = lambda width: pl.BlockSpec((None, rows, width), lambda b, t: (b, t, 0))
    const = lambda shape: pl.BlockSpec(shape, lambda b, t: (0,) * len(shape))
    in_specs = [tok(GDN_QKV), tok(GDN_WIDTH), tok(BA_PAD), const((GDN_CONV, GDN_QKV)), const((2, BA_PAD)),
                const((1, GDN_HEAD_DIM))]
    args = [qkv, z, ba, conv_w, gparams, norm_w]
    state_tail = (GDN_HEADS, GDN_HEAD_DIM, GDN_HEAD_DIM)
    if carry_mode:
        in_specs.append(pl.BlockSpec((None, SUBLANES, GDN_QKV), lambda b, t: (b, 0, 0)))
        args.append(halo)
        state_spec = pl.BlockSpec((None,) + state_tail, lambda b, t: (b, 0, 0, 0))
        scratch = [pltpu.VMEM(state_tail, F32), pltpu.VMEM((SUBLANES, GDN_QKV), F32)]
    else:
        in_specs += [tok(GDN_QKV)] * (GDN_CONV - 1)
        args += list(halo)
        state_spec = pl.BlockSpec((rows // seq_len,) + state_tail, lambda b, t: (t, 0, 0, 0))
        scratch = []
    in_specs.append(state_spec)
    args.append(s_in)
    return pl.pallas_call(
        functools.partial(_gdn_kernel, rows=rows, chunk=chunk, seq_len=seq_len, carry_mode=carry_mode),
        grid=grid,
        in_specs=in_specs,
        out_specs=[tok(GDN_WIDTH), state_spec],
        out_shape=[jax.ShapeDtypeStruct((bsz, t_len, GDN_WIDTH), F32),
                   jax.ShapeDtypeStruct(s_in.shape, F32)],
        scratch_shapes=scratch,
        compiler_params=pltpu.CompilerParams(dimension_semantics=("arbitrary",) * 2,
                                             vmem_limit_bytes=VMEM_LIMIT_BYTES),
        name="gdn_mixer",
    )(*args)


def _rwkv_kernel(*refs, rows, chunk, seq_len, carry_mode):
    (rkv_ref, lora_ref, mur_ref, mul_ref, par_ref, w2_ref, a2_ref, g2_ref, hr_ref, hl_ref, s_in_ref,
     o_ref, s_out_ref) = refs[:13]
    scratch = refs[13:]
    hd, npair, c, wd_all = RWKV_HEAD_DIM, RWKV_PAIRS, chunk, RWKV_WIDTH
    pw = 2 * hd
    n_chunks = rows // chunk
    n_seg = chunk // seq_len
    t_idx = pl.program_id(1)

    if carry_mode:
        s_scr, cr_scr, cl_scr = scratch

        @pl.when(t_idx == 0)
        def _():
            s_scr[...] = s_in_ref[...]
            cr_scr[...] = hr_ref[...]
            cl_scr[...] = hl_ref[...]

    def token_shift(x_ref, mu_ref, h_ref, c_scr):
        x = x_ref[...]
        if carry_mode:
            prev = _shift_rows_carry(x, 1, c_scr[...])
            c_scr[...] = x[rows - SUBLANES:]
        else:
            prev = _shift_rows_fix(x, 1, h_ref[...], seq_len)
        return x + mu_ref[...] * (prev - x)

    xs = token_shift(rkv_ref, mur_ref, hr_ref, cr_scr if carry_mode else None)
    lora = token_shift(lora_ref, mul_ref, hl_ref, cl_scr if carry_mode else None)
    r, kr, vr = xs[:, :wd_all], xs[:, wd_all:2 * wd_all], xs[:, 2 * wd_all:]
    par = par_ref[...]
    w0, a0, k_k, k_a, r_k, ln_w, ln_b = (par[n:n + 1] for n in range(7))
    wd = lora[:, :DECAY_LORA]
    ad = lora[:, DECAY_LORA:DECAY_LORA + AAA_LORA]
    gd = lora[:, DECAY_LORA + AAA_LORA:]
    lw = -(DECAY_SCALE * _sigmoid(w0 + _mm(jnp.tanh(wd), w2_ref[...])))
    asig = _sigmoid(a0 + _mm(ad, a2_ref[...]))
    g = _mm(_sigmoid(gd), g2_ref[...])

    li = _iota((pw, pw), 0)
    lj = _iota((pw, pw), 1)
    pair_bd = (li >= hd) == (lj >= hd)
    head_ones = pair_bd.astype(BF16)

    def head_sum(a):
        return _cat([_mm(a[:, p * pw:(p + 1) * pw], head_ones) for p in range(a.shape[1] // pw)], 1)

    kk = kr * k_k
    kk = kk * lax.rsqrt(head_sum(kk * kk) + L2_EPS)
    kr = kr * (1.0 + (asig - 1.0) * k_a)
    a_vec = -kk
    b_vec = kk * asig

    i1 = _iota((c, c), 0)
    j1 = _iota((c, c), 1)
    same1 = (i1 ^ j1) < seq_len
    incl1_bf = (same1 & (j1 <= i1)).astype(BF16)
    same1_bf = same1.astype(BF16)
    ip = _iota((c, 2 * c), 0)
    jp_full = _iota((c, 2 * c), 1)
    jp = jp_full & (c - 1)
    head1_cols = jp_full >= c
    same_p = (ip ^ jp) < seq_len
    incl_p = same_p & (jp <= ip)
    strict_p = same_p & (jp < ip)

    def stack_heads(a):
        head1 = (_iota(a.shape, 1) & hd) != 0
        return jnp.concatenate([jnp.where(head1, 0.0, a), jnp.where(head1, a, 0.0)], axis=0)

    def block_diag(m):
        return jnp.concatenate([jnp.where(head1_cols, 0.0, m), jnp.where(head1_cols, m, 0.0)], axis=0)

    pair_mul = lambda a, b: _mm_acc(a, block_diag(b))

    probs = []
    for ci in range(n_chunks):
        rs = slice(ci * c, (ci + 1) * c)
        lwc = lw[rs]
        g_incl = _mm_mask_lhs(incl1_bf, lwc)
        g_mid = 0.5 * _mm_mask_lhs(same1_bf, lwc)
        e_mid = jnp.exp(g_mid)
        inv_n = jnp.exp(g_mid - g_incl)
        r_n = r[rs] * jnp.exp(g_incl - g_mid)
        a_n = a_vec[rs] * jnp.exp(g_incl - lwc - g_mid)
        to_end = inv_n * e_mid
        full = dict(r_n=r_n, a_n=a_n, b_n=b_vec[rs] * inv_n, k_n=kr[rs] * inv_n, r_t=r_n * e_mid,
                    a_t=a_n * e_mid, b_end=b_vec[rs] * to_end, k_end=kr[rs] * to_end,
                    gam=e_mid * e_mid, v=vr[rs])
        for p in range(npair):
            prob = {name: val[:, p * pw:(p + 1) * pw] for name, val in full.items()}
            prob.update(ci=ci, p=p)
            probs.append(prob)

    ars = [jnp.concatenate([p["a_n"], p["r_n"]], axis=0) for p in probs]
    m_bks = [_mm(ar, jnp.concatenate([stack_heads(p["b_n"]), stack_heads(p["k_n"])], axis=0), NT)
             for ar, p in zip(ars, probs)]
    a_abs = [jnp.where(strict_p, m[:c, :2 * c], 0.0) for m in m_bks]
    a_rbs = [jnp.where(incl_p, m[c:, :2 * c], 0.0) for m in m_bks]
    a_aks = [jnp.where(strict_p, m[:c, 2 * c:], 0.0) for m in m_bks]
    a_rks = [jnp.where(incl_p, m[c:, 2 * c:], 0.0) for m in m_bks]
    t_invs = _unit_lower_inverses(a_abs, ip, jp, seq_len, pair_mul)
    akvs = [_mm(jnp.concatenate([a_ak, a_rk], axis=0), stack_heads(p["v"]))
            for a_ak, a_rk, p in zip(a_aks, a_rks, probs)]
    uws = [_mm_acc(t, stack_heads(jnp.concatenate([akv[:c], p["a_t"]], axis=1)))
           for t, akv, p in zip(t_invs, akvs, probs)]
    rbs = [_mm(a_rb, stack_heads(uw)) for a_rb, uw in zip(a_rbs, uws)]
    y_frees = [rb[:, :pw] + akv[c:] for rb, akv in zip(rbs, akvs)]
    r_effs = [p["r_t"] + rb[:, pw:] for p, rb in zip(probs, rbs)]
    m_corrs, n_adds = [], []
    for p, uw in zip(probs, uws):
        ms, ns = [], []
        for n in range(n_seg):
            sr = slice(n * seq_len, (n + 1) * seq_len)
            zero = jnp.zeros((seq_len, pw), F32)
            lhs = jnp.concatenate([jnp.concatenate([uw[sr, pw:], uw[sr, :pw]], axis=1),
                                   jnp.concatenate([zero, p["v"][sr]], axis=1)], axis=0)
            rhs = jnp.concatenate([p["b_end"][sr], p["k_end"][sr]], axis=0)
            mn = _mm_acc(lhs, rhs, TN)
            ms.append(jnp.where(pair_bd, mn[:pw], 0.0))
            ns.append(jnp.where(pair_bd, mn[pw:], 0.0))
        m_corrs.append(ms)
        n_adds.append(ns)

    states = [s_scr[p] for p in range(npair)] if carry_mode else None
    y_rows = [[] for _ in range(npair)]
    for idx, prob in enumerate(probs):
        p, ci = prob["p"], prob["ci"]
        for n in range(n_seg):
            sr = slice(n * seq_len, (n + 1) * seq_len)
            seq = ci * n_seg + n
            s = states[p] if carry_mode else s_in_ref[seq, p]
            y_rows[p].append(_mm_acc(r_effs[idx][sr], s, NT) + y_frees[idx][sr])
            gam = prob["gam"][n * seq_len:n * seq_len + 1]
            s_new = s * gam + _mm_acc(s, m_corrs[idx][n]) + n_adds[idx][n]
            if carry_mode:
                states[p] = s_new
            else:
                s_out_ref[seq, p] = s_new

    y = _cat([_cat(y_rows[p], 0) for p in range(npair)], 1)
    y_c = y - head_sum(y) * (1.0 / hd)
    y_n = y_c * lax.rsqrt(head_sum(y_c * y_c) * (1.0 / hd) + RWKV_LN_EPS)
    y_n = y_n * ln_w + ln_b
    bonus = head_sum(r * kr * r_k) * vr
    o_ref[...] = (y_n + bonus) * g

    if carry_mode:
        for p in range(npair):
            s_scr[p] = states[p]

        @pl.when(t_idx == pl.num_programs(1) - 1)
        def _():
            for p in range(npair):
                s_out_ref[p] = states[p]


def _rwkv_mixer(rkv, lora, mu_rkv, mu_lora, params, w2, a2, g2, halo_rkv, halo_lora, s_in, *,
                rows, chunk, seq_len, carry_mode):
    bsz, t_len, _ = rkv.shape
    grid = (bsz, t_len // rows)
    tok = lambda width: pl.BlockSpec((None, rows, width), lambda b, t: (b, t, 0))
    const = lambda shape: pl.BlockSpec(shape, lambda b, t: (0,) * len(shape))
    in_specs = [tok(3 * RWKV_WIDTH), tok(LORA_PAD), const((1, 3 * RWKV_WIDTH)), const((1, LORA_PAD)),
                const((SUBLANES, RWKV_WIDTH)), const((DECAY_LORA, RWKV_WIDTH)), const((AAA_LORA, RWKV_WIDTH)),
                const((LORA_PAD - DECAY_LORA - AAA_LORA, RWKV_WIDTH))]
    state_tail = (RWKV_PAIRS, LANES, LANES)
    if carry_mode:
        in_specs += [pl.BlockSpec((None, SUBLANES, 3 * RWKV_WIDTH), lambda b, t: (b, 0, 0)),
                     pl.BlockSpec((None, SUBLANES, LORA_PAD), lambda b, t: (b, 0, 0))]
        state_spec = pl.BlockSpec((None,) + state_tail, lambda b, t: (b, 0, 0, 0))
        scratch = [pltpu.VMEM(state_tail, F32), pltpu.VMEM((SUBLANES, 3 * RWKV_WIDTH), F32),
                   pltpu.VMEM((SUBLANES, LORA_PAD), F32)]
    else:
        in_specs += [tok(3 * RWKV_WIDTH), tok(LORA_PAD)]
        state_spec = pl.BlockSpec((rows // seq_len,) + state_tail, lambda b, t: (t, 0, 0, 0))
        scratch = []
    in_specs.append(state_spec)
    args = [rkv, lora, mu_rkv, mu_lora, params, w2, a2, g2, halo_rkv, halo_lora, s_in]
    return pl.pallas_call(
        functools.partial(_rwkv_kernel, rows=rows, chunk=chunk, seq_len=seq_len, carry_mode=carry_mode),
        grid=grid,
        in_specs=in_specs,
        out_specs=[tok(RWKV_WIDTH), state_spec],
        out_shape=[jax.ShapeDtypeStruct((bsz, t_len, RWKV_WIDTH), F32),
                   jax.ShapeDtypeStruct(s_in.shape, F32)],
        scratch_shapes=scratch,
        compiler_params=pltpu.CompilerParams(dimension_semantics=("arbitrary",) * 2,
                                             vmem_limit_bytes=VMEM_LIMIT_BYTES),
        name="rwkv_mixer",
    )(*args)


def _ffn_kernel(*refs, rows, seq_len, blocks_per_seq, carry_mode):
    x_ref, og_ref, orw_ref, wo_ref, nf_ref, wu_ref, cw_ref, wd_ref, nfin_ref = refs[:9]
    if carry_mode:
        h_ref, y_ref, tail_ref, carry_scr = refs[9:]
    else:
        f1_ref, f2_ref, y_ref, tail_ref = refs[9:]
    blk = pl.program_id(0)

    if carry_mode:
        @pl.when(blk % blocks_per_seq == 0)
        def _():
            carry_scr[...] = h_ref[...]

    o = (jnp.dot(og_ref[...].astype(BF16), wo_ref[:GDN_WIDTH], preferred_element_type=F32)
         + jnp.dot(orw_ref[...].astype(BF16), wo_ref[GDN_WIDTH:], preferred_element_type=F32))
    x1 = x_ref[...] + o
    h2 = _rms_norm(x1, nf_ref[...]).astype(BF16)
    cw = cw_ref[...]
    x2 = x1
    offs = [sum(FFN_COL_CHUNKS[:n]) for n in range(len(FFN_COL_CHUNKS))]

    def up_proj(n):
        lo, hi = offs[n], offs[n] + FFN_COL_CHUNKS[n]
        return (jnp.dot(h2, wu_ref[:, lo:hi], preferred_element_type=F32),
                jnp.dot(h2, wu_ref[:, FFN_DIM + lo:FFN_DIM + hi], preferred_element_type=F32))

    nxt = up_proj(0)
    for n, width in enumerate(FFN_COL_CHUNKS):
        cols = slice(offs[n], offs[n] + width)
        gate, val = nxt
        if n + 1 < len(FFN_COL_CHUNKS):
            nxt = up_proj(n + 1)
        conv = gate * cw[FFN_CONV - 1:FFN_CONV, cols]
        for s in range(1, FFN_CONV):
            if carry_mode:
                gs = _shift_rows_carry(gate, s, carry_scr[:, cols])
            else:
                gs = _shift_rows_fix(gate, s, (f1_ref, f2_ref)[s - 1][:, cols], seq_len)
            conv = conv + gs * cw[FFN_CONV - 1 - s:FFN_CONV - s, cols]
        if carry_mode:
            carry_scr[:, cols] = gate[rows - SUBLANES:]
            tail_ref[:, cols] = gate[rows - SUBLANES:]
        else:
            tail_ref[:, cols] = gate
        act = (_silu(conv) * val).astype(BF16)
        x2 = x2 + jnp.dot(act, wd_ref[cols, :], preferred_element_type=F32)
    y_ref[...] = _rms_norm(x2, nfin_ref[...])


def _ffn(x2d, o_gdn, o_rwkv, w_out, norm_ffn, w_up, conv_w, w_down, norm_final, halo, *,
         rows, seq_len, carry_mode):
    n_rows = x2d.shape[0]
    n_blocks = n_rows // rows
    row_spec = lambda w: pl.BlockSpec((rows, w), lambda i: (i, 0))
    whole = lambda shape: pl.BlockSpec(shape, lambda i: (0,) * len(shape), pipeline_mode=pl.Buffered(1))
    in_specs = [row_spec(D_MODEL), row_spec(GDN_WIDTH), row_spec(RWKV_WIDTH),
                whole((D_MODEL, D_MODEL)), whole((1, D_MODEL)), whole((D_MODEL, 2 * FFN_DIM)),
                whole((FFN_CONV, FFN_DIM)), whole((FFN_DIM, D_MODEL)), whole((1, D_MODEL))]
    args = [x2d, o_gdn, o_rwkv, w_out, norm_ffn, w_up, conv_w, w_down, norm_final]
    if carry_mode:
        blocks_per_seq = seq_len // rows
        in_specs.append(pl.BlockSpec((None, SUBLANES, FFN_DIM), lambda i: (i // blocks_per_seq, 0, 0)))
        args.append(halo)
        tail_spec = pl.BlockSpec((None, SUBLANES, FFN_DIM), lambda i: (i, 0, 0))
        tail_shape = jax.ShapeDtypeStruct((n_blocks, SUBLANES, FFN_DIM), F32)
        scratch = [pltpu.VMEM((SUBLANES, FFN_DIM), F32)]
    else:
        blocks_per_seq = 1
        in_specs += [row_spec(FFN_DIM), row_spec(FFN_DIM)]
        args += [halo[0], halo[1]]
        tail_spec = row_spec(FFN_DIM)
        tail_shape = jax.ShapeDtypeStruct((n_rows, FFN_DIM), F32)
        scratch = []
    return pl.pallas_call(
        functools.partial(_ffn_kernel, rows=rows, seq_len=seq_len, blocks_per_seq=blocks_per_seq,
                          carry_mode=carry_mode),
        grid=(n_blocks,),
        in_specs=in_specs,
        out_specs=[row_spec(D_MODEL), tail_spec],
        out_shape=[jax.ShapeDtypeStruct((n_rows, D_MODEL), F32), tail_shape],
        scratch_shapes=scratch,
        compiler_params=pltpu.CompilerParams(dimension_semantics=("arbitrary",),
                                             vmem_limit_bytes=VMEM_LIMIT_BYTES),
        name="ffn",
    )(*args)


def _pad_cols(a, width):
    return jnp.pad(a, ((0, 0), (0, width - a.shape[1])))


def _regroup_in_weights(w_in):
    g_end = GDN_QKV + GDN_WIDTH
    ba = w_in[:, g_end:g_end + 2 * GDN_HEADS]
    rw = w_in[:, g_end + 2 * GDN_HEADS:]
    return jnp.concatenate([w_in[:, :g_end], rw[:, :3 * RWKV_WIDTH],
                            _pad_cols(rw[:, 3 * RWKV_WIDTH:], LORA_PAD), _pad_cols(ba, BA_PAD)],
                           axis=1).astype(BF16)


def _tail_tile(state_rows):
    return jnp.pad(state_rows, ((0, 0), (SUBLANES - state_rows.shape[1], 0), (0, 0)))


def _fix_rows(state_rows, s, seq_len):
    n = state_rows.shape[1]
    first = state_rows[:, n - s:, :]
    return jnp.pad(first, ((0, 0), (0, seq_len - s), (0, 0))).reshape(-1, state_rows.shape[2])


def _pair_states(s):
    bsz = s.shape[0]
    sp = s.reshape(bsz, RWKV_PAIRS, 2, RWKV_HEAD_DIM, RWKV_HEAD_DIM)
    zero = jnp.zeros_like(sp[:, :, 0])
    top = jnp.concatenate([sp[:, :, 0], zero], axis=-1)
    bot = jnp.concatenate([zero, sp[:, :, 1]], axis=-1)
    return jnp.concatenate([top, bot], axis=-2)


def _unpair_states(sp):
    hd = RWKV_HEAD_DIM
    bsz = sp.shape[0]
    return jnp.stack([sp[:, :, :hd, :hd], sp[:, :, hd:, hd:]], axis=2).reshape(bsz, RWKV_HEADS, hd, hd)


def _layer(x, st_gdn_conv, st_gdn, st_shift, st_rwkv, st_ffn, wts, *, carry_mode):
    bsz, t_len, _ = x.shape
    n_rows = bsz * t_len
    x2d = x.reshape(n_rows, D_MODEL)
    qkv, z, rkv, lora, ba = _inproj(x2d, wts["norm_mix"], wts["w_in"])

    shift_rkv = st_shift[:, :, :3 * RWKV_WIDTH]
    shift_lora = _pad_cols(st_shift[:, 0, 3 * RWKV_WIDTH:], LORA_PAD)[:, None, :]
    if carry_mode:
        shape3 = lambda a: a.reshape(bsz, t_len, a.shape[-1])
        o_gdn, s_gdn = _gdn_mixer(shape3(qkv), shape3(z), shape3(ba), wts["gdn_conv_w"], wts["gparams"],
                                  wts["gdn_norm"], _tail_tile(st_gdn_conv), st_gdn,
                                  rows=MIXER_ROWS, chunk=GDN_CHUNK, seq_len=GDN_CHUNK, carry_mode=True)
        o_rwkv, s_rwkv = _rwkv_mixer(shape3(rkv), shape3(lora), wts["mu_rkv"], wts["mu_lora"], wts["rparams"],
                                     wts["w2"], wts["a2"], wts["g2"], _tail_tile(shift_rkv),
                                     _tail_tile(shift_lora), _pair_states(st_rwkv),
                                     rows=MIXER_ROWS, chunk=RWKV_CHUNK, seq_len=RWKV_CHUNK, carry_mode=True)
        ffn_halo = _tail_tile(st_ffn)
        ffn_rows, ffn_seq = ROW_BLOCK, t_len
    else:
        shape3 = lambda a: a.reshape(1, n_rows, a.shape[-1])
        fix = lambda st, s: _fix_rows(st, s, t_len)[None]
        o_gdn, s_gdn = _gdn_mixer(shape3(qkv), shape3(z), shape3(ba), wts["gdn_conv_w"], wts["gparams"],
                                  wts["gdn_norm"], [fix(st_gdn_conv, s) for s in range(1, GDN_CONV)], st_gdn,
                                  rows=n_rows, chunk=GDN_CHUNK, seq_len=t_len, carry_mode=False)
        o_rwkv, s_rwkv = _rwkv_mixer(shape3(rkv), shape3(lora), wts["mu_rkv"], wts["mu_lora"], wts["rparams"],
                                     wts["w2"], wts["a2"], wts["g2"], fix(shift_rkv, 1), fix(shift_lora, 1),
                                     _pair_states(st_rwkv),
                                     rows=n_rows, chunk=RWKV_CHUNK, seq_len=t_len, carry_mode=False)
        ffn_halo = [_fix_rows(st_ffn, s, t_len) for s in range(1, FFN_CONV)]
        ffn_rows, ffn_seq = n_rows, t_len
    y2d, gate_tail = _ffn(x2d, o_gdn.reshape(n_rows, GDN_WIDTH), o_rwkv.reshape(n_rows, RWKV_WIDTH),
                          wts["w_out"], wts["norm_ffn"], wts["w_up"], wts["ffn_conv_w"], wts["w_down"],
                          wts["norm_final"], ffn_halo, rows=ffn_rows, seq_len=ffn_seq, carry_mode=carry_mode)

    qkv3 = qkv.reshape(bsz, t_len, GDN_QKV)
    gdn_conv_new = qkv3[:, t_len - (GDN_CONV - 1):, :]
    shift_new = jnp.concatenate([rkv.reshape(bsz, t_len, -1)[:, t_len - 1:, :],
                                 lora.reshape(bsz, t_len, -1)[:, t_len - 1:, :RWKV_PROJ - 3 * RWKV_WIDTH]], axis=-1)
    if carry_mode:
        blocks_per_seq = t_len // ROW_BLOCK
        tails = gate_tail.reshape(bsz, blocks_per_seq, SUBLANES, FFN_DIM)[:, -1]
        ffn_new = tails[:, SUBLANES - (FFN_CONV - 1):, :]
    else:
        ffn_new = gate_tail.reshape(bsz, t_len, FFN_DIM)[:, t_len - (FFN_CONV - 1):, :]
    return (y2d.reshape(bsz, t_len, D_MODEL), gdn_conv_new, s_gdn, shift_new, _unpair_states(s_rwkv), ffn_new)


def kernel(x_prompt, x_sample, state_gdn_conv, state_gdn, state_rwkv_shift, state_rwkv, state_ffn_conv,
           norm_mix, w_in, gdn_conv_w, gdn_a_log, gdn_dt_bias, gdn_norm,
           rwkv_mu, rwkv_w0, rwkv_w2, rwkv_a0, rwkv_a2, rwkv_g2, rwkv_k_k, rwkv_k_a, rwkv_r_k,
           rwkv_ln_w, rwkv_ln_b, w_out, norm_ffn, w_up, ffn_conv_w, w_down, norm_final):
    layer = 0
    gparams = jnp.zeros((2, BA_PAD), F32)
    gparams = gparams.at[0, GDN_HEADS:2 * GDN_HEADS].set(gdn_a_log[layer])
    gparams = gparams.at[1, GDN_HEADS:2 * GDN_HEADS].set(gdn_dt_bias[layer])
    mu = rwkv_mu[layer][None, :]
    rparams = jnp.stack([rwkv_w0[layer], rwkv_a0[layer], rwkv_k_k[layer], rwkv_k_a[layer],
                         rwkv_r_k[layer].reshape(-1), rwkv_ln_w[layer], rwkv_ln_b[layer],
                         jnp.zeros((RWKV_WIDTH,), F32)])
    gate_rows = LORA_PAD - DECAY_LORA - AAA_LORA
    wts = {
        "norm_mix": norm_mix[layer][None, :],
        "w_in": _regroup_in_weights(w_in[layer]),
        "gdn_conv_w": gdn_conv_w[layer],
        "gparams": gparams,
        "gdn_norm": gdn_norm[layer][None, :],
        "mu_rkv": mu[:, :3 * RWKV_WIDTH],
        "mu_lora": _pad_cols(mu[:, 3 * RWKV_WIDTH:], LORA_PAD),
        "rparams": rparams,
        "w2": rwkv_w2[layer].astype(BF16),
        "a2": rwkv_a2[layer].astype(BF16),
        "g2": jnp.pad(rwkv_g2[layer], ((0, gate_rows - GATE_LORA), (0, 0))).astype(BF16),
        "w_out": w_out[layer].astype(BF16),
        "norm_ffn": norm_ffn[layer][None, :],
        "w_up": w_up[layer].astype(BF16),
        "ffn_conv_w": ffn_conv_w[layer],
        "w_down": w_down[layer].astype(BF16),
        "norm_final": norm_final[None, :],
    }
    n_prompt = x_prompt.shape[0]
    zeros = lambda c: jnp.zeros((n_prompt,) + c.shape[2:], c.dtype)
    outs_p = _layer(x_prompt, zeros(state_gdn_conv), zeros(state_gdn), zeros(state_rwkv_shift),
                    zeros(state_rwkv), zeros(state_ffn_conv), wts, carry_mode=True)
    outs_s = _layer(x_sample, state_gdn_conv[layer], state_gdn[layer], state_rwkv_shift[layer],
                    state_rwkv[layer], state_ffn_conv[layer], wts, carry_mode=False)
    y_p, y_s = outs_p[0], outs_s[0]
    states = []
    for n in range(1, 6):
        states += [outs_p[n][None], outs_s[n][None]]
    return (y_p, y_s, *states)
```

```python
import functools
import math

import jax
import jax.numpy as jnp
from jax import lax
from jax.experimental import pallas as pl
from jax.experimental.pallas import tpu as pltpu

F32 = jnp.float32
BF16 = jnp.bfloat16

D_MODEL = 1024
GDN_HEAD_DIM = 128
GDN_HEADS = 4
GDN_WIDTH = GDN_HEADS * GDN_HEAD_DIM
GDN_QKV = 3 * GDN_WIDTH
GDN_CONV = 4
RWKV_HEAD_DIM = 64
RWKV_HEADS = 8
RWKV_WIDTH = RWKV_HEADS * RWKV_HEAD_DIM
RWKV_PAIRS = RWKV_HEADS // 2
DECAY_LORA = 64
AAA_LORA = 64
GATE_LORA = 160
RWKV_PROJ = 3 * RWKV_WIDTH + DECAY_LORA + AAA_LORA + GATE_LORA
FFN_DIM = 2816
FFN_CONV = 3
NORM_EPS = 1e-6
L2_EPS = 1e-6
RWKV_LN_EPS = 64e-5
DECAY_SCALE = math.exp(-0.5)

LANES = 128
SUBLANES = 8
VMEM_LIMIT_BYTES = 56 * 1024 * 1024

LORA_PAD = 384
BA_PAD = LANES
P_PAD = GDN_QKV + GDN_WIDTH + 3 * RWKV_WIDTH + LORA_PAD + BA_PAD
OFF_QKV = 0
OFF_Z = OFF_QKV + GDN_QKV
OFF_RKV = OFF_Z + GDN_WIDTH
OFF_LORA = OFF_RKV + 3 * RWKV_WIDTH
OFF_BA = OFF_LORA + LORA_PAD

INPROJ_ROWS = 1024
ROW_BLOCK = 512
FFN_COL_CHUNKS = (1536, 1280)
assert sum(FFN_COL_CHUNKS) == FFN_DIM
MIXER_ROWS = 512
GDN_CHUNK = 128
RWKV_CHUNK = 64

NN = (((1,), (0,)), ((), ()))
NT = (((1,), (1,)), ((), ()))
TN = (((0,), (0,)), ((), ()))


def _dot(a, b, dims):
    return lax.dot_general(a, b, dims, preferred_element_type=F32)


def _mm(a, b, dims=NN):
    return _dot(a.astype(BF16), b.astype(BF16), dims)


def _split2(a):
    hi = a.astype(BF16)
    return hi, (a - hi.astype(F32)).astype(BF16)


def _mm_mask_lhs(mask_bf16, x, dims=NN):
    hi, lo = _split2(x)
    return _dot(mask_bf16, hi, dims) + _dot(mask_bf16, lo, dims)


def _mm_mask_rhs(x, mask_bf16, dims=NN):
    hi, lo = _split2(x)
    return _dot(hi, mask_bf16, dims) + _dot(lo, mask_bf16, dims)


_mm_acc = _mm


def _iota(shape, dim):
    return lax.broadcasted_iota(jnp.int32, shape, dim)


def _sigmoid(x):
    return 0.5 + 0.5 * jnp.tanh(0.5 * x)


def _silu(x):
    h = 0.5 * x
    return h + h * jnp.tanh(h)


def _softplus(x):
    return jnp.maximum(x, 0.0) + jnp.log1p(jnp.exp(-jnp.abs(x)))


def _rms_norm(x, w):
    return x * lax.rsqrt(jnp.mean(x * x, axis=-1, keepdims=True) + NORM_EPS) * w


def _cat(parts, axis):
    return parts[0] if len(parts) == 1 else jnp.concatenate(parts, axis=axis)


def _shift_rows_carry(x, s, carry):
    xr = pltpu.roll(x, s, 0)
    cr = pltpu.roll(carry, s, 0)
    row = _iota((SUBLANES, x.shape[1]), 0)
    head = jnp.where(row < s, cr, xr[:SUBLANES])
    return jnp.concatenate([head, xr[SUBLANES:]], axis=0)


def _shift_rows_fix(x, s, fix, seq_len):
    xr = pltpu.roll(x, s, 0)
    row = _iota(x.shape, 0)
    return jnp.where((row & (seq_len - 1)) < s, fix, xr)


def _unit_lower_inverses(ps, ti, tj, seq_len, mul):
    def joins(s):
        return ((ti ^ tj) < 2 * s) & ((ti & s) != 0) & ((tj & s) == 0)

    eye = jnp.where(ti == tj, 1.0, 0.0)
    ts = [eye + jnp.where(joins(1), p, 0.0) for p in ps]
    s = 2
    while s < seq_len:
        mask = joins(s)
        tbs = [mul(t, jnp.where(mask, p, 0.0)) for t, p in zip(ts, ps)]
        ts = [t + mul(tb, t) for tb, t in zip(tbs, ts)]
        s *= 2
    return ts


def _unit_lower_inverses_halved(ps):
    c = ps[0].shape[0]
    half = c // 2
    ti = _iota((half, c), 0)
    lane = _iota((half, c), 1)
    right = lane >= half
    zeros = jnp.zeros((half, c), F32)

    def block_diag(m):
        return jnp.concatenate([jnp.where(right, 0.0, m), jnp.where(right, m, 0.0)], axis=0)

    diag = [jnp.where(right, p[half:], p[:half]) for p in ps]
    t_ab = _unit_lower_inverses(diag, ti, lane & (half - 1), half, lambda a, b: _mm_acc(a, block_diag(b)))
    ys = [_mm_acc(jnp.where(right, 0.0, p[half:]), jnp.concatenate([jnp.where(right, 0.0, t), zeros], axis=0))
          for p, t in zip(ps, t_ab)]
    x21s = [_mm_acc(t, jnp.concatenate([zeros, y], axis=0)) for t, y in zip(t_ab, ys)]
    return [jnp.concatenate([jnp.where(right, 0.0, t), x21 + jnp.where(right, t, 0.0)], axis=0)
            for t, x21 in zip(t_ab, x21s)]


def _inproj_kernel(x_ref, nw_ref, w_ref, qkv_ref, z_ref, rkv_ref, lora_ref, ba_ref):
    h = _rms_norm(x_ref[...], nw_ref[...]).astype(BF16)
    for out_ref, off in ((qkv_ref, OFF_QKV), (z_ref, OFF_Z), (rkv_ref, OFF_RKV), (lora_ref, OFF_LORA),
                         (ba_ref, OFF_BA)):
        width = out_ref.shape[1]
        out_ref[...] = jnp.dot(h, w_ref[:, off:off + width], preferred_element_type=F32)


def _inproj(x2d, norm_w, w_perm):
    n_rows = x2d.shape[0]
    rows = min(INPROJ_ROWS, n_rows)
    widths = (GDN_QKV, GDN_WIDTH, 3 * RWKV_WIDTH, LORA_PAD, BA_PAD)
    return pl.pallas_call(
        _inproj_kernel,
        grid=(n_rows // rows,),
        in_specs=[
            pl.BlockSpec((rows, D_MODEL), lambda i: (i, 0)),
            pl.BlockSpec((1, D_MODEL), lambda i: (0, 0)),
            pl.BlockSpec((D_MODEL, P_PAD), lambda i: (0, 0), pipeline_mode=pl.Buffered(1)),
        ],
        out_specs=[pl.BlockSpec((rows, w), lambda i: (i, 0)) for w in widths],
        out_shape=[jax.ShapeDtypeStruct((n_rows, w), F32) for w in widths],
        compiler_params=pltpu.CompilerParams(dimension_semantics=("arbitrary",),
                                             vmem_limit_bytes=VMEM_LIMIT_BYTES),
        name="inproj",
    )(x2d, norm_w, w_perm)


def _gdn_kernel(*refs, rows, chunk, seq_len, carry_mode):
    qkv_ref, z_ref, ba_ref, cw_ref, gp_ref, nw_ref = refs[:6]
    n_halo = 1 if carry_mode else GDN_CONV - 1
    halo_refs = refs[6:6 + n_halo]
    s_in_ref, o_ref, s_out_ref = refs[6 + n_halo:9 + n_halo]
    scratch = refs[9 + n_halo:]
    nh, hd, c = GDN_HEADS, GDN_HEAD_DIM, chunk
    n_chunks = rows // chunk
    n_seg = chunk // seq_len
    t_idx = pl.program_id(1)

    if carry_mode:
        s_scr, c_scr = scratch

        @pl.when(t_idx == 0)
        def _():
            s_scr[...] = s_in_ref[...]
            c_scr[...] = halo_refs[0][...]

    ones = jnp.ones((hd, hd), BF16)
    lane_sum = lambda a: _mm(a, ones)
    l2n = lambda a: a * lax.rsqrt(lane_sum(a * a) + L2_EPS)

    def conv_silu(col):
        cs = slice(col * hd, (col + 1) * hd)
        x = qkv_ref[:, cs]
        w = cw_ref[:, cs]
        if carry_mode:
            tail = c_scr[:, cs]
            x1 = _shift_rows_carry(x, 1, tail)
            u = x * w[1:2] + x1 * w[0:1]
            u_tail = tail * w[1:2] + pltpu.roll(tail, 1, 0) * w[0:1]
            acc = x * w[3:4] + x1 * w[2:3] + _shift_rows_carry(u, 2, u_tail)
            c_scr[:, cs] = x[rows - SUBLANES:]
        else:
            acc = x * w[GDN_CONV - 1:GDN_CONV]
            for s in range(1, GDN_CONV):
                xs = _shift_rows_fix(x, s, halo_refs[s - 1][:, cs], seq_len)
                acc = acc + xs * w[GDN_CONV - 1 - s:GDN_CONV - s]
        return _silu(acc)

    q_h = [l2n(conv_silu(h)) * (hd ** -0.5) for h in range(nh)]
    k_h = [l2n(conv_silu(nh + h)) for h in range(nh)]
    v_h = [conv_silu(2 * nh + h) for h in range(nh)]

    ba = ba_ref[...]
    gp = gp_ref[...]
    beta_all = _sigmoid(ba)
    lg_all = -jnp.exp(gp[0:1]) * _softplus(ba + gp[1:2])
    lane = _iota((c, LANES), 1)
    pick = lambda arr, col: jnp.sum(jnp.where(lane == col, arr, 0.0), axis=-1, keepdims=True)

    ti = _iota((c, c), 0)
    tj = _iota((c, c), 1)
    same = (ti ^ tj) < seq_len
    incl = same & (tj <= ti)
    strict = same & (tj < ti)
    incl_bf = incl.astype(BF16)
    same_bf = same.astype(BF16)
    incl_t_bf = (same & (ti <= tj)).astype(BF16)

    probs = []
    for ci in range(n_chunks):
        r = slice(ci * c, (ci + 1) * c)
        lgc = lg_all[r]
        gb_all = _mm_mask_lhs(incl_bf, lgc)
        gtot_all = _mm_mask_lhs(same_bf, lgc)
        gb_rows = _mm_mask_rhs(lgc, incl_t_bf, TN)
        for h in range(nh):
            probs.append(dict(ci=ci, h=h, q=q_h[h][r], k=k_h[h][r], v=v_h[h][r], beta=pick(beta_all[r], h),
                              gb=pick(gb_all, nh + h), gtot=pick(gtot_all, nh + h),
                              gb_row=gb_rows[nh + h:nh + h + 1, :]))

    decays = [jnp.exp(jnp.where(incl, p["gb"] - p["gb_row"], -jnp.inf)) for p in probs]
    kbetas = [p["k"] * p["beta"] for p in probs]
    kqs = [_mm(jnp.concatenate([kb, p["q"]], axis=0), p["k"], NT) for kb, p in zip(kbetas, probs)]
    neg_ls = [jnp.where(strict, -(kq[:c] * d), 0.0) for kq, d in zip(kqs, decays)]
    attns = [kq[c:] * d for kq, d in zip(kqs, decays)]
    e_gs = [jnp.exp(p["gb"]) for p in probs]
    if seq_len == c:
        t_invs = _unit_lower_inverses_halved(neg_ls)
    else:
        t_invs = _unit_lower_inverses(neg_ls, ti, tj, seq_len, _mm_acc)
    sols = [_mm_acc(t, jnp.concatenate([p["v"] * p["beta"], kb * eg], axis=1))
            for t, p, kb, eg in zip(t_invs, probs, kbetas, e_gs)]
    atts = [_mm(a, sol) for a, sol in zip(attns, sols)]
    q_effs = [p["q"] * eg - att[:, hd:] for p, eg, att in zip(probs, e_gs, atts)]
    k_decs = [p["k"] * jnp.exp(p["gtot"] - p["gb"]) for p in probs]
    g_lasts = [jnp.exp(p["gtot"]) for p in probs]
    nms = [[_mm(kd[n * seq_len:(n + 1) * seq_len], sol[n * seq_len:(n + 1) * seq_len], TN) for n in range(n_seg)]
           for kd, sol in zip(k_decs, sols)]

    states = [s_scr[h] for h in range(nh)] if carry_mode else None
    o_rows = [[] for _ in range(nh)]
    for idx, p in enumerate(probs):
        h, ci = p["h"], p["ci"]
        for n in range(n_seg):
            rs = slice(n * seq_len, (n + 1) * seq_len)
            seq = ci * n_seg + n
            s = states[h] if carry_mode else s_in_ref[seq, h]
            o_rows[h].append(_mm(q_effs[idx][rs], s) + atts[idx][rs, :hd])
            nm = nms[idx][n]
            s_new = s * g_lasts[idx][n * seq_len:n * seq_len + 1] - _mm(nm[:, hd:], s) + nm[:, :hd]
            if carry_mode:
                states[h] = s_new
            else:
                s_out_ref[seq, h] = s_new

    nw = nw_ref[...]
    for h in range(nh):
        o = _cat(o_rows[h], 0)
        o = o * lax.rsqrt(lane_sum(o * o) * (1.0 / hd) + NORM_EPS) * nw
        o_ref[:, h * hd:(h + 1) * hd] = o * _silu(z_ref[:, h * hd:(h + 1) * hd])

    if carry_mode:
        for h in range(nh):
            s_scr[h] = states[h]

        @pl.when(t_idx == pl.num_programs(1) - 1)
        def _():
            for h in range(nh):
                s_out_ref[h] = states[h]


def _gdn_mixer(qkv, z, ba, conv_w, gparams, norm_w, halo, s_in, *, rows, chunk, seq_len, carry_mode):
    bsz, t_len, _ = qkv.shape
    grid = (bsz, t_len // rows)
    tok = lambda width: pl.BlockSpec((None, rows, width), lambda b, t: (b, t, 0))
    const = lambda shape: pl.BlockSpec(shape, lambda b, t: (0,) * len(shape))
    in_specs = [tok(GDN_QKV), tok(GDN_WIDTH), tok(BA_PAD), const((GDN_CONV, GDN_QKV)), const((2, BA_PAD)),
                const((1, GDN_HEAD_DIM))]
    args = [qkv, z, ba, conv_w, gparams, norm_w]
    state_tail = (GDN_HEADS, GDN_HEAD_DIM, GDN_HEAD_DIM)
    if carry_mode:
        in_specs.append(pl.BlockSpec((None, SUBLANES, GDN_QKV), lambda b, t: (b, 0, 0)))
        args.append(halo)
        state_spec = pl.BlockSpec((None,) + state_tail, lambda b, t: (b, 0, 0, 0))
        scratch = [pltpu.VMEM(state_tail, F32), pltpu.VMEM((SUBLANES, GDN_QKV), F32)]
    else:
        in_specs += [tok(GDN_QKV)] * (GDN_CONV - 1)
        args += list(halo)
        state_spec = pl.BlockSpec((rows // seq_len,) + state_tail, lambda b, t: (t, 0, 0, 0))
        scratch = []
    in_specs.append(state_spec)
    args.append(s_in)
    return pl.pallas_call(
        functools.partial(_gdn_kernel, rows=rows, chunk=chunk, seq_len=seq_len, carry_mode=carry_mode),
        grid=grid,
        in_specs=in_specs,
        out_specs=[tok(GDN_WIDTH), state_spec],
        out_shape=[jax.ShapeDtypeStruct((bsz, t_len, GDN_WIDTH), F32),
                   jax.ShapeDtypeStruct(s_in.shape, F32)],
        scratch_shapes=scratch,
        compiler_params=pltpu.CompilerParams(dimension_semantics=("arbitrary",) * 2,
                                             vmem_limit_bytes=VMEM_LIMIT_BYTES),
        name="gdn_mixer",
    )(*args)


def _rwkv_kernel(*refs, rows, chunk, seq_len, carry_mode):
    (rkv_ref, lora_ref, mur_ref, mul_ref, par_ref, w2_ref, a2_ref, g2_ref, hr_ref, hl_ref, s_in_ref,
     o_ref, s_out_ref) = refs[:13]
    scratch = refs[13:]
    hd, npair, c, wd_all = RWKV_HEAD_DIM, RWKV_PAIRS, chunk, RWKV_WIDTH
    pw = 2 * hd
    n_chunks = rows // chunk
    n_seg = chunk // seq_len
    t_idx = pl.program_id(1)

    if carry_mode:
        s_scr, cr_scr, cl_scr = scratch

        @pl.when(t_idx == 0)
        def _():
            s_scr[...] = s_in_ref[...]
            cr_scr[...] = hr_ref[...]
            cl_scr[...] = hl_ref[...]

    def token_shift(x_ref, mu_ref, h_ref, c_scr):
        x = x_ref[...]
        if carry_mode:
            prev = _shift_rows_carry(x, 1, c_scr[...])
            c_scr[...] = x[rows - SUBLANES:]
        else:
            prev = _shift_rows_fix(x, 1, h_ref[...], seq_len)
        return x + mu_ref[...] * (prev - x)

    xs = token_shift(rkv_ref, mur_ref, hr_ref, cr_scr if carry_mode else None)
    lora = token_shift(lora_ref, mul_ref, hl_ref, cl_scr if carry_mode else None)
    r, kr, vr = xs[:, :wd_all], xs[:, wd_all:2 * wd_all], xs[:, 2 * wd_all:]
    par = par_ref[...]
    w0, a0, k_k, k_a, r_k, ln_w, ln_b = (par[n:n + 1] for n in range(7))
    wd = lora[:, :DECAY_LORA]
    ad = lora[:, DECAY_LORA:DECAY_LORA + AAA_LORA]
    gd = lora[:, DECAY_LORA + AAA_LORA:]
    lw = -(DECAY_SCALE * _sigmoid(w0 + _mm(jnp.tanh(wd), w2_ref[...])))
    asig = _sigmoid(a0 + _mm(ad, a2_ref[...]))
    g = _mm(_sigmoid(gd), g2_ref[...])

    li = _iota((pw, pw), 0)
    lj = _iota((pw, pw), 1)
    pair_bd = (li >= hd) == (lj >= hd)
    head_ones = pair_bd.astype(BF16)

    def head_sum(a):
        return _cat([_mm(a[:, p * pw:(p + 1) * pw], head_ones) for p in range(a.shape[1] // pw)], 1)

    kk = kr * k_k
    kk = kk * lax.rsqrt(head_sum(kk * kk) + L2_EPS)
    kr = kr * (1.0 + (asig - 1.0) * k_a)
    a_vec = -kk
    b_vec = kk * asig

    i1 = _iota((c, c), 0)
    j1 = _iota((c, c), 1)
    same1 = (i1 ^ j1) < seq_len
    incl1_bf = (same1 & (j1 <= i1)).astype(BF16)
    same1_bf = same1.astype(BF16)
    ip = _iota((c, 2 * c), 0)
    jp_full = _iota((c, 2 * c), 1)
    jp = jp_full & (c - 1)
    head1_cols = jp_full >= c
    same_p = (ip ^ jp) < seq_len
    incl_p = same_p & (jp <= ip)
    strict_p = same_p & (jp < ip)

    def stack_heads(a):
        head1 = (_iota(a.shape, 1) & hd) != 0
        return jnp.concatenate([jnp.where(head1, 0.0, a), jnp.where(head1, a, 0.0)], axis=0)

    def block_diag(m):
        return jnp.concatenate([jnp.where(head1_cols, 0.0, m), jnp.where(head1_cols, m, 0.0)], axis=0)

    pair_mul = lambda a, b: _mm_acc(a, block_diag(b))

    probs = []
    for ci in range(n_chunks):
        rs = slice(ci * c, (ci + 1) * c)
        lwc = lw[rs]
        g_incl = _mm_mask_lhs(incl1_bf, lwc)
        g_mid = 0.5 * _mm_mask_lhs(same1_bf, lwc)
        e_mid = jnp.exp(g_mid)
        inv_n = jnp.exp(g_mid - g_incl)
        r_n = r[rs] * jnp.exp(g_incl - g_mid)
        a_n = a_vec[rs] * jnp.exp(g_incl - lwc - g_mid)
        to_end = inv_n * e_mid
        full = dict(r_n=r_n, a_n=a_n, b_n=b_vec[rs] * inv_n, k_n=kr[rs] * inv_n, r_t=r_n * e_mid,
                    a_t=a_n * e_mid, b_end=b_vec[rs] * to_end, k_end=kr[rs] * to_end,
                    gam=e_mid * e_mid, v=vr[rs])
        for p in range(npair):
            prob = {name: val[:, p * pw:(p + 1) * pw] for name, val in full.items()}
            prob.update(ci=ci, p=p)
            probs.append(prob)

    ars = [jnp.concatenate([p["a_n"], p["r_n"]], axis=0) for p in probs]
    m_bks = [_mm(ar, jnp.concatenate([stack_heads(p["b_n"]), stack_heads(p["k_n"])], axis=0), NT)
             for ar, p in zip(ars, probs)]
    a_abs = [jnp.where(strict_p, m[:c, :2 * c], 0.0) for m in m_bks]
    a_rbs = [jnp.where(incl_p, m[c:, :2 * c], 0.0) for m in m_bks]
    a_aks = [jnp.where(strict_p, m[:c, 2 * c:], 0.0) for m in m_bks]
    a_rks = [jnp.where(incl_p, m[c:, 2 * c:], 0.0) for m in m_bks]
    t_invs = _unit_lower_inverses(a_abs, ip, jp, seq_len, pair_mul)
    akvs = [_mm(jnp.concatenate([a_ak, a_rk], axis=0), stack_heads(p["v"]))
            for a_ak, a_rk, p in zip(a_aks, a_rks, probs)]
    uws = [_mm_acc(t, stack_heads(jnp.concatenate([akv[:c], p["a_t"]], axis=1)))
           for t, akv, p in zip(t_invs, akvs, probs)]
    rbs = [_mm(a_rb, stack_heads(uw)) for a_rb, uw in zip(a_rbs, uws)]
    y_frees = [rb[:, :pw] + akv[c:] for rb, akv in zip(rbs, akvs)]
    r_effs = [p["r_t"] + rb[:, pw:] for p, rb in zip(probs, rbs)]
    m_corrs, n_adds = [], []
    for p, uw in zip(probs, uws):
        ms, ns = [], []
        for n in range(n_seg):
            sr = slice(n * seq_len, (n + 1) * seq_len)
            zero = jnp.zeros((seq_len, pw), F32)
            lhs = jnp.concatenate([jnp.concatenate([uw[sr, pw:], uw[sr, :pw]], axis=1),
                                   jnp.concatenate([zero, p["v"][sr]], axis=1)], axis=0)
            rhs = jnp.concatenate([p["b_end"][sr], p["k_end"][sr]], axis=0)
            mn = _mm_acc(lhs, rhs, TN)
            ms.append(jnp.where(pair_bd, mn[:pw], 0.0))
            ns.append(jnp.where(pair_bd, mn[pw:], 0.0))
        m_corrs.append(ms)
        n_adds.append(ns)

    states = [s_scr[p] for p in range(npair)] if carry_mode else None
    y_rows = [[] for _ in range(npair)]
    for idx, prob in enumerate(probs):
        p, ci = prob["p"], prob["ci"]
        for n in range(n_seg):
            sr = slice(n * seq_len, (n + 1) * seq_len)
            seq = ci * n_seg + n
            s = states[p] if carry_mode else s_in_ref[seq, p]
            y_rows[p].append(_mm_acc(r_effs[idx][sr], s, NT) + y_frees[idx][sr])
            gam = prob["gam"][n * seq_len:n * seq_len + 1]
            s_new = s * gam + _mm_acc(s, m_corrs[idx][n]) + n_adds[idx][n]
            if carry_mode:
                states[p] = s_new
            else:
                s_out_ref[seq, p] = s_new

    y = _cat([_cat(y_rows[p], 0) for p in range(npair)], 1)
    y_c = y - head_sum(y) * (1.0 / hd)
    y_n = y_c * lax.rsqrt(head_sum(y_c * y_c) * (1.0 / hd) + RWKV_LN_EPS)
    y_n = y_n * ln_w + ln_b
    bonus = head_sum(r * kr * r_k) * vr
    o_ref[...] = (y_n + bonus) * g

    if carry_mode:
        for p in range(npair):
            s_scr[p] = states[p]

        @pl.when(t_idx == pl.num_programs(1) - 1)
        def _():
            for p in range(npair):
                s_out_ref[p] = states[p]


def _rwkv_mixer(rkv, lora, mu_rkv, mu_lora, params, w2, a2, g2, halo_rkv, halo_lora, s_in, *,
                rows, chunk, seq_len, carry_mode):
    bsz, t_len, _ = rkv.shape
    grid = (bsz, t_len // rows)
    tok = lambda width: pl.BlockSpec((None, rows, width), lambda b, t: (b, t, 0))
    const = lambda shape: pl.BlockSpec(shape, lambda b, t: (0,) * len(shape))
    in_specs = [tok(3 * RWKV_WIDTH), tok(LORA_PAD), const((1, 3 * RWKV_WIDTH)), const((1, LORA_PAD)),
                const((SUBLANES, RWKV_WIDTH)), const((DECAY_LORA, RWKV_WIDTH)), const((AAA_LORA, RWKV_WIDTH)),
                const((LORA_PAD - DECAY_LORA - AAA_LORA, RWKV_WIDTH))]
    state_tail = (RWKV_PAIRS, LANES, LANES)
    if carry_mode:
        in_specs += [pl.BlockSpec((None, SUBLANES, 3 * RWKV_WIDTH), lambda b, t: (b, 0, 0)),
                     pl.BlockSpec((None, SUBLANES, LORA_PAD), lambda b, t: (b, 0, 0))]
        state_spec = pl.BlockSpec((None,) + state_tail, lambda b, t: (b, 0, 0, 0))
        scratch = [pltpu.VMEM(state_tail, F32), pltpu.VMEM((SUBLANES, 3 * RWKV_WIDTH), F32),
                   pltpu.VMEM((SUBLANES, LORA_PAD), F32)]
    else:
        in_specs += [tok(3 * RWKV_WIDTH), tok(LORA_PAD)]
        state_spec = pl.BlockSpec((rows // seq_len,) + state_tail, lambda b, t: (t, 0, 0, 0))
        scratch = []
    in_specs.append(state_spec)
    args = [rkv, lora, mu_rkv, mu_lora, params, w2, a2, g2, halo_rkv, halo_lora, s_in]
    return pl.pallas_call(
        functools.partial(_rwkv_kernel, rows=rows, chunk=chunk, seq_len=seq_len, carry_mode=carry_mode),
        grid=grid,
        in_specs=in_specs,
        out_specs=[tok(RWKV_WIDTH), state_spec],
        out_shape=[jax.ShapeDtypeStruct((bsz, t_len, RWKV_WIDTH), F32),
                   jax.ShapeDtypeStruct(s_in.shape, F32)],
        scratch_shapes=scratch,
        compiler_params=pltpu.CompilerParams(dimension_semantics=("arbitrary",) * 2,
                                             vmem_limit_bytes=VMEM_LIMIT_BYTES),
        name="rwkv_mixer",
    )(*args)


def _ffn_kernel(*refs, rows, seq_len, blocks_per_seq, carry_mode):
    x_ref, og_ref, orw_ref, wo_ref, nf_ref, wu_ref, cw_ref, wd_ref, nfin_ref = refs[:9]
    if carry_mode:
        h_ref, y_ref, tail_ref, carry_scr = refs[9:]
    else:
        f1_ref, f2_ref, y_ref, tail_ref = refs[9:]
    blk = pl.program_id(0)

    if carry_mode:
        @pl.when(blk % blocks_per_seq == 0)
        def _():
            carry_scr[...] = h_ref[...]

    o = (jnp.dot(og_ref[...].astype(BF16), wo_ref[:GDN_WIDTH], preferred_element_type=F32)
         + jnp.dot(orw_ref[...].astype(BF16), wo_ref[GDN_WIDTH:], preferred_element_type=F32))
    x1 = x_ref[...] + o
    h2 = _rms_norm(x1, nf_ref[...]).astype(BF16)
    cw = cw_ref[...]
    x2 = x1
    offs = [sum(FFN_COL_CHUNKS[:n]) for n in range(len(FFN_COL_CHUNKS))]

    def up_proj(n):
        lo, hi = offs[n], offs[n] + FFN_COL_CHUNKS[n]
        return (jnp.dot(h2, wu_ref[:, lo:hi], preferred_element_type=F32),
                jnp.dot(h2, wu_ref[:, FFN_DIM + lo:FFN_DIM + hi], preferred_element_type=F32))

    nxt = up_proj(0)
    for n, width in enumerate(FFN_COL_CHUNKS):
        cols = slice(offs[n], offs[n] + width)
        gate, val = nxt
        if n + 1 < len(FFN_COL_CHUNKS):
            nxt = up_proj(n + 1)
        conv = gate * cw[FFN_CONV - 1:FFN_CONV, cols]
        for s in range(1, FFN_CONV):
            if carry_mode:
                gs = _shift_rows_carry(gate, s, carry_scr[:, cols])
            else:
                gs = _shift_rows_fix(gate, s, (f1_ref, f2_ref)[s - 1][:, cols], seq_len)
            conv = conv + gs * cw[FFN_CONV - 1 - s:FFN_CONV - s, cols]
        if carry_mode:
            carry_scr[:, cols] = gate[rows - SUBLANES:]
            tail_ref[:, cols] = gate[rows - SUBLANES:]
        else:
            tail_ref[:, cols] = gate
        act = (_silu(conv) * val).astype(BF16)
        x2 = x2 + jnp.dot(act, wd_ref[cols, :], preferred_element_type=F32)
    y_ref[...] = _rms_norm(x2, nfin_ref[...])


def _ffn(x2d, o_gdn, o_rwkv, w_out, norm_ffn, w_up, conv_w, w_down, norm_final, halo, *,
         rows, seq_len, carry_mode):
    n_rows = x2d.shape[0]
    n_blocks = n_rows // rows
    row_spec = lambda w: pl.BlockSpec((rows, w), lambda i: (i, 0))
    whole = lambda shape: pl.BlockSpec(shape, lambda i: (0,) * len(shape), pipeline_mode=pl.Buffered(1))
    in_specs = [row_spec(D_MODEL), row_spec(GDN_WIDTH), row_spec(RWKV_WIDTH),
                whole((D_MODEL, D_MODEL)), whole((1, D_MODEL)), whole((D_MODEL, 2 * FFN_DIM)),
                whole((FFN_CONV, FFN_DIM)), whole((FFN_DIM, D_MODEL)), whole((1, D_MODEL))]
    args = [x2d, o_gdn, o_rwkv, w_out, norm_ffn, w_up, conv_w, w_down, norm_final]
    if carry_mode:
        blocks_per_seq = seq_len // rows
        in_specs.append(pl.BlockSpec((None, SUBLANES, FFN_DIM), lambda i: (i // blocks_per_seq, 0, 0)))
        args.append(halo)
        tail_spec = pl.BlockSpec((None, SUBLANES, FFN_DIM), lambda i: (i, 0, 0))
        tail_shape = jax.ShapeDtypeStruct((n_blocks, SUBLANES, FFN_DIM), F32)
        scratch = [pltpu.VMEM((SUBLANES, FFN_DIM), F32)]
    else:
        blocks_per_seq = 1
        in_specs += [row_spec(FFN_DIM), row_spec(FFN_DIM)]
        args += [halo[0], halo[1]]
        tail_spec = row_spec(FFN_DIM)
        tail_shape = jax.ShapeDtypeStruct((n_rows, FFN_DIM), F32)
        scratch = []
    return pl.pallas_call(
        functools.partial(_ffn_kernel, rows=rows, seq_len=seq_len, blocks_per_seq=blocks_per_seq,
                          carry_mode=carry_mode),
        grid=(n_blocks,),
        in_specs=in_specs,
        out_specs=[row_spec(D_MODEL), tail_spec],
        out_shape=[jax.ShapeDtypeStruct((n_rows, D_MODEL), F32), tail_shape],
        scratch_shapes=scratch,
        compiler_params=pltpu.CompilerParams(dimension_semantics=("arbitrary",),
                                             vmem_limit_bytes=VMEM_LIMIT_BYTES),
        name="ffn",
    )(*args)


def _pad_cols(a, width):
    return jnp.pad(a, ((0, 0), (0, width - a.shape[1])))


def _regroup_in_weights(w_in):
    g_end = GDN_QKV + GDN_WIDTH
    ba = w_in[:, g_end:g_end + 2 * GDN_HEADS]
    rw = w_in[:, g_end + 2 * GDN_HEADS:]
    return jnp.concatenate([w_in[:, :g_end], rw[:, :3 * RWKV_WIDTH],
                            _pad_cols(rw[:, 3 * RWKV_WIDTH:], LORA_PAD), _pad_cols(ba, BA_PAD)],
                           axis=1).astype(BF16)


def _tail_tile(state_rows):
    return jnp.pad(state_rows, ((0, 0), (SUBLANES - state_rows.shape[1], 0), (0, 0)))


def _fix_rows(state_rows, s, seq_len):
    n = state_rows.shape[1]
    first = state_rows[:, n - s:, :]
    return jnp.pad(first, ((0, 0), (0, seq_len - s), (0, 0))).reshape(-1, state_rows.shape[2])


def _pair_states(s):
    bsz = s.shape[0]
    sp = s.reshape(bsz, RWKV_PAIRS, 2, RWKV_HEAD_DIM, RWKV_HEAD_DIM)
    zero = jnp.zeros_like(sp[:, :, 0])
    top = jnp.concatenate([sp[:, :, 0], zero], axis=-1)
    bot = jnp.concatenate([zero, sp[:, :, 1]], axis=-1)
    return jnp.concatenate([top, bot], axis=-2)


def _unpair_states(sp):
    hd = RWKV_HEAD_DIM
    bsz = sp.shape[0]
    return jnp.stack([sp[:, :, :hd, :hd], sp[:, :, hd:, hd:]], axis=2).reshape(bsz, RWKV_HEADS, hd, hd)


def _layer(x, st_gdn_conv, st_gdn, st_shift, st_rwkv, st_ffn, wts, *, carry_mode):
    bsz, t_len, _ = x.shape
    n_rows = bsz * t_len
    x2d = x.reshape(n_rows, D_MODEL)
    qkv, z, rkv, lora, ba = _inproj(x2d, wts["norm_mix"], wts["w_in"])

    shift_rkv = st_shift[:, :, :3 * RWKV_WIDTH]
    shift_lora = _pad_cols(st_shift[:, 0, 3 * RWKV_WIDTH:], LORA_PAD)[:, None, :]
    if carry_mode:
        shape3 = lambda a: a.reshape(bsz, t_len, a.shape[-1])
        o_gdn, s_gdn = _gdn_mixer(shape3(qkv), shape3(z), shape3(ba), wts["gdn_conv_w"], wts["gparams"],
                                  wts["gdn_norm"], _tail_tile(st_gdn_conv), st_gdn,
                                  rows=MIXER_ROWS, chunk=GDN_CHUNK, seq_len=GDN_CHUNK, carry_mode=True)
        o_rwkv, s_rwkv = _rwkv_mixer(shape3(rkv), shape3(lora), wts["mu_rkv"], wts["mu_lora"], wts["rparams"],
                                     wts["w2"], wts["a2"], wts["g2"], _tail_tile(shift_rkv),
                                     _tail_tile(shift_lora), _pair_states(st_rwkv),
                                     rows=MIXER_ROWS, chunk=RWKV_CHUNK, seq_len=RWKV_CHUNK, carry_mode=True)
        ffn_halo = _tail_tile(st_ffn)
        ffn_rows, ffn_seq = ROW_BLOCK, t_len
    else:
        shape3 = lambda a: a.reshape(1, n_rows, a.shape[-1])
        fix = lambda st, s: _fix_rows(st, s, t_len)[None]
        o_gdn, s_gdn = _gdn_mixer(shape3(qkv), shape3(z), shape3(ba), wts["gdn_conv_w"], wts["gparams"],
                                  wts["gdn_norm"], [fix(st_gdn_conv, s) for s in range(1, GDN_CONV)], st_gdn,
                                  rows=n_rows, chunk=GDN_CHUNK, seq_len=t_len, carry_mode=False)
        o_rwkv, s_rwkv = _rwkv_mixer(shape3(rkv), shape3(lora), wts["mu_rkv"], wts["mu_lora"], wts["rparams"],
                                     wts["w2"], wts["a2"], wts["g2"], fix(shift_rkv, 1), fix(shift_lora, 1),
                                     _pair_states(st_rwkv),
                                     rows=n_rows, chunk=RWKV_CHUNK, seq_len=t_len, carry_mode=False)
        ffn_halo = [_fix_rows(st_ffn, s, t_len) for s in range(1, FFN_CONV)]
        ffn_rows, ffn_seq = n_rows, t_len
    y2d, gate_tail = _ffn(x2d, o_gdn.reshape(n_rows, GDN_WIDTH), o_rwkv.reshape(n_rows, RWKV_WIDTH),
                          wts["w_out"], wts["norm_ffn"], wts["w_up"], wts["ffn_conv_w"], wts["w_down"],
                          wts["norm_final"], ffn_halo, rows=ffn_rows, seq_len=ffn_seq, carry_mode=carry_mode)

    qkv3 = qkv.reshape(bsz, t_len, GDN_QKV)
    gdn_conv_new = qkv3[:, t_len - (GDN_CONV - 1):, :]
    shift_new = jnp.concatenate([rkv.reshape(bsz, t_len, -1)[:, t_len - 1:, :],
                                 lora.reshape(bsz, t_len, -1)[:, t_len - 1:, :RWKV_PROJ - 3 * RWKV_WIDTH]], axis=-1)
    if carry_mode:
        blocks_per_seq = t_len // ROW_BLOCK
        tails = gate_tail.reshape(bsz, blocks_per_seq, SUBLANES, FFN_DIM)[:, -1]
        ffn_new = tails[:, SUBLANES - (FFN_CONV - 1):, :]
    else:
        ffn_new = gate_tail.reshape(bsz, t_len, FFN_DIM)[:, t_len - (FFN_CONV - 1):, :]
    return (y2d.reshape(bsz, t_len, D_MODEL), gdn_conv_new, s_gdn, shift_new, _unpair_states(s_rwkv), ffn_new)


def kernel(x_prompt, x_sample, state_gdn_conv, state_gdn, state_rwkv_shift, state_rwkv, state_ffn_conv,
           norm_mix, w_in, gdn_conv_w, gdn_a_log, gdn_dt_bias, gdn_norm,
           rwkv_mu, rwkv_w0, rwkv_w2, rwkv_a0, rwkv_a2, rwkv_g2, rwkv_k_k, rwkv_k_a, rwkv_r_k,
           rwkv_ln_w, rwkv_ln_b, w_out, norm_ffn, w_up, ffn_conv_w, w_down, norm_final):
    layer = 0
    gparams = jnp.zeros((2, BA_PAD), F32)
    gparams = gparams.at[0, GDN_HEADS:2 * GDN_HEADS].set(gdn_a_log[layer])
    gparams = gparams.at[1, GDN_HEADS:2 * GDN_HEADS].set(gdn_dt_bias[layer])
    mu = rwkv_mu[layer][None, :]
    rparams = jnp.stack([rwkv_w0[layer], rwkv_a0[layer], rwkv_k_k[layer], rwkv_k_a[layer],
                         rwkv_r_k[layer].reshape(-1), rwkv_ln_w[layer], rwkv_ln_b[layer],
                         jnp.zeros((RWKV_WIDTH,), F32)])
    gate_rows = LORA_PAD - DECAY_LORA - AAA_LORA
    wts = {
        "norm_mix": norm_mix[layer][None, :],
        "w_in": _regroup_in_weights(w_in[layer]),
        "gdn_conv_w": gdn_conv_w[layer],
        "gparams": gparams,
        "gdn_norm": gdn_norm[layer][None, :],
        "mu_rkv": mu[:, :3 * RWKV_WIDTH],
        "mu_lora": _pad_cols(mu[:, 3 * RWKV_WIDTH:], LORA_PAD),
        "rparams": rparams,
        "w2": rwkv_w2[layer].astype(BF16),
        "a2": rwkv_a2[layer].astype(BF16),
        "g2": jnp.pad(rwkv_g2[layer], ((0, gate_rows - GATE_LORA), (0, 0))).astype(BF16),
        "w_out": w_out[layer].astype(BF16),
        "norm_ffn": norm_ffn[layer][None, :],
        "w_up": w_up[layer].astype(BF16),
        "ffn_conv_w": ffn_conv_w[layer],
        "w_down": w_down[layer].astype(BF16),
        "norm_final": norm_final[None, :],
    }
    n_prompt = x_prompt.shape[0]
    zeros = lambda c: jnp.zeros((n_prompt,) + c.shape[2:], c.dtype)
    outs_p = _layer(x_prompt, zeros(state_gdn_conv), zeros(state_gdn), zeros(state_rwkv_shift),
                    zeros(state_rwkv), zeros(state_ffn_conv), wts, carry_mode=True)
    outs_s = _layer(x_sample, state_gdn_conv[layer], state_gdn[layer], state_rwkv_shift[layer],
                    state_rwkv[layer], state_ffn_conv[layer], wts, carry_mode=False)
    y_p, y_s = outs_p[0], outs_s[0]
    states = []
    for n in range(1, 6):
        states += [outs_p[n][None], outs_s[n][None]]
    return (y_p, y_s, *states)
```
